```python
import functools
import jax, jax.numpy as jnp
from jax import lax
import numpy as np

D_MODEL = 1024
BATCH = 4
SEQ = 8192
DEPTH = 1
DEC_BATCH = 128
DEC_SEQ = 1
PAST_LEN = 8192
PAGE_SIZE = 128

HEAD_DIM = 64
RWKV_HEADS = 8
SB_HEADS = 8
RWKV_WIDTH = RWKV_HEADS * HEAD_DIM
SB_WIDTH = SB_HEADS * HEAD_DIM
MIX_WIDTH = RWKV_WIDTH + SB_WIDTH
W_LORA = 64
A_LORA = 64
G_LORA = 128
RWKV_PROJ = 3 * RWKV_WIDTH + W_LORA + A_LORA + G_LORA
IN_COLS = RWKV_PROJ + 3 * SB_WIDTH
GN_EPS = HEAD_DIM * 1e-5
LN_EPS = 1e-5
SB_BLOCK = 128
SB_SCALE = HEAD_DIM ** -0.5
PEER_HEADS = 8
N_KEYS = 128
N_EXPERTS = N_KEYS * N_KEYS
PEER_QDIM = 128
PEER_HALF = PEER_QDIM // 2
PEER_TOPK = 16
PEER_BLOCK = 128
N_MOD = 6
DN_ALPHA = (2 * DEPTH) ** 0.25
DN_BETA = (8 * DEPTH) ** -0.25

kernel_name = "hymba_rwkv7_stickbreak_peer_step"


def _ln(x):
    xf = x.astype(jnp.float32)
    mu = xf.mean(-1, keepdims=True)
    var = jnp.square(xf - mu).mean(-1, keepdims=True)
    return (xf - mu) * lax.rsqrt(var + LN_EPS)


def _layer_norm(x, g, b):
    return (_ln(x) * g.astype(jnp.float32) + b.astype(jnp.float32)).astype(x.dtype)


def _modulate(x, shift, scale):
    y = _ln(x) * (1.0 + scale[:, None, :].astype(jnp.float32)) + shift[:, None, :].astype(jnp.float32)
    return y.astype(x.dtype)


def _heads(t, n):
    return t.reshape(t.shape[:-1] + (n, HEAD_DIM))


def _rwkv7(p, shift_prev, wkv0, mu_shift, w0, w_up, a0, a_up, g_up, k_k, k_a, r_k, lnx_g, lnx_b):
    B, T, _ = p.shape
    f32 = jnp.float32
    pf = p.astype(f32)
    prev = jnp.concatenate([shift_prev.astype(f32)[:, None, :], pf[:, :-1]], axis=1)
    pm = pf + (prev - pf) * mu_shift
    cuts = [RWKV_WIDTH, 2 * RWKV_WIDTH, 3 * RWKV_WIDTH,
            3 * RWKV_WIDTH + W_LORA, 3 * RWKV_WIDTH + W_LORA + A_LORA]
    r, k, v, dw, da, dg = jnp.split(pm, cuts, axis=-1)
    w = -jax.nn.softplus(-(w0 + jnp.tanh(dw) @ w_up)) - 0.5
    decay = jnp.exp(-jnp.exp(w))
    a = jax.nn.sigmoid(a0 + da @ a_up)
    g = jax.nn.sigmoid(dg) @ g_up
    kk = _heads(k * k_k, RWKV_HEADS)
    kk = kk * lax.rsqrt(jnp.maximum(jnp.sum(kk * kk, -1, keepdims=True), 1e-24))
    k = k * (1.0 + (a - 1.0) * k_a)
    r, k, v, decay, a = (_heads(t, RWKV_HEADS) for t in (r, k, v, decay, a))

    def step(S, inp):
        r_t, w_t, k_t, v_t, kk_t, a_t = inp
        sa = jnp.einsum('bhij,bhj->bhi', S, -kk_t)
        S = (S * w_t[:, :, None, :] + sa[..., None] * (kk_t * a_t)[:, :, None, :]
             + v_t[..., None] * k_t[:, :, None, :])
        return S, jnp.einsum('bhij,bhj->bhi', S, r_t)

    xs = tuple(jnp.moveaxis(t, 1, 0) for t in (r, decay, k, v, kk, a))
    S_T, y = lax.scan(step, wkv0.astype(f32), xs)
    y = jnp.moveaxis(y, 0, 1)
    mu = y.mean(-1, keepdims=True)
    var = jnp.square(y - mu).mean(-1, keepdims=True)
    yn = ((y - mu) * lax.rsqrt(var + GN_EPS)).reshape(B, T, RWKV_WIDTH) * lnx_g + lnx_b
    bonus = (jnp.sum(r * k * r_k, -1, keepdims=True) * v).reshape(B, T, RWKV_WIDTH)
    out = (yn + bonus) * g
    return out.astype(p.dtype), p[:, -1], S_T.astype(wkv0.dtype)


def _stick_break(z, mask):
    z = z.astype(jnp.float32)
    log_keep = jnp.where(mask, jax.nn.log_sigmoid(-z), 0.0)
    later = lax.cumsum(log_keep, axis=z.ndim - 1, reverse=True) - log_keep
    return jnp.where(mask, jnp.exp(jax.nn.log_sigmoid(z) + later), 0.0)


def _sb_prompt(q, k, v, sb_bias):
    B, T, H, d = q.shape
    nb = T // SB_BLOCK
    qb = q.reshape(B, nb, SB_BLOCK, H, d).swapaxes(0, 1)
    q_pos = jnp.arange(T, dtype=jnp.int32).reshape(nb, SB_BLOCK)
    k_pos = jnp.arange(T, dtype=jnp.int32)
    bias = sb_bias.astype(jnp.float32)[None, :, None, None]

    def block(args):
        qi, pi = args
        z = jnp.einsum('bqhd,bshd->bhqs', qi, k).astype(jnp.float32) * SB_SCALE + bias
        A = _stick_break(z, k_pos[None, :] < pi[:, None]).astype(v.dtype)
        return jnp.einsum('bhqs,bshd->bqhd', A, v)

    out = lax.map(block, (qb, q_pos))
    return out.swapaxes(0, 1).reshape(B, T, H * d)


def _sb_sample(q, k, v, sb_bias, cache_k, cache_v, page_table):
    DB, DS, H, d = q.shape
    past = page_table.shape[1] * cache_k.shape[1]
    kp = cache_k[page_table].reshape(DB, past, H, d)
    vp = cache_v[page_table].reshape(DB, past, H, d)
    z = jnp.concatenate([jnp.einsum('bqhd,bshd->bhqs', q, kp),
                         jnp.einsum('bqhd,bshd->bhqs', q, k)], axis=-1).astype(jnp.float32) * SB_SCALE
    z = z + sb_bias.astype(jnp.float32)[None, :, None, None]
    k_pos = jnp.arange(past + DS, dtype=jnp.int32)
    q_pos = past + jnp.arange(DS, dtype=jnp.int32)
    A = _stick_break(z, k_pos[None, :] < q_pos[:, None]).astype(v.dtype)
    out = (jnp.einsum('bhqs,bshd->bqhd', A[..., :past], vp)
           + jnp.einsum('bhqs,bshd->bqhd', A[..., past:], v))
    return out.reshape(DB, DS, H * d)


def _peer(h, w_pq, sub_keys, peer_u, peer_v):
    n, D = h.shape
    nb = -(-n // PEER_BLOCK)
    hp = jnp.pad(h, ((0, nb * PEER_BLOCK - n), (0, 0))).reshape(nb, PEER_BLOCK, D)

    def block(hb):
        q = (hb @ w_pq).reshape(PEER_BLOCK, PEER_HEADS, 2, PEER_HALF)
        s = jnp.einsum('thcd,hcnd->thcn', q, sub_keys).astype(jnp.float32)
        sv, si = lax.top_k(s, PEER_TOPK)
        cand = (sv[:, :, 0, :, None] + sv[:, :, 1, None, :]).reshape(PEER_BLOCK, PEER_HEADS, PEER_TOPK * PEER_TOPK)
        cidx = (si[:, :, 0, :, None] * N_KEYS + si[:, :, 1, None, :]).reshape(PEER_BLOCK, PEER_HEADS, PEER_TOPK * PEER_TOPK)
        top_s, top_p = lax.top_k(cand, PEER_TOPK)
        eidx = jnp.take_along_axis(cidx, top_p, axis=-1)
        gate = jax.nn.softmax(top_s, axis=-1)
        act = jax.nn.gelu(jnp.einsum('td,thkd->thk', hb, peer_u[eidx]).astype(jnp.float32), approximate=False)
        return jnp.einsum('thk,thkd->td', (gate * act).astype(hb.dtype), peer_v[eidx])

    return lax.map(block, hp).reshape(nb * PEER_BLOCK, D)[:n]


def _layer(x, c, shift_prev, wkv0, attend, w_cond, b_cond, w_in, rwkv_w, sb_bias, w_out,
           ln1_g, ln1_b, w_pq, sub_keys, peer_u, peer_v, ln2_g, ln2_b):
    B, T, D = x.shape
    sh1, sc1, g1, sh2, sc2, g2 = jnp.split(jax.nn.silu(c) @ w_cond + b_cond, N_MOD, axis=-1)
    h = _modulate(x, sh1, sc1)
    p = h @ w_in
    rwkv_out, shift_new, wkv_new = _rwkv7(p[..., :RWKV_PROJ], shift_prev, wkv0, *rwkv_w)
    q, k, v = (_heads(t, SB_HEADS) for t in jnp.split(p[..., RWKV_PROJ:], 3, axis=-1))
    sb_out = attend(q, k, v, sb_bias)
    mix = jnp.concatenate([rwkv_out, sb_out], axis=-1) @ w_out
    x = _layer_norm(DN_ALPHA * x + g1[:, None, :] * mix, ln1_g, ln1_b)
    h2 = _modulate(x, sh2, sc2)
    f = _peer(h2.reshape(B * T, D), w_pq, sub_keys, peer_u, peer_v).reshape(B, T, D)
    x = _layer_norm(DN_ALPHA * x + g2[:, None, :] * f, ln2_g, ln2_b)
    return x, k, v, wkv_new, shift_new


def setup_inputs(seed: int = 0) -> dict:
    key = jax.random.key(seed)
    ks = iter(jax.random.split(key, 48))
    f32 = jnp.float32

    def nrm(shape, s):
        return jax.random.normal(next(ks), shape, f32) * s

    n_pages = PAST_LEN // PAGE_SIZE
    n_used = DEC_BATCH * n_pages
    n_pool = n_used + max(1, n_used // 4)
    page_table = jax.random.permutation(next(ks), n_pool)[:n_used].astype(jnp.int32).reshape(DEC_BATCH, n_pages)
    L = DEPTH
    return {
        "x_prompt": nrm((BATCH, SEQ, D_MODEL), 1.0),
        "x_sample": nrm((DEC_BATCH, DEC_SEQ, D_MODEL), 1.0),
        "c_prompt": nrm((BATCH, D_MODEL), 1.0),
        "c_sample": nrm((DEC_BATCH, D_MODEL), 1.0),
        "cache_k": nrm((L, n_pool, PAGE_SIZE, SB_HEADS, HEAD_DIM), 1.0),
        "cache_v": nrm((L, n_pool, PAGE_SIZE, SB_HEADS, HEAD_DIM), 1.0),
        "page_table": page_table,
        "state_wkv": nrm((L, DEC_BATCH, RWKV_HEADS, HEAD_DIM, HEAD_DIM), 0.1),
        "state_shift": nrm((L, DEC_BATCH, RWKV_PROJ), 1.0),
        "w_cond": nrm((L, D_MODEL, N_MOD * D_MODEL), 0.5 * D_MODEL ** -0.5),
        "b_cond": nrm((L, N_MOD * D_MODEL), 0.02),
        "w_in": nrm((L, D_MODEL, IN_COLS), D_MODEL ** -0.5),
        "mu_shift": jax.random.uniform(next(ks), (L, RWKV_PROJ), f32, 0.0, 1.0),
        "w0": jax.random.uniform(next(ks), (L, RWKV_WIDTH), f32, -6.0, -1.0),
        "w_up": nrm((L, W_LORA, RWKV_WIDTH), 0.5 * W_LORA ** -0.5),
        "a0": nrm((L, RWKV_WIDTH), 0.1),
        "a_up": nrm((L, A_LORA, RWKV_WIDTH), 0.5 * A_LORA ** -0.5),
        "g_up": nrm((L, G_LORA, RWKV_WIDTH), G_LORA ** -0.5),
        "k_k": 0.85 + nrm((L, RWKV_WIDTH), 0.05),
        "k_a": 1.0 + nrm((L, RWKV_WIDTH), 0.05),
        "r_k": nrm((L, RWKV_HEADS, HEAD_DIM), 0.1),
        "lnx_g": 1.0 + nrm((L, RWKV_WIDTH), 0.05),
        "lnx_b": nrm((L, RWKV_WIDTH), 0.02),
        "sb_bias": jax.random.uniform(next(ks), (L, SB_HEADS), f32, -9.0, -7.0),
        "w_out": nrm((L, MIX_WIDTH, D_MODEL), DN_BETA * MIX_WIDTH ** -0.5),
        "ln1_g": 1.0 + nrm((L, D_MODEL), 0.05),
        "ln1_b": nrm((L, D_MODEL), 0.02),
        "w_pq": nrm((L, D_MODEL, PEER_HEADS * PEER_QDIM), D_MODEL ** -0.5),
        "sub_keys": nrm((L, PEER_HEADS, 2, N_KEYS, PEER_HALF), PEER_HALF ** -0.5),
        "peer_u": nrm((L, N_EXPERTS, D_MODEL), D_MODEL ** -0.5),
        "peer_v": nrm((L, N_EXPERTS, D_MODEL), DN_BETA * PEER_HEADS ** -0.5),
        "ln2_g": 1.0 + nrm((L, D_MODEL), 0.05),
        "ln2_b": nrm((L, D_MODEL), 0.02),
    }


def reference(x_prompt, x_sample, c_prompt, c_sample, cache_k, cache_v, page_table, state_wkv, state_shift,
              w_cond, b_cond, w_in, mu_shift, w0, w_up, a0, a_up, g_up, k_k, k_a, r_k, lnx_g, lnx_b,
              sb_bias, w_out, ln1_g, ln1_b, w_pq, sub_keys, peer_u, peer_v, ln2_g, ln2_b):
    yp, ys = x_prompt, x_sample
    kp_l, vp_l, ks_l, vs_l, wp_l, ws_l, sp_l, ss_l = [], [], [], [], [], [], [], []
    for l in range(DEPTH):
        rwkv_w = (mu_shift[l], w0[l], w_up[l], a0[l], a_up[l], g_up[l], k_k[l], k_a[l], r_k[l], lnx_g[l], lnx_b[l])
        shared = (w_cond[l], b_cond[l], w_in[l], rwkv_w, sb_bias[l], w_out[l], ln1_g[l], ln1_b[l],
                  w_pq[l], sub_keys[l], peer_u[l], peer_v[l], ln2_g[l], ln2_b[l])
        b = yp.shape[0]
        yp, kp, vp, wp, sp = _layer(
            yp, c_prompt, jnp.zeros((b, RWKV_PROJ), yp.dtype),
            jnp.zeros((b, RWKV_HEADS, HEAD_DIM, HEAD_DIM), state_wkv.dtype), _sb_prompt, *shared)
        attend_s = functools.partial(_sb_sample, cache_k=cache_k[l], cache_v=cache_v[l], page_table=page_table)
        ys, ks_, vs_, ws, ss = _layer(ys, c_sample, state_shift[l], state_wkv[l], attend_s, *shared)
        kp_l.append(kp); vp_l.append(vp); ks_l.append(ks_); vs_l.append(vs_)
        wp_l.append(wp); ws_l.append(ws); sp_l.append(sp); ss_l.append(ss)
    return (yp, ys, jnp.stack(kp_l), jnp.stack(vp_l), jnp.stack(ks_l), jnp.stack(vs_l),
            jnp.stack(wp_l), jnp.stack(ws_l), jnp.stack(sp_l), jnp.stack(ss_l))
```

```python
import functools

import jax
import jax.numpy as jnp
from jax import lax
from jax.experimental import pallas as pl
from jax.experimental.pallas import tpu as pltpu

F32 = jnp.float32
BF16 = jnp.bfloat16

D_MODEL = 1024
HEAD_DIM = 64
N_HEADS = 8
WIDTH = N_HEADS * HEAD_DIM
W_LORA, A_LORA, G_LORA = 64, 64, 128
RWKV_PROJ = 3 * WIDTH + W_LORA + A_LORA + G_LORA
IN_COLS = RWKV_PROJ + 3 * WIDTH
GN_EPS = HEAD_DIM * 1e-5
LN_EPS = 1e-5
SB_SCALE = HEAD_DIM ** -0.5
PAGE_SIZE = 128
PEER_HEADS = 8
N_KEYS = 128
PEER_TOPK = 16
PEER_SLOTS = PEER_HEADS * PEER_TOPK
N_MOD = 6
DEPTH = 1
DN_ALPHA = (2 * DEPTH) ** 0.25

VMEM_LIMIT = 56 * 1024 * 1024


def _cparams(*sem):
    return pltpu.CompilerParams(dimension_semantics=sem, vmem_limit_bytes=VMEM_LIMIT)


def _ln_rows(x):
    mu = jnp.mean(x, axis=-1, keepdims=True)
    xc = x - mu
    var = jnp.mean(xc * xc, axis=-1, keepdims=True)
    return xc * lax.rsqrt(var + LN_EPS)


def _split_bf16(x):
    hi = x.astype(BF16)
    lo = (x - hi.astype(F32)).astype(BF16)
    return hi, lo


def _dot(a, b):
    return jnp.dot(a, b, preferred_element_type=F32)


def _dot_nt(a, b):
    return lax.dot_general(a, b, (((1,), (1,)), ((), ())), preferred_element_type=F32)


def _dot2(x, w_bf16):
    hi, lo = _split_bf16(x)
    return _dot(hi, w_bf16) + _dot(lo, w_bf16)


def _cond_kernel(c_ref, w_ref, b_ref, o_ref):
    c = c_ref[...]
    s = c * jax.nn.sigmoid(c)
    o_ref[...] = jnp.dot(s, w_ref[...], preferred_element_type=F32,
                         precision=lax.Precision.HIGHEST) + b_ref[...]


def _cond(c, w_cond, b_cond):
    n, d = c.shape
    cols = w_cond.shape[1]
    bn = 1024
    return pl.pallas_call(
        _cond_kernel,
        grid=(cols // bn,),
        in_specs=[pl.BlockSpec((n, d), lambda j: (0, 0)),
                  pl.BlockSpec((d, bn), lambda j: (0, j)),
                  pl.BlockSpec((1, bn), lambda j: (0, j))],
        out_specs=pl.BlockSpec((n, bn), lambda j: (0, j)),
        out_shape=jax.ShapeDtypeStruct((n, cols), F32),
        compiler_params=_cparams("parallel"),
        name="cond",
    )(c, w_cond, b_cond.reshape(1, cols))


def _inproj_kernel(x_ref, sh_ref, sc_ref, w_ref, p_ref, q_ref, k_ref, v_ref, kb_ref, vb_ref):
    h = _ln_rows(x_ref[...]) * (1.0 + sc_ref[...]) + sh_ref[...]
    hb = h.astype(BF16)
    p_ref[...] = _dot(hb, w_ref[:, :RWKV_PROJ])
    q_ref[...] = _dot(hb, w_ref[:, RWKV_PROJ:RWKV_PROJ + WIDTH]).astype(BF16)
    k = _dot(hb, w_ref[:, RWKV_PROJ + WIDTH:RWKV_PROJ + 2 * WIDTH])
    v = _dot(hb, w_ref[:, RWKV_PROJ + 2 * WIDTH:])
    k_ref[...] = k
    v_ref[...] = v
    kb_ref[...] = k.astype(BF16)
    vb_ref[...] = v.astype(BF16)


def _inproj(x, shift, scale, w_in_bf16, rows_per_mod, block_rows):
    n, d = x.shape
    rb = block_rows
    if rows_per_mod == 1:
        mod_spec = pl.BlockSpec((rb, d), lambda i: (i, 0))
    else:
        per = rows_per_mod // rb
        shift = shift.reshape(-1, 1, d)
        scale = scale.reshape(-1, 1, d)
        mod_spec = pl.BlockSpec((None, 1, d), lambda i: (i // per, 0, 0))
    row = lambda c: pl.BlockSpec((rb, c), lambda i: (i, 0))
    return pl.pallas_call(
        _inproj_kernel,
        grid=(n // rb,),
        in_specs=[row(d), mod_spec, mod_spec,
                  pl.BlockSpec((d, IN_COLS), lambda i: (0, 0))],
        out_specs=[row(RWKV_PROJ), row(WIDTH), row(WIDTH), row(WIDTH), row(WIDTH), row(WIDTH)],
        out_shape=[jax.ShapeDtypeStruct((n, RWKV_PROJ), F32),
                   jax.ShapeDtypeStruct((n, WIDTH), BF16),
                   jax.ShapeDtypeStruct((n, WIDTH), F32),
                   jax.ShapeDtypeStruct((n, WIDTH), F32),
                   jax.ShapeDtypeStruct((n, WIDTH), BF16),
                   jax.ShapeDtypeStruct((n, WIDTH), BF16)],
        compiler_params=_cparams("parallel"),
        name="inproj",
    )(x, shift, scale, w_in_bf16)


def _seg_ones(n):
    i = jnp.arange(n) // HEAD_DIM
    return (i[:, None] == i[None, :]).astype(BF16)


def _softplus(u):
    return jnp.maximum(u, 0.0) + jnp.log(1.0 + jnp.exp(-jnp.abs(u)))


def _rwkv_pre_kernel(has_prev, p_ref, prev_ref, mu_ref, w0_ref, wup_ref, a0_ref, aup_ref, gup_ref,
                     kk_ref, ka_ref, rk_ref, seg_ref,
                     okk_ref, odec_ref, ob_ref, okm_ref, ov_ref, owr_ref, obr_ref, okr_ref, og_ref, obon_ref,
                     carry_ref):
    hi = lax.Precision.HIGHEST
    pf = p_ref[...]
    if has_prev:
        prev = prev_ref[...]
    else:
        tb = pl.program_id(1)
        first = jnp.where(tb == 0, prev_ref[...], carry_ref[...])
        rows = lax.broadcasted_iota(jnp.int32, pf.shape, 0)
        prev = jnp.where(rows == 0, first, pltpu.roll(pf, 1, axis=0))
        carry_ref[...] = pf[pf.shape[0] - 1:, :]
    pm = pf + (prev - pf) * mu_ref[...]
    r = pm[:, :WIDTH]
    k = pm[:, WIDTH:2 * WIDTH]
    v = pm[:, 2 * WIDTH:3 * WIDTH]
    dwa = pm[:, 3 * WIDTH:3 * WIDTH + W_LORA + A_LORA]
    dg = pm[:, 3 * WIDTH + W_LORA + A_LORA:]
    seg = seg_ref[...]
    w = -_softplus(-(w0_ref[...] + jnp.dot(jnp.tanh(dwa), wup_ref[...], precision=hi,
                                           preferred_element_type=F32))) - 0.5
    dec = jnp.exp(-jnp.exp(w))
    a = jax.nn.sigmoid(a0_ref[...] + jnp.dot(dwa, aup_ref[...], precision=hi, preferred_element_type=F32))
    g = jnp.dot(jax.nn.sigmoid(dg), gup_ref[...], precision=hi, preferred_element_type=F32)
    kkr = k * kk_ref[...]
    kk = kkr * lax.rsqrt(jnp.maximum(_dot2(kkr * kkr, seg), 1e-24))
    km = k * (1.0 + (a - 1.0) * ka_ref[...])
    b = kk * a
    okk_ref[...] = kk
    odec_ref[...] = dec
    ob_ref[...] = b
    okm_ref[...] = km
    ov_ref[...] = v
    owr_ref[...] = dec * r
    obr_ref[...] = _dot2(b * r, seg)
    okr_ref[...] = _dot2(km * r, seg)
    og_ref[...] = g
    obon_ref[...] = _dot2(r * km * rk_ref[...], seg) * v


def _rwkv_pre(p, prev, has_prev, prm, tb):
    bsz, t, _ = p.shape
    blk = lambda c: pl.BlockSpec((None, tb, c), lambda i, j: (i, j, 0))
    full = lambda a: pl.BlockSpec(a.shape, lambda i, j: (0,) * a.ndim)
    prev_spec = blk(RWKV_PROJ) if has_prev else pl.BlockSpec((None, 1, RWKV_PROJ), lambda i, j: (i, 0, 0))
    params = [prm["mu"], prm["w0"], prm["wup"], prm["a0"], prm["aup"], prm["gup"],
              prm["k_k"], prm["k_a"], prm["r_k"], prm["seg512"]]
    return pl.pallas_call(
        functools.partial(_rwkv_pre_kernel, has_prev),
        grid=(bsz, t // tb),
        in_specs=[blk(RWKV_PROJ), prev_spec] + [full(a) for a in params],
        out_specs=[blk(WIDTH)] * 10,
        out_shape=[jax.ShapeDtypeStruct((bsz, t, WIDTH), F32)] * 10,
        scratch_shapes=[pltpu.VMEM((1, RWKV_PROJ), F32)],
        compiler_params=_cparams("parallel", "arbitrary"),
        name="rwkv_pre",
    )(p, prev, *params)


def _rwkv_scan_kernel(nb_count, tb, kk_ref, dec_ref, b_ref, km_ref, v_ref, wr_ref, br_ref, kr_ref,
                      g_ref, bon_ref, s0_ref, lg_ref, lb_ref, seg256_ref, seg512_ref,
                      out_ref, st_ref, s_scr, y_scr):
    step_blk = pl.program_id(1)

    @pl.when(step_blk == 0)
    def _():
        s_scr[...] = s0_ref[...]

    shape = (HEAD_DIM, WIDTH)
    ident = (lax.broadcasted_iota(jnp.int32, shape, 1) & (HEAD_DIM - 1)) == lax.broadcasted_iota(jnp.int32, shape, 0)
    seg = seg256_ref[...]
    half = WIDTH // 2

    def segsum(lhs):
        return jnp.concatenate([_dot(lhs[:, :half], seg), _dot(lhs[:, half:], seg)], axis=1)

    def step(t, carry):
        for nb in range(nb_count):
            row = lambda ref: ref[nb, pl.ds(t, 1), :]
            s = s_scr[nb]
            p_hi, p_lo = _split_bf16(s * row(kk_ref))
            dv = jnp.where(ident, row(v_ref), 0.0).astype(BF16)
            pr = (s * row(wr_ref)).astype(BF16)
            res = segsum(jnp.concatenate([p_hi, p_lo, dv, pr], axis=0))
            sa = res[:HEAD_DIM] + res[HEAD_DIM:2 * HEAD_DIM]
            vcol = res[2 * HEAD_DIM:3 * HEAD_DIM]
            ycol = res[3 * HEAD_DIM:] - sa * row(br_ref) + vcol * row(kr_ref)
            s_scr[nb] = s * row(dec_ref) - sa * row(b_ref) + vcol * row(km_ref)
            y_scr[nb, pl.ds(t, 1), :] = jnp.sum(jnp.where(ident, ycol, 0.0), axis=0, keepdims=True)
        return carry

    lax.fori_loop(0, tb, step, 0)

    seg512 = seg512_ref[...]
    for nb in range(nb_count):
        y = y_scr[nb]
        mu = _dot2(y, seg512) * (1.0 / HEAD_DIM)
        yc = y - mu
        var = _dot2(yc * yc, seg512) * (1.0 / HEAD_DIM)
        yn = yc * lax.rsqrt(var + GN_EPS) * lg_ref[...] + lb_ref[...]
        out_ref[nb] = (yn + bon_ref[nb]) * g_ref[nb]

    @pl.when(step_blk == pl.num_programs(1) - 1)
    def _():
        st_ref[...] = s_scr[...]


def _rwkv_scan(pre, s0, prm, nb, tb):
    kk, dec, b, km, v, wr, br, kr, g, bon = pre
    bsz, t, _ = kk.shape
    blk = pl.BlockSpec((nb, tb, WIDTH), lambda i, j: (i, j, 0))
    sblk = pl.BlockSpec((nb, HEAD_DIM, WIDTH), lambda i, j: (i, 0, 0))
    full = lambda a: pl.BlockSpec(a.shape, lambda i, j: (0,) * a.ndim)
    params = [prm["lnx_g"], prm["lnx_b"], prm["seg256"], prm["seg512"]]
    return pl.pallas_call(
        functools.partial(_rwkv_scan_kernel, nb, tb),
        grid=(bsz // nb, t // tb),
        in_specs=[blk] * 10 + [sblk] + [full(a) for a in params],
        out_specs=[blk, sblk],
        out_shape=[jax.ShapeDtypeStruct((bsz, t, WIDTH), F32),
                   jax.ShapeDtypeStruct((bsz, HEAD_DIM, WIDTH), F32)],
        scratch_shapes=[pltpu.VMEM((nb, HEAD_DIM, WIDTH), F32), pltpu.VMEM((nb, tb, WIDTH), F32)],
        compiler_params=_cparams("parallel", "arbitrary"),
        name="rwkv_scan",
    )(kk, dec, b, km, v, wr, br, kr, g, bon, s0, *params)


def _rwkv_params(mu_shift, w0, w_up, a0, a_up, g_up, k_k, k_a, r_k, lnx_g, lnx_b):
    row = lambda a: a.reshape(1, -1).astype(F32)
    zeros = jnp.zeros((A_LORA, WIDTH), F32)
    return dict(mu=row(mu_shift), w0=row(w0), a0=row(a0), k_k=row(k_k), k_a=row(k_a), r_k=row(r_k),
                lnx_g=row(lnx_g), lnx_b=row(lnx_b), gup=g_up,
                wup=jnp.concatenate([w_up, zeros], axis=0), aup=jnp.concatenate([zeros, a_up], axis=0),
                seg256=_seg_ones(WIDTH // 2), seg512=_seg_ones(WIDTH))


def _state_to_rows(wkv):
    bsz = wkv.shape[0]
    return wkv.transpose(0, 2, 1, 3).reshape(bsz, HEAD_DIM, WIDTH)


def _rows_to_state(s):
    bsz = s.shape[0]
    return s.reshape(bsz, HEAD_DIM, N_HEADS, HEAD_DIM).transpose(0, 2, 1, 3)


def _rwkv(p, shift_prev, wkv0, prm, nb, tb):
    bsz, t, _ = p.shape
    if t == 1:
        pre = _rwkv_pre(p.reshape(1, bsz, RWKV_PROJ), shift_prev.reshape(1, bsz, RWKV_PROJ), True, prm, bsz)
        pre = [a.reshape(bsz, 1, WIDTH) for a in pre]
    else:
        pre = _rwkv_pre(p, shift_prev.reshape(bsz, 1, RWKV_PROJ), False, prm, tb)
    out, st = _rwkv_scan(pre, _state_to_rows(wkv0), prm, nb, min(tb, t))
    return out, _rows_to_state(st)


def _sb_tile(z, mask, carry, m_ge, ones_cols):
    lk = -_softplus(z)
    if mask is not None:
        lk = jnp.where(mask, lk, 0.0)
    lkb = lk.astype(BF16)
    inc = _dot(lkb, m_ge)
    bk = z.shape[1]
    later = inc - lk + jnp.concatenate([carry] * (bk // carry.shape[1]), axis=1)
    a = jnp.exp(z + lk + later)
    if mask is not None:
        a = jnp.where(mask, a, 0.0)
    return a.astype(BF16), carry + _dot(lkb, ones_cols)


def _sb_prompt_kernel(bq, bk, bias_ref, q_ref, k_ref, v_ref, mge_ref, ones_ref, o_ref):
    hp = pl.program_id(1)
    qi = pl.program_id(2)
    lane = lax.broadcasted_iota(jnp.int32, (1, 2 * HEAD_DIM), 1)
    first = lane < HEAD_DIM
    q2 = q_ref[...]
    zero = jnp.zeros_like(q2)
    q_heads = (jnp.where(first, q2, zero), jnp.where(first, zero, q2))
    biases = (bias_ref[2 * hp], bias_ref[2 * hp + 1])
    m_ge = mge_ref[...]
    ones_cols = ones_ref[...]
    rows = lax.broadcasted_iota(jnp.int32, (bq, bk), 0)
    cols = lax.broadcasted_iota(jnp.int32, (bq, bk), 1)
    diag_mask = cols < rows

    def tile(j, mask, state):
        acc, carries = state
        kblk = k_ref[pl.ds(pl.multiple_of(j * bk, bk), bk), :]
        vblk = v_ref[pl.ds(pl.multiple_of(j * bk, bk), bk), :]
        vzero = jnp.zeros_like(vblk)
        v_heads = (jnp.where(first, vblk, vzero), jnp.where(first, vzero, vblk))
        new_carries = []
        for e in range(2):
            z = _dot_nt(q_heads[e], kblk) * SB_SCALE + biases[e]
            a, c = _sb_tile(z, mask, carries[e], m_ge, ones_cols)
            acc = acc + _dot(a, v_heads[e])
            new_carries.append(c)
        return acc, tuple(new_carries)

    zc = jnp.zeros((bq, 2 * HEAD_DIM), F32)
    state = tile(qi, diag_mask, (zc, (zc, zc)))
    state = lax.fori_loop(0, qi, lambda i, s: tile(qi - 1 - i, None, s), state)
    o_ref[...] = state[0]


def _sb_prompt(q, k, v, sb_bias, bq):
    bsz, t, _ = q.shape
    pair = 2 * HEAD_DIM
    idx = jnp.arange(bq)
    m_ge = (idx[:, None] >= idx[None, :]).astype(BF16)
    ones_cols = jnp.ones((bq, pair), BF16)
    return pl.pallas_call(
        functools.partial(_sb_prompt_kernel, bq, bq),
        grid=(bsz, WIDTH // pair, t // bq),
        in_specs=[pl.BlockSpec(memory_space=pltpu.SMEM),
                  pl.BlockSpec((None, bq, pair), lambda b, h, i: (b, i, h)),
                  pl.BlockSpec((None, t, pair), lambda b, h, i: (b, 0, h)),
                  pl.BlockSpec((None, t, pair), lambda b, h, i: (b, 0, h)),
                  pl.BlockSpec((bq, bq), lambda b, h, i: (0, 0)),
                  pl.BlockSpec((bq, pair), lambda b, h, i: (0, 0))],
        out_specs=pl.BlockSpec((None, bq, pair), lambda b, h, i: (b, i, h)),
        out_shape=jax.ShapeDtypeStruct((bsz, t, WIDTH), F32),
        compiler_params=_cparams("parallel", "parallel", "arbitrary"),
        name="sb_prompt",
    )(sb_bias.astype(F32), q, k, v, m_ge, ones_cols)


def _sb_sample_kernel(pp, n_pages, pt_ref, q_ref, knew_ref, vnew_ref, bias_ref, mge_ref, ones_ref, *refs):
    k_refs, v_refs = refs[:pp], refs[pp:2 * pp]
    o_ref, carry_scr, acc_scr = refs[2 * pp:]
    g = pl.program_id(1)

    @pl.when(g == 0)
    def _():
        carry_scr[...] = jnp.zeros_like(carry_scr)
        acc_scr[...] = jnp.zeros_like(acc_scr)

    shape = (N_HEADS, WIDTH)
    own = (lax.broadcasted_iota(jnp.int32, shape, 1) // HEAD_DIM) == lax.broadcasted_iota(jnp.int32, shape, 0)
    q_rows = jnp.where(own, q_ref[...], 0.0)
    qb = q_rows.astype(BF16)
    bias = bias_ref[...]
    carry = carry_scr[...]
    acc = acc_scr[...]
    for u in range(pp):
        z = _dot_nt(qb, k_refs[u][...].astype(BF16)) * SB_SCALE + bias
        a, carry = _sb_tile(z, None, carry, mge_ref[...], ones_ref[...])
        acc = acc + _dot(a, v_refs[u][...].astype(BF16))
    carry_scr[...] = carry
    acc_scr[...] = acc

    @pl.when(g == pl.num_programs(1) - 1)
    def _():
        past = n_pages * PAGE_SIZE
        z_new = jnp.sum(q_rows * knew_ref[...], axis=1, keepdims=True) * SB_SCALE + bias[:, :1]
        a_new = jnp.where(past < past, jnp.exp(-_softplus(-z_new)), 0.0)
        total = acc + a_new * vnew_ref[...]
        o_ref[...] = jnp.sum(jnp.where(own, total, 0.0), axis=0, keepdims=True)


def _sb_sample(q, k_new, v_new, sb_bias, cache_k, cache_v, page_table, pp):
    bsz = q.shape[0]
    n_pages = page_table.shape[1]
    idx = jnp.arange(PAGE_SIZE)
    m_ge = (idx[:, None] >= idx[None, :]).astype(BF16)
    ones_cols = jnp.ones((PAGE_SIZE, PAGE_SIZE), BF16)
    bias = jnp.broadcast_to(sb_bias.astype(F32)[:, None], (N_HEADS, PAGE_SIZE))
    row = pl.BlockSpec((None, 1, WIDTH), lambda b, g, pt: (b, 0, 0))
    full = lambda a: pl.BlockSpec(a.shape, lambda b, g, pt: (0,) * a.ndim)

    def page_spec(u):
        return pl.BlockSpec((None, PAGE_SIZE, WIDTH),
                            lambda b, g, pt: (pt[b * n_pages + n_pages - 1 - (g * pp + u)], 0, 0))

    grid_spec = pltpu.PrefetchScalarGridSpec(
        num_scalar_prefetch=1,
        grid=(bsz, n_pages // pp),
        in_specs=[row, row, row, full(bias), full(m_ge), full(ones_cols)]
        + [page_spec(u) for u in range(pp)] * 2,
        out_specs=pl.BlockSpec((None, 1, WIDTH), lambda b, g, pt: (b, 0, 0)),
        scratch_shapes=[pltpu.VMEM((N_HEADS, PAGE_SIZE), F32), pltpu.VMEM((N_HEADS, WIDTH), F32)],
    )
    r3 = lambda a: a.reshape(bsz, 1, WIDTH)
    out = pl.pallas_call(
        functools.partial(_sb_sample_kernel, pp, n_pages),
        grid_spec=grid_spec,
        out_shape=jax.ShapeDtypeStruct((bsz, 1, WIDTH), F32),
        compiler_params=_cparams("parallel", "arbitrary"),
        name="sb_sample",
    )(page_table.reshape(-1), r3(q), r3(k_new), r3(v_new), bias, m_ge, ones_cols,
      *([cache_k] * pp), *([cache_v] * pp))
    return out.reshape(bsz, WIDTH)


def _outproj_kernel(rw_ref, sb_ref, x_ref, g1_ref, sh2_ref, sc2_ref, wo_ref, lg_ref, lb_ref, wpq_ref,
                    x1_ref, h2_ref, qp_ref):
    mix = _dot(rw_ref[...].astype(BF16), wo_ref[:WIDTH, :]) + _dot(sb_ref[...].astype(BF16), wo_ref[WIDTH:, :])
    x1 = _ln_rows(DN_ALPHA * x_ref[...] + g1_ref[...] * mix) * lg_ref[...] + lb_ref[...]
    h2 = _ln_rows(x1) * (1.0 + sc2_ref[...]) + sh2_ref[...]
    x1_ref[...] = x1
    h2_ref[...] = h2
    qp_ref[...] = _dot(h2.astype(BF16), wpq_ref[...]).astype(BF16)


def _mod_spec(a, rows_per_mod, rb, d):
    if rows_per_mod == 1:
        return a, pl.BlockSpec((rb, d), lambda i: (i, 0))
    per = rows_per_mod // rb
    return a.reshape(-1, 1, d), pl.BlockSpec((None, 1, d), lambda i: (i // per, 0, 0))


def _outproj(rw, sb, x, g1, sh2, sc2, w_out_bf16, ln_g, ln_b, w_pq_bf16, rows_per_mod, rb):
    n, d = x.shape
    row = lambda c: pl.BlockSpec((rb, c), lambda i: (i, 0))
    full = lambda a: pl.BlockSpec(a.shape, lambda i: (0,) * a.ndim)
    mods, mod_specs = zip(*[_mod_spec(a, rows_per_mod, rb, d) for a in (g1, sh2, sc2)])
    lg, lb = ln_g.reshape(1, d), ln_b.reshape(1, d)
    return pl.pallas_call(
        _outproj_kernel,
        grid=(n // rb,),
        in_specs=[row(WIDTH), row(WIDTH), row(d), *mod_specs, full(w_out_bf16), full(lg), full(lb), full(w_pq_bf16)],
        out_specs=[row(d), row(d), row(d)],
        out_shape=[jax.ShapeDtypeStruct((n, d), F32), jax.ShapeDtypeStruct((n, d), F32),
                   jax.ShapeDtypeStruct((n, d), BF16)],
        compiler_params=_cparams("parallel"),
        name="outproj",
    )(rw, sb, x, *mods, w_out_bf16, lg, lb, w_pq_bf16)


def _take_top(x, ids, payload, count):
    rows = x.shape[0]
    vals, picked = [], []
    for _ in range(count):
        m = jnp.max(x, axis=0, keepdims=True)
        pos = jnp.min(jnp.where(x == m, ids, rows), axis=0, keepdims=True)
        hit = ids == pos
        vals.append(m)
        picked.append(pos if payload is None else jnp.sum(jnp.where(hit, payload, 0), axis=0, keepdims=True))
        x = jnp.where(hit, -jnp.inf, x)
    return jnp.concatenate(vals, axis=0), jnp.concatenate(picked, axis=0)


def _route_kernel(qp_ref, sk_ref, e_ref, g_ref):
    tokens = qp_ref.shape[0]
    scores = _dot_nt(sk_ref[...], qp_ref[...])
    key_ids = lax.broadcasted_iota(jnp.int32, (N_KEYS, tokens), 0)
    sv0, si0 = _take_top(scores[:N_KEYS], key_ids, None, PEER_TOPK)
    sv1, si1 = _take_top(scores[N_KEYS:], key_ids, None, PEER_TOPK)
    cand = jnp.concatenate([sv0[a:a + 1] + sv1 for a in range(PEER_TOPK)], axis=0)
    cidx = jnp.concatenate([si0[a:a + 1] * N_KEYS + si1 for a in range(PEER_TOPK)], axis=0)
    cand_ids = lax.broadcasted_iota(jnp.int32, cand.shape, 0)
    top, eidx = _take_top(cand, cand_ids, cidx, PEER_TOPK)
    ex = jnp.exp(top - top[:1])
    e_ref[...] = eidx
    g_ref[...] = ex / jnp.sum(ex, axis=0, keepdims=True)


def _route(qp, sk_pairs, tb):
    n = qp.shape[0]
    blk = pl.BlockSpec((None, PEER_TOPK, tb), lambda i, h: (h, 0, i))
    return pl.pallas_call(
        _route_kernel,
        grid=(n // tb, PEER_HEADS),
        in_specs=[pl.BlockSpec((tb, 2 * HEAD_DIM), lambda i, h: (i, h)),
                  pl.BlockSpec((None, 2 * N_KEYS, 2 * HEAD_DIM), lambda i, h: (h, 0, 0))],
        out_specs=[blk, blk],
        out_shape=[jax.ShapeDtypeStruct((PEER_HEADS, PEER_TOPK, n), jnp.int32),
                   jax.ShapeDtypeStruct((PEER_HEADS, PEER_TOPK, n), F32)],
        compiler_params=_cparams("parallel", "parallel"),
        name="peer_route",
    )(qp, sk_pairs)


def _sub_key_pairs(sub_keys):
    z = jnp.zeros_like(sub_keys[:, 0])
    top = jnp.concatenate([sub_keys[:, 0], z], axis=-1)
    bot = jnp.concatenate([z, sub_keys[:, 1]], axis=-1)
    return jnp.concatenate([top, bot], axis=1).astype(BF16)


ROW_WORDS = D_MODEL // 2
ROW_PLANES = ROW_WORDS // 128
PLANE_STRIDE = PEER_SLOTS + 8


def _pack_table(w):
    bits = lax.bitcast_convert_type(w.astype(BF16), jnp.uint16).astype(jnp.uint32)
    packed = bits[:, :ROW_WORDS] | (bits[:, ROW_WORDS:] << 16)
    return lax.bitcast_convert_type(packed, jnp.int32).reshape(-1, 128)


def _gather_rows(idx_ref, base, tbl_ref, buf_ref):
    for m in range(PEER_SLOTS):
        start = pl.multiple_of(idx_ref[base + m], ROW_PLANES)
        buf_ref[pl.ds(m, ROW_PLANES, stride=PLANE_STRIDE), :] = tbl_ref[pl.ds(start, ROW_PLANES), :]


def _plane_halves(buf_ref, j):
    words = buf_ref[pl.ds(j * PLANE_STRIDE, PEER_SLOTS), :]
    lo = lax.bitcast_convert_type(words << 16, F32).astype(BF16)
    hi = lax.bitcast_convert_type(words & jnp.int32(-65536), F32).astype(BF16)
    return lo, hi


def _peer_act_kernel(tb, idx_ref, h_ref, tbl_ref, act_ref, buf_ref):
    def token(t, carry):
        _gather_rows(idx_ref, t * PEER_SLOTS, tbl_ref, buf_ref)
        hrow = h_ref[pl.ds(t, 1), :]
        acc = jnp.zeros((8, PEER_SLOTS), F32)
        for j in range(ROW_PLANES):
            lo, hi = _plane_halves(buf_ref, j)
            h_lo = jnp.broadcast_to(hrow[:, j * 128:(j + 1) * 128], (8, 128)).astype(BF16)
            h_hi = jnp.broadcast_to(hrow[:, ROW_WORDS + j * 128:ROW_WORDS + (j + 1) * 128], (8, 128)).astype(BF16)
            acc = acc + _dot_nt(h_lo, lo) + _dot_nt(h_hi, hi)
        act_ref[pl.ds(t, 1), :] = acc[:1]
        return carry

    lax.fori_loop(0, tb, token, 0)


def _peer_out_kernel(tb, idx_ref, act_ref, gate_ref, tbl_ref, f_ref, buf_ref, coef_ref):
    act = act_ref[...]
    coef_ref[...] = gate_ref[...] * (0.5 * act * (1.0 + lax.erf(act * (2.0 ** -0.5))))

    def token(t, carry):
        _gather_rows(idx_ref, t * PEER_SLOTS, tbl_ref, buf_ref)
        coef = jnp.broadcast_to(coef_ref[pl.ds(t, 1), :], (8, PEER_SLOTS))
        los, his = [], []
        for j in range(ROW_PLANES):
            lo, hi = _plane_halves(buf_ref, j)
            los.append(_dot2(coef, lo)[:1])
            his.append(_dot2(coef, hi)[:1])
        f_ref[pl.ds(t, 1), :] = jnp.concatenate(los + his, axis=1)
        return carry

    lax.fori_loop(0, tb, token, 0)


def _peer_experts(h2, idx_flat, gate, tbl_u, tbl_v, tb):
    n, d = h2.shape
    smem = pl.BlockSpec((tb * PEER_SLOTS,), lambda i: (i,), memory_space=pltpu.SMEM)
    row = lambda c: pl.BlockSpec((tb, c), lambda i: (i, 0))
    table = pl.BlockSpec(memory_space=pltpu.VMEM)
    buf = pltpu.VMEM((ROW_PLANES * PLANE_STRIDE, 128), jnp.int32)
    act = pl.pallas_call(
        functools.partial(_peer_act_kernel, tb),
        grid=(n // tb,),
        in_specs=[smem, row(d), table],
        out_specs=row(PEER_SLOTS),
        out_shape=jax.ShapeDtypeStruct((n, PEER_SLOTS), F32),
        scratch_shapes=[buf],
        compiler_params=_cparams("arbitrary"),
        name="peer_act",
    )(idx_flat, h2, tbl_u)
    return pl.pallas_call(
        functools.partial(_peer_out_kernel, tb),
        grid=(n // tb,),
        in_specs=[smem, row(PEER_SLOTS), row(PEER_SLOTS), table],
        out_specs=row(d),
        out_shape=jax.ShapeDtypeStruct((n, d), F32),
        scratch_shapes=[buf, pltpu.VMEM((tb, PEER_SLOTS), F32)],
        compiler_params=_cparams("arbitrary"),
        name="peer_out",
    )(idx_flat, act, gate, tbl_v)


def _final_kernel(x1_ref, f_ref, g2_ref, lg_ref, lb_ref, y_ref):
    y_ref[...] = _ln_rows(DN_ALPHA * x1_ref[...] + g2_ref[...] * f_ref[...]) * lg_ref[...] + lb_ref[...]


def _final(x1, f, g2, ln_g, ln_b, rows_per_mod, rb):
    n, d = x1.shape
    row = pl.BlockSpec((rb, d), lambda i: (i, 0))
    g2, g2_spec = _mod_spec(g2, rows_per_mod, rb, d)
    vec = pl.BlockSpec((1, d), lambda i: (0, 0))
    return pl.pallas_call(
        _final_kernel,
        grid=(n // rb,),
        in_specs=[row, row, g2_spec, vec, vec],
        out_specs=row,
        out_shape=jax.ShapeDtypeStruct((n, d), F32),
        compiler_params=_cparams("parallel"),
        name="final_ln",
    )(x1, f, g2, ln_g.reshape(1, d), ln_b.reshape(1, d))


ROW_BLOCK = 256
SCAN_BATCH = 4
SCAN_BLOCK = 128
SB_BLOCK = 256
SB_PAGES = 8
ROUTE_BLOCK = 128
PEER_BLOCK = 64


def _mixer_half(x, mods, shift_prev, wkv0, attend, wts, rows_per_mod, rb):
    bsz, t, d = x.shape
    n = bsz * t
    sh1, sc1, g1, sh2, sc2, _ = mods
    xr = x.reshape(n, d)
    p, q, k, v, kb, vb = _inproj(xr, sh1, sc1, wts["w_in"], rows_per_mod, rb)
    p3 = p.reshape(bsz, t, RWKV_PROJ)
    rw, wkv_new = _rwkv(p3, shift_prev, wkv0, wts["rwkv"], SCAN_BATCH, SCAN_BLOCK)
    sb = attend(q, k, v, kb, vb)
    x1, h2, qp = _outproj(rw.reshape(n, WIDTH), sb.reshape(n, WIDTH), xr, g1, sh2, sc2, wts["w_out"],
                          wts["ln1_g"], wts["ln1_b"], wts["w_pq"], rows_per_mod, rb)
    heads = lambda a: a.reshape(bsz, t, N_HEADS, HEAD_DIM)
    return x1, h2, qp, heads(k), heads(v), wkv_new, p3[:, -1]


def _layer_pair(xp, xs, mod_p, mod_s, cache_k, cache_v, page_table, state_wkv, state_shift, wts):
    bp, tp, d = xp.shape
    bs, ts, _ = xs.shape
    assert ts == 1, "the paged attention handles one new token per sequence"
    n_p = bp * tp

    def attend_p(q, k, v, kb, vb):
        r = lambda a: a.reshape(bp, tp, WIDTH)
        return _sb_prompt(r(q), r(kb), r(vb), wts["sb_bias"], SB_BLOCK)

    def attend_s(q, k, v, kb, vb):
        ck = cache_k.reshape(cache_k.shape[0], PAGE_SIZE, WIDTH)
        cv = cache_v.reshape(cache_v.shape[0], PAGE_SIZE, WIDTH)
        return _sb_sample(q.astype(F32), k, v, wts["sb_bias"], ck, cv, page_table, SB_PAGES)

    zero_shift = jnp.zeros((bp, RWKV_PROJ), xp.dtype)
    zero_wkv = jnp.zeros((bp, N_HEADS, HEAD_DIM, HEAD_DIM), state_wkv.dtype)
    x1p, h2p, qpp, kp, vp, wp, sp = _mixer_half(xp, mod_p, zero_shift, zero_wkv, attend_p, wts, tp, ROW_BLOCK)
    x1s, h2s, qps, ks, vs, ws, ss = _mixer_half(xs, mod_s, state_shift, state_wkv, attend_s, wts, 1, bs)

    h2 = jnp.concatenate([h2p, h2s], axis=0)
    qp = jnp.concatenate([qpp, qps], axis=0)
    n = h2.shape[0]
    eidx, gate = _route(qp, wts["sub_keys"], ROUTE_BLOCK)
    idx_flat = eidx.transpose(2, 0, 1).reshape(-1) * ROW_PLANES
    gate = gate.transpose(2, 0, 1).reshape(n, PEER_SLOTS)
    f = _peer_experts(h2, idx_flat, gate, wts["peer_u"], wts["peer_v"], PEER_BLOCK)
    yp = _final(x1p, f[:n_p], mod_p[5], wts["ln2_g"], wts["ln2_b"], tp, ROW_BLOCK).reshape(bp, tp, d)
    ys = _final(x1s, f[n_p:], mod_s[5], wts["ln2_g"], wts["ln2_b"], 1, bs).reshape(bs, ts, d)
    return yp, ys, kp, vp, ks, vs, wp, ws, sp, ss


def kernel(x_prompt, x_sample, c_prompt, c_sample, cache_k, cache_v, page_table, state_wkv, state_shift,
           w_cond, b_cond, w_in, mu_shift, w0, w_up, a0, a_up, g_up, k_k, k_a, r_k, lnx_g, lnx_b,
           sb_bias, w_out, ln1_g, ln1_b, w_pq, sub_keys, peer_u, peer_v, ln2_g, ln2_b):
    depth = w_in.shape[0]
    bp, bs = c_prompt.shape[0], c_sample.shape[0]
    pad = (-(bp + bs)) % 8
    c_all = jnp.concatenate([c_prompt, c_sample, jnp.zeros((pad, c_prompt.shape[1]), c_prompt.dtype)], axis=0)
    yp, ys = x_prompt, x_sample
    outs = [[] for _ in range(8)]
    for l in range(depth):
        wts = dict(
            w_in=w_in[l].astype(BF16), w_out=w_out[l].astype(BF16), w_pq=w_pq[l].astype(BF16),
            rwkv=_rwkv_params(mu_shift[l], w0[l], w_up[l], a0[l], a_up[l], g_up[l], k_k[l], k_a[l], r_k[l],
                              lnx_g[l], lnx_b[l]),
            sb_bias=sb_bias[l], ln1_g=ln1_g[l], ln1_b=ln1_b[l], ln2_g=ln2_g[l], ln2_b=ln2_b[l],
            sub_keys=_sub_key_pairs(sub_keys[l]), peer_u=_pack_table(peer_u[l]), peer_v=_pack_table(peer_v[l]))
        mod = _cond(c_all, w_cond[l], b_cond[l])
        mod_p = jnp.split(mod[:bp], N_MOD, axis=-1)
        mod_s = jnp.split(mod[bp:bp + bs], N_MOD, axis=-1)
        res = _layer_pair(yp, ys, mod_p, mod_s, cache_k[l], cache_v[l], page_table, state_wkv[l], state_shift[l], wts)
        yp, ys = res[0], res[1]
        for acc, val in zip(outs, res[2:]):
            acc.append(val)
    return (yp, ys) + tuple(jnp.stack(o) for o in outs)
```

```python
import functools

import jax
import jax.numpy as jnp
from jax import lax
from jax.experimental import pallas as pl
from jax.experimental.pallas import tpu as pltpu

F32 = jnp.float32
BF16 = jnp.bfloat16

D_MODEL = 1024
HEAD_DIM = 64
N_HEADS = 8
WIDTH = N_HEADS * HEAD_DIM
W_LORA, A_LORA, G_LORA = 64, 64, 128
RWKV_PROJ = 3 * WIDTH + W_LORA + A_LORA + G_LORA
IN_COLS = RWKV_PROJ + 3 * WIDTH
GN_EPS = HEAD_DIM * 1e-5
LN_EPS = 1e-5
SB_SCALE = HEAD_DIM ** -0.5
PAGE_SIZE = 128
PEER_HEADS = 8
N_KEYS = 128
PEER_TOPK = 16
PEER_SLOTS = PEER_HEADS * PEER_TOPK
N_MOD = 6
DEPTH = 1
DN_ALPHA = (2 * DEPTH) ** 0.25

VMEM_LIMIT = 56 * 1024 * 1024


def _cparams(*sem):
    return pltpu.CompilerParams(dimension_semantics=sem, vmem_limit_bytes=VMEM_LIMIT)


def _ln_rows(x):
    mu = jnp.mean(x, axis=-1, keepdims=True)
    xc = x - mu
    var = jnp.mean(xc * xc, axis=-1, keepdims=True)
    return xc * lax.rsqrt(var + LN_EPS)


def _split_bf16(x):
    hi = x.astype(BF16)
    lo = (x - hi.astype(F32)).astype(BF16)
    return hi, lo


def _dot(a, b):
    return jnp.dot(a, b, preferred_element_type=F32)


def _dot_nt(a, b):
    return lax.dot_general(a, b, (((1,), (1,)), ((), ())), preferred_element_type=F32)


def _dot2(x, w_bf16):
    hi, lo = _split_bf16(x)
    return _dot(hi, w_bf16) + _dot(lo, w_bf16)


def _cond_kernel(c_ref, w_ref, b_ref, o_ref):
    c = c_ref[...]
    s = c * jax.nn.sigmoid(c)
    o_ref[...] = jnp.dot(s, w_ref[...], preferred_element_type=F32,
                         precision=lax.Precision.HIGHEST) + b_ref[...]


def _cond(c, w_cond, b_cond):
    n, d = c.shape
    cols = w_cond.shape[1]
    bn = 1024
    return pl.pallas_call(
        _cond_kernel,
        grid=(cols // bn,),
        in_specs=[pl.BlockSpec((n, d), lambda j: (0, 0)),
                  pl.BlockSpec((d, bn), lambda j: (0, j)),
                  pl.BlockSpec((1, bn), lambda j: (0, j))],
        out_specs=pl.BlockSpec((n, bn), lambda j: (0, j)),
        out_shape=jax.ShapeDtypeStruct((n, cols), F32),
        compiler_params=_cparams("parallel"),
        name="cond",
    )(c, w_cond, b_cond.reshape(1, cols))


def _inproj_kernel(x_ref, sh_ref, sc_ref, w_ref, p_ref, q_ref, k_ref, v_ref, kb_ref, vb_ref):
    h = _ln_rows(x_ref[...]) * (1.0 + sc_ref[...]) + sh_ref[...]
    hb = h.astype(BF16)
    p_ref[...] = _dot(hb, w_ref[:, :RWKV_PROJ])
    q_ref[...] = _dot(hb, w_ref[:, RWKV_PROJ:RWKV_PROJ + WIDTH]).astype(BF16)
    k = _dot(hb, w_ref[:, RWKV_PROJ + WIDTH:RWKV_PROJ + 2 * WIDTH])
    v = _dot(hb, w_ref[:, RWKV_PROJ + 2 * WIDTH:])
    k_ref[...] = k
    v_ref[...] = v
    kb_ref[...] = k.astype(BF16)
    vb_ref[...] = v.astype(BF16)


def _inproj(x, shift, scale, w_in_bf16, rows_per_mod, block_rows):
    n, d = x.shape
    rb = block_rows
    if rows_per_mod == 1:
        mod_spec = pl.BlockSpec((rb, d), lambda i: (i, 0))
    else:
        per = rows_per_mod // rb
        shift = shift.reshape(-1, 1, d)
        scale = scale.reshape(-1, 1, d)
        mod_spec = pl.BlockSpec((None, 1, d), lambda i: (i // per, 0, 0))
    row = lambda c: pl.BlockSpec((rb, c), lambda i: (i, 0))
    return pl.pallas_call(
        _inproj_kernel,
        grid=(n // rb,),
        in_specs=[row(d), mod_spec, mod_spec,
                  pl.BlockSpec((d, IN_COLS), lambda i: (0, 0))],
        out_specs=[row(RWKV_PROJ), row(WIDTH), row(WIDTH), row(WIDTH), row(WIDTH), row(WIDTH)],
        out_shape=[jax.ShapeDtypeStruct((n, RWKV_PROJ), F32),
                   jax.ShapeDtypeStruct((n, WIDTH), BF16),
                   jax.ShapeDtypeStruct((n, WIDTH), F32),
                   jax.ShapeDtypeStruct((n, WIDTH), F32),
                   jax.ShapeDtypeStruct((n, WIDTH), BF16),
                   jax.ShapeDtypeStruct((n, WIDTH), BF16)],
        compiler_params=_cparams("parallel"),
        name="inproj",
    )(x, shift, scale, w_in_bf16)


def _seg_ones(n):
    i = jnp.arange(n) // HEAD_DIM
    return (i[:, None] == i[None, :]).astype(BF16)


def _softplus(u):
    return jnp.maximum(u, 0.0) + jnp.log(1.0 + jnp.exp(-jnp.abs(u)))


def _rwkv_pre_kernel(has_prev, p_ref, prev_ref, mu_ref, w0_ref, wup_ref, a0_ref, aup_ref, gup_ref,
                     kk_ref, ka_ref, rk_ref, seg_ref,
                     okk_ref, odec_ref, ob_ref, okm_ref, ov_ref, owr_ref, obr_ref, okr_ref, og_ref, obon_ref,
                     carry_ref):
    hi = lax.Precision.HIGHEST
    pf = p_ref[...]
    if has_prev:
        prev = prev_ref[...]
    else:
        tb = pl.program_id(1)
        first = jnp.where(tb == 0, prev_ref[...], carry_ref[...])
        rows = lax.broadcasted_iota(jnp.int32, pf.shape, 0)
        prev = jnp.where(rows == 0, first, pltpu.roll(pf, 1, axis=0))
        carry_ref[...] = pf[pf.shape[0] - 1:, :]
    pm = pf + (prev - pf) * mu_ref[...]
    r = pm[:, :WIDTH]
    k = pm[:, WIDTH:2 * WIDTH]
    v = pm[:, 2 * WIDTH:3 * WIDTH]
    dwa = pm[:, 3 * WIDTH:3 * WIDTH + W_LORA + A_LORA]
    dg = pm[:, 3 * WIDTH + W_LORA + A_LORA:]
    seg = seg_ref[...]
    w = -_softplus(-(w0_ref[...] + jnp.dot(jnp.tanh(dwa), wup_ref[...], precision=hi,
                                           preferred_element_type=F32))) - 0.5
    dec = jnp.exp(-jnp.exp(w))
    a = jax.nn.sigmoid(a0_ref[...] + jnp.dot(dwa, aup_ref[...], precision=hi, preferred_element_type=F32))
    g = jnp.dot(jax.nn.sigmoid(dg), gup_ref[...], precision=hi, preferred_element_type=F32)
    kkr = k * kk_ref[...]
    kk = kkr * lax.rsqrt(jnp.maximum(_dot2(kkr * kkr, seg), 1e-24))
    km = k * (1.0 + (a - 1.0) * ka_ref[...])
    b = kk * a
    okk_ref[...] = kk
    odec_ref[...] = dec
    ob_ref[...] = b
    okm_ref[...] = km
    ov_ref[...] = v
    owr_ref[...] = dec * r
    obr_ref[...] = _dot2(b * r, seg)
    okr_ref[...] = _dot2(km * r, seg)
    og_ref[...] = g
    obon_ref[...] = _dot2(r * km * rk_ref[...], seg) * v


def _rwkv_pre(p, prev, has_prev, prm, tb):
    bsz, t, _ = p.shape
    blk = lambda c: pl.BlockSpec((None, tb, c), lambda i, j: (i, j, 0))
    full = lambda a: pl.BlockSpec(a.shape, lambda i, j: (0,) * a.ndim)
    prev_spec = blk(RWKV_PROJ) if has_prev else pl.BlockSpec((None, 1, RWKV_PROJ), lambda i, j: (i, 0, 0))
    params = [prm["mu"], prm["w0"], prm["wup"], prm["a0"], prm["aup"], prm["gup"],
              prm["k_k"], prm["k_a"], prm["r_k"], prm["seg512"]]
    return pl.pallas_call(
        functools.partial(_rwkv_pre_kernel, has_prev),
        grid=(bsz, t // tb),
        in_specs=[blk(RWKV_PROJ), prev_spec] + [full(a) for a in params],
        out_specs=[blk(WIDTH)] * 10,
        out_shape=[jax.ShapeDtypeStruct((bsz, t, WIDTH), F32)] * 10,
        scratch_shapes=[pltpu.VMEM((1, RWKV_PROJ), F32)],
        compiler_params=_cparams("parallel", "arbitrary"),
        name="rwkv_pre",
    )(p, prev, *params)


def _rwkv_scan_kernel(nb_count, tb, kk_ref, dec_ref, b_ref, km_ref, v_ref, wr_ref, br_ref, kr_ref,
                      g_ref, bon_ref, s0_ref, lg_ref, lb_ref, seg256_ref, seg512_ref,
                      out_ref, st_ref, s_scr, y_scr):
    step_blk = pl.program_id(1)

    @pl.when(step_blk == 0)
    def _():
        s_scr[...] = s0_ref[...]

    shape = (HEAD_DIM, WIDTH)
    ident = (lax.broadcasted_iota(jnp.int32, shape, 1) & (HEAD_DIM - 1)) == lax.broadcasted_iota(jnp.int32, shape, 0)
    seg = seg256_ref[...]
    half = WIDTH // 2

    def segsum(lhs):
        return jnp.concatenate([_dot(lhs[:, :half], seg), _dot(lhs[:, half:], seg)], axis=1)

    def step(t, carry):
        for nb in range(nb_count):
            row = lambda ref: ref[nb, pl.ds(t, 1), :]
            s = s_scr[nb]
            p_hi, p_lo = _split_bf16(s * row(kk_ref))
            dv = jnp.where(ident, row(v_ref), 0.0).astype(BF16)
            pr = (s * row(wr_ref)).astype(BF16)
            res = segsum(jnp.concatenate([p_hi, p_lo, dv, pr], axis=0))
            sa = res[:HEAD_DIM] + res[HEAD_DIM:2 * HEAD_DIM]
            vcol = res[2 * HEAD_DIM:3 * HEAD_DIM]
            ycol = res[3 * HEAD_DIM:] - sa * row(br_ref)
            s_scr[nb] = s * row(dec_ref) - sa * row(b_ref) + vcol * row(km_ref)
            y_scr[nb, pl.ds(t, 1), :] = (jnp.sum(jnp.where(ident, ycol, 0.0), axis=0, keepdims=True)
                                         + row(v_ref) * row(kr_ref))
        return carry

    lax.fori_loop(0, tb, step, 0)

    seg512 = seg512_ref[...]
    for nb in range(nb_count):
        y = y_scr[nb]
        mu = _dot2(y, seg512) * (1.0 / HEAD_DIM)
        yc = y - mu
        var = _dot2(yc * yc, seg512) * (1.0 / HEAD_DIM)
        yn = yc * lax.rsqrt(var + GN_EPS) * lg_ref[...] + lb_ref[...]
        out_ref[nb] = (yn + bon_ref[nb]) * g_ref[nb]

    @pl.when(step_blk == pl.num_programs(1) - 1)
    def _():
        st_ref[...] = s_scr[...]


def _rwkv_scan(pre, s0, prm, nb, tb):
    kk, dec, b, km, v, wr, br, kr, g, bon = pre
    bsz, t, _ = kk.shape
    blk = pl.BlockSpec((nb, tb, WIDTH), lambda i, j: (i, j, 0))
    sblk = pl.BlockSpec((nb, HEAD_DIM, WIDTH), lambda i, j: (i, 0, 0))
    full = lambda a: pl.BlockSpec(a.shape, lambda i, j: (0,) * a.ndim)
    params = [prm["lnx_g"], prm["lnx_b"], prm["seg256"], prm["seg512"]]
    return pl.pallas_call(
        functools.partial(_rwkv_scan_kernel, nb, tb),
        grid=(bsz // nb, t // tb),
        in_specs=[blk] * 10 + [sblk] + [full(a) for a in params],
        out_specs=[blk, sblk],
        out_shape=[jax.ShapeDtypeStruct((bsz, t, WIDTH), F32),
                   jax.ShapeDtypeStruct((bsz, HEAD_DIM, WIDTH), F32)],
        scratch_shapes=[pltpu.VMEM((nb, HEAD_DIM, WIDTH), F32), pltpu.VMEM((nb, tb, WIDTH), F32)],
        compiler_params=_cparams("parallel", "arbitrary"),
        name="rwkv_scan",
    )(kk, dec, b, km, v, wr, br, kr, g, bon, s0, *params)


def _rwkv_params(mu_shift, w0, w_up, a0, a_up, g_up, k_k, k_a, r_k, lnx_g, lnx_b):
    row = lambda a: a.reshape(1, -1).astype(F32)
    zeros = jnp.zeros((A_LORA, WIDTH), F32)
    return dict(mu=row(mu_shift), w0=row(w0), a0=row(a0), k_k=row(k_k), k_a=row(k_a), r_k=row(r_k),
                lnx_g=row(lnx_g), lnx_b=row(lnx_b), gup=g_up,
                wup=jnp.concatenate([w_up, zeros], axis=0), aup=jnp.concatenate([zeros, a_up], axis=0),
                seg256=_seg_ones(WIDTH // 2), seg512=_seg_ones(WIDTH))


def _state_to_rows(wkv):
    bsz = wkv.shape[0]
    return wkv.transpose(0, 2, 1, 3).reshape(bsz, HEAD_DIM, WIDTH)


def _rows_to_state(s):
    bsz = s.shape[0]
    return s.reshape(bsz, HEAD_DIM, N_HEADS, HEAD_DIM).transpose(0, 2, 1, 3)


def _rwkv(p, shift_prev, wkv0, prm, nb, tb):
    bsz, t, _ = p.shape
    if t == 1:
        pre = _rwkv_pre(p.reshape(1, bsz, RWKV_PROJ), shift_prev.reshape(1, bsz, RWKV_PROJ), True, prm, bsz)
        pre = [a.reshape(bsz, 1, WIDTH) for a in pre]
    else:
        pre = _rwkv_pre(p, shift_prev.reshape(bsz, 1, RWKV_PROJ), False, prm, tb)
    out, st = _rwkv_scan(pre, _state_to_rows(wkv0), prm, nb, min(tb, t))
    return out, _rows_to_state(st)


def _sb_tile(z, mask, carry, neg_ge, neg_ones):
    sp = _softplus(z)
    if mask is not None:
        sp = jnp.where(mask, sp, 0.0)
    spb = sp.astype(BF16)
    inc = _dot(spb, neg_ge)
    bk = z.shape[1]
    a = jnp.exp(z + inc + jnp.concatenate([carry] * (bk // carry.shape[1]), axis=1))
    if mask is not None:
        a = jnp.where(mask, a, 0.0)
    if neg_ones is None:
        return a.astype(BF16), carry + jnp.broadcast_to(inc[:, :1], carry.shape)
    return a.astype(BF16), carry + _dot(spb, neg_ones)


def _sb_prompt_kernel(bq, bk, bias_ref, q_ref, k_ref, v_ref, mge_ref, o_ref):
    hp = pl.program_id(1)
    qi = pl.program_id(2)
    lane = lax.broadcasted_iota(jnp.int32, (1, 2 * HEAD_DIM), 1)
    first = lane < HEAD_DIM
    q2 = q_ref[...] * jnp.asarray(SB_SCALE, BF16)
    zero = jnp.zeros_like(q2)
    q_heads = (jnp.where(first, q2, zero), jnp.where(first, zero, q2))
    biases = (bias_ref[2 * hp], bias_ref[2 * hp + 1])
    m_ge = mge_ref[...]
    per_q = bq // bk
    rows = lax.broadcasted_iota(jnp.int32, (bq, bk), 0)
    cols = lax.broadcasted_iota(jnp.int32, (bq, bk), 1)

    def tile(j, mask, state):
        acc, carries = state
        kblk = k_ref[pl.ds(pl.multiple_of(j * bk, bk), bk), :]
        vblk = v_ref[pl.ds(pl.multiple_of(j * bk, bk), bk), :]
        vzero = jnp.zeros_like(vblk)
        v_heads = (jnp.where(first, vblk, vzero), jnp.where(first, vzero, vblk))
        new_carries = []
        for e in range(2):
            z = _dot_nt(q_heads[e], kblk) + biases[e]
            a, c = _sb_tile(z, mask, carries[e], m_ge, None)
            acc = acc + _dot(a, v_heads[e])
            new_carries.append(c)
        return acc, tuple(new_carries)

    zc = jnp.zeros((bq, 2 * HEAD_DIM), F32)
    state = (zc, (zc, zc))
    for u in reversed(range(per_q)):
        state = tile(qi * per_q + u, cols + u * bk < rows, state)
    n_full = qi * per_q

    def full_tiles(i, s):
        for u in range(per_q):
            s = tile(n_full - 1 - i * per_q - u, None, s)
        return s

    state = lax.fori_loop(0, qi, full_tiles, state)
    o_ref[...] = state[0]


def _sb_prompt(q, k, v, sb_bias, bq, bk):
    bsz, t, _ = q.shape
    pair = 2 * HEAD_DIM
    idx = jnp.arange(bk)
    m_ge = -(idx[:, None] >= idx[None, :]).astype(BF16)
    return pl.pallas_call(
        functools.partial(_sb_prompt_kernel, bq, bk),
        grid=(bsz, WIDTH // pair, t // bq),
        in_specs=[pl.BlockSpec(memory_space=pltpu.SMEM),
                  pl.BlockSpec((None, bq, pair), lambda b, h, i: (b, i, h)),
                  pl.BlockSpec((None, t, pair), lambda b, h, i: (b, 0, h)),
                  pl.BlockSpec((None, t, pair), lambda b, h, i: (b, 0, h)),
                  pl.BlockSpec((bk, bk), lambda b, h, i: (0, 0))],
        out_specs=pl.BlockSpec((None, bq, pair), lambda b, h, i: (b, i, h)),
        out_shape=jax.ShapeDtypeStruct((bsz, t, WIDTH), F32),
        compiler_params=_cparams("parallel", "parallel", "arbitrary"),
        name="sb_prompt",
    )(sb_bias.astype(F32), q, k, v, m_ge)


def _sb_sample_kernel(pp, n_pages, pt_ref, q_ref, knew_ref, vnew_ref, bias_ref, mge_ref, ones_ref, *refs):
    k_refs, v_refs = refs[:pp], refs[pp:2 * pp]
    o_ref, carry_scr, acc_scr = refs[2 * pp:]
    g = pl.program_id(1)

    @pl.when(g == 0)
    def _():
        carry_scr[...] = jnp.zeros_like(carry_scr)
        acc_scr[...] = jnp.zeros_like(acc_scr)

    head_row = lax.broadcasted_iota(jnp.int32, (N_HEADS, PAGE_SIZE), 0)
    head_row_d = lax.broadcasted_iota(jnp.int32, (N_HEADS, HEAD_DIM), 0)
    q = q_ref[...]
    qb = (q * SB_SCALE).astype(BF16)
    bias = bias_ref[...]
    zs = []
    for u in range(pp):
        z = None
        for h in range(N_HEADS):
            zh = _dot(qb, k_refs[u][h].astype(BF16))
            z = zh if z is None else jnp.where(head_row == h, zh, z)
        zs.append(z + bias)
    z_all = jnp.concatenate(zs, axis=1)
    spb = _softplus(z_all).astype(BF16)
    incs, carries = [], []
    carry = carry_scr[...]
    for u in range(pp):
        page = spb[:, u * PAGE_SIZE:(u + 1) * PAGE_SIZE]
        incs.append(_dot(page, mge_ref[...]))
        carries.append(carry)
        carry = carry + _dot(page, ones_ref[...])
    carry_scr[...] = carry
    a_all = jnp.exp(z_all + jnp.concatenate(incs, axis=1) + jnp.concatenate(carries, axis=1)).astype(BF16)
    acc = acc_scr[...]
    for u in range(pp):
        a = a_all[:, u * PAGE_SIZE:(u + 1) * PAGE_SIZE]
        for h in range(N_HEADS):
            acc = acc + jnp.where(head_row_d == h, _dot_nt(a, v_refs[u][h].astype(BF16)), 0.0)
    acc_scr[...] = acc

    @pl.when(g == pl.num_programs(1) - 1)
    def _():
        past = n_pages * PAGE_SIZE
        z_new = jnp.sum(q * knew_ref[...], axis=1, keepdims=True) * SB_SCALE + bias[:, :1]
        a_new = jnp.where(past < past, jnp.exp(-_softplus(-z_new)), 0.0)
        o_ref[...] = acc + a_new * vnew_ref[...]


def _sb_sample(q, k_new, v_new, sb_bias, cache_k, cache_v, page_table, pp):
    bsz = q.shape[0]
    n_pages = page_table.shape[1]
    idx = jnp.arange(PAGE_SIZE)
    m_ge = -(idx[:, None] >= idx[None, :]).astype(BF16)
    ones_cols = -jnp.ones((PAGE_SIZE, PAGE_SIZE), BF16)
    bias = jnp.broadcast_to(sb_bias.astype(F32)[:, None], (N_HEADS, PAGE_SIZE))
    rows = lambda c: c.transpose(0, 2, 3, 1)
    row = pl.BlockSpec((None, N_HEADS, HEAD_DIM), lambda b, g, pt: (b, 0, 0))
    full = lambda a: pl.BlockSpec(a.shape, lambda b, g, pt: (0,) * a.ndim)

    def page_spec(u):
        return pl.BlockSpec((None, N_HEADS, HEAD_DIM, PAGE_SIZE),
                            lambda b, g, pt: (pt[b * n_pages + n_pages - 1 - (g * pp + u)], 0, 0, 0))

    grid_spec = pltpu.PrefetchScalarGridSpec(
        num_scalar_prefetch=1,
        grid=(bsz, n_pages // pp),
        in_specs=[row, row, row, full(bias), full(m_ge), full(ones_cols)]
        + [page_spec(u) for u in range(pp)] * 2,
        out_specs=row,
        scratch_shapes=[pltpu.VMEM((N_HEADS, PAGE_SIZE), F32), pltpu.VMEM((N_HEADS, HEAD_DIM), F32)],
    )
    return pl.pallas_call(
        functools.partial(_sb_sample_kernel, pp, n_pages),
        grid_spec=grid_spec,
        out_shape=jax.ShapeDtypeStruct((bsz, N_HEADS, HEAD_DIM), F32),
        compiler_params=_cparams("parallel", "arbitrary"),
        name="sb_sample",
    )(page_table.reshape(-1), q, k_new, v_new, bias, m_ge, ones_cols, *([rows(cache_k)] * pp), *([rows(cache_v)] * pp))


def _outproj_kernel(rw_ref, sb_ref, x_ref, g1_ref, sh2_ref, sc2_ref, wo_ref, lg_ref, lb_ref, wpq_ref,
                    x1_ref, h2_ref, qp_ref):
    mix = _dot(rw_ref[...].astype(BF16), wo_ref[:WIDTH, :]) + _dot(sb_ref[...].astype(BF16), wo_ref[WIDTH:, :])
    x1 = _ln_rows(DN_ALPHA * x_ref[...] + g1_ref[...] * mix) * lg_ref[...] + lb_ref[...]
    h2 = _ln_rows(x1) * (1.0 + sc2_ref[...]) + sh2_ref[...]
    x1_ref[...] = x1
    h2_ref[...] = h2
    qp_ref[...] = _dot(h2.astype(BF16), wpq_ref[...]).astype(BF16)


def _mod_spec(a, rows_per_mod, rb, d):
    if rows_per_mod == 1:
        return a, pl.BlockSpec((rb, d), lambda i: (i, 0))
    per = rows_per_mod // rb
    return a.reshape(-1, 1, d), pl.BlockSpec((None, 1, d), lambda i: (i // per, 0, 0))


def _outproj(rw, sb, x, g1, sh2, sc2, w_out_bf16, ln_g, ln_b, w_pq_bf16, rows_per_mod, rb):
    n, d = x.shape
    row = lambda c: pl.BlockSpec((rb, c), lambda i: (i, 0))
    full = lambda a: pl.BlockSpec(a.shape, lambda i: (0,) * a.ndim)
    mods, mod_specs = zip(*[_mod_spec(a, rows_per_mod, rb, d) for a in (g1, sh2, sc2)])
    lg, lb = ln_g.reshape(1, d), ln_b.reshape(1, d)
    return pl.pallas_call(
        _outproj_kernel,
        grid=(n // rb,),
        in_specs=[row(WIDTH), row(WIDTH), row(d), *mod_specs, full(w_out_bf16), full(lg), full(lb), full(w_pq_bf16)],
        out_specs=[row(d), row(d), row(d)],
        out_shape=[jax.ShapeDtypeStruct((n, d), F32), jax.ShapeDtypeStruct((n, d), F32),
                   jax.ShapeDtypeStruct((n, d), BF16)],
        compiler_params=_cparams("parallel"),
        name="outproj",
    )(rw, sb, x, *mods, w_out_bf16, lg, lb, w_pq_bf16)


def _take_top(x, ids, payload, count):
    vals, picked = [], []
    for _ in range(count):
        m = jnp.max(x, axis=0, keepdims=True)
        pos = jnp.min(jnp.where(x == m, ids, jnp.iinfo(jnp.int32).max), axis=0, keepdims=True)
        hit = ids == pos
        vals.append(m)
        picked.append(pos if payload is None else jnp.sum(jnp.where(hit, payload, 0), axis=0, keepdims=True))
        x = jnp.where(hit, -jnp.inf, x)
    return jnp.concatenate(vals, axis=0), jnp.concatenate(picked, axis=0)


def _pair_candidates(sv0, si0, sv1, si1):
    k = PEER_TOPK
    tokens = sv0.shape[1]
    wide = 4
    vals, ids, experts = [], [], []
    for a in range(wide):
        n = -(-(k // (a + 1)) // 8) * 8
        b_ids = lax.broadcasted_iota(jnp.int32, (n, tokens), 0)
        valid = b_ids < k // (a + 1)
        vals.append(jnp.where(valid, sv0[a:a + 1] + sv1[:n], -jnp.inf))
        ids.append(jnp.where(valid, a * k + b_ids, -1))
        experts.append(si0[a:a + 1] * N_KEYS + si1[:n])
    for b in range(k // (wide + 1)):
        last_a = k // (b + 1) - 1
        n = -(-(last_a + 1) // 8) * 8
        a_ids = lax.broadcasted_iota(jnp.int32, (n, tokens), 0)
        valid = (a_ids >= wide) & (a_ids <= last_a)
        vals.append(jnp.where(valid, sv0[:n] + sv1[b:b + 1], -jnp.inf))
        ids.append(jnp.where(valid, a_ids * k + b, -1))
        experts.append(si0[:n] * N_KEYS + si1[b:b + 1])
    return jnp.concatenate(vals, axis=0), jnp.concatenate(ids, axis=0), jnp.concatenate(experts, axis=0)


def _route_kernel(qp_ref, sk_ref, e_ref, g_ref):
    tokens = qp_ref.shape[0]
    scores = _dot_nt(sk_ref[...], qp_ref[...])
    key_ids = lax.broadcasted_iota(jnp.int32, (N_KEYS, tokens), 0)
    sv0, si0 = _take_top(scores[:N_KEYS], key_ids, None, PEER_TOPK)
    sv1, si1 = _take_top(scores[N_KEYS:], key_ids, None, PEER_TOPK)
    cand, cand_ids, cidx = _pair_candidates(sv0, si0, sv1, si1)
    top, eidx = _take_top(cand, cand_ids, cidx, PEER_TOPK)
    ex = jnp.exp(top - top[:1])
    e_ref[...] = eidx
    g_ref[...] = ex / jnp.sum(ex, axis=0, keepdims=True)


def _route(qp, sk_pairs, tb):
    n = qp.shape[0]
    blk = pl.BlockSpec((None, PEER_TOPK, tb), lambda i, h: (h, 0, i))
    return pl.pallas_call(
        _route_kernel,
        grid=(n // tb, PEER_HEADS),
        in_specs=[pl.BlockSpec((tb, 2 * HEAD_DIM), lambda i, h: (i, h)),
                  pl.BlockSpec((None, 2 * N_KEYS, 2 * HEAD_DIM), lambda i, h: (h, 0, 0))],
        out_specs=[blk, blk],
        out_shape=[jax.ShapeDtypeStruct((PEER_HEADS, PEER_TOPK, n), jnp.int32),
                   jax.ShapeDtypeStruct((PEER_HEADS, PEER_TOPK, n), F32)],
        compiler_params=_cparams("parallel", "parallel"),
        name="peer_route",
    )(qp, sk_pairs)


def _sub_key_pairs(sub_keys):
    z = jnp.zeros_like(sub_keys[:, 0])
    top = jnp.concatenate([sub_keys[:, 0], z], axis=-1)
    bot = jnp.concatenate([z, sub_keys[:, 1]], axis=-1)
    return jnp.concatenate([top, bot], axis=1).astype(BF16)


ROW_WORDS = D_MODEL // 2
ROW_PLANES = ROW_WORDS // 128
PLANE_STRIDE = PEER_SLOTS + 8


def _pack_table(w):
    bits = lax.bitcast_convert_type(w.astype(BF16), jnp.uint16).astype(jnp.uint32)
    packed = bits[:, :ROW_WORDS] | (bits[:, ROW_WORDS:] << 16)
    return lax.bitcast_convert_type(packed, jnp.int32).reshape(-1, 128)


def _gather_rows(idx_ref, base, tbl_ref, buf_ref):
    for m in range(PEER_SLOTS):
        start = pl.multiple_of(idx_ref[base + m], ROW_PLANES)
        buf_ref[pl.ds(m, ROW_PLANES, stride=PLANE_STRIDE), :] = tbl_ref[pl.ds(start, ROW_PLANES), :]


def _plane_halves(buf_ref, j):
    words = buf_ref[pl.ds(j * PLANE_STRIDE, PEER_SLOTS), :]
    lo = lax.bitcast_convert_type(words << 16, F32)
    hi = lax.bitcast_convert_type(words & jnp.int32(-65536), F32)
    return lo, hi


def _token_pairs(tb, idx_ref, tbl_ref, buf_a, buf_b, compute, init):
    _gather_rows(idx_ref, 0, tbl_ref, buf_a)

    def pair(i, carry):
        t = 2 * i
        _gather_rows(idx_ref, (t + 1) * PEER_SLOTS, tbl_ref, buf_b)
        carry = compute(t, buf_a, carry)
        _gather_rows(idx_ref, jnp.minimum(t + 2, tb - 1) * PEER_SLOTS, tbl_ref, buf_a)
        return compute(t + 1, buf_b, carry)

    return lax.fori_loop(0, tb // 2, pair, init)


def _peer_act_kernel(tb, idx_ref, h_ref, tbl_ref, act_ref, buf_a, buf_b):
    token_lane = lax.broadcasted_iota(jnp.int32, (PEER_SLOTS, tb), 1)

    def compute(t, buf, acts):
        hrow = h_ref[pl.ds(t, 1), :]
        acc = None
        for j in range(ROW_PLANES):
            lo, hi = _plane_halves(buf, j)
            term = (lo * hrow[:, j * 128:(j + 1) * 128]
                    + hi * hrow[:, ROW_WORDS + j * 128:ROW_WORDS + (j + 1) * 128])
            acc = term if acc is None else acc + term
        return jnp.where(token_lane == t, jnp.sum(acc, axis=1, keepdims=True), acts)

    act_ref[...] = _token_pairs(tb, idx_ref, tbl_ref, buf_a, buf_b, compute, jnp.zeros((PEER_SLOTS, tb), F32))


def _peer_out_kernel(tb, idx_ref, act_ref, gate_ref, tbl_ref, f_ref, buf_a, buf_b, coef_ref):
    act = act_ref[...]
    coef_ref[...] = gate_ref[...] * (0.5 * act * (1.0 + lax.erf(act * (2.0 ** -0.5))))
    token_lane = lax.broadcasted_iota(jnp.int32, (PEER_SLOTS, tb), 1)

    def compute(t, buf, carry):
        coef = jnp.sum(jnp.where(token_lane == t, coef_ref[...], 0.0), axis=1, keepdims=True)
        los, his = [], []
        for j in range(ROW_PLANES):
            lo, hi = _plane_halves(buf, j)
            los.append(jnp.sum(lo * coef, axis=0, keepdims=True))
            his.append(jnp.sum(hi * coef, axis=0, keepdims=True))
        f_ref[pl.ds(t, 1), :] = jnp.concatenate(los + his, axis=1)
        return carry

    _token_pairs(tb, idx_ref, tbl_ref, buf_a, buf_b, compute, 0)


def _peer_experts(h2, idx_flat, gate_t, tbl_u, tbl_v, tb):
    n, d = h2.shape
    smem = pl.BlockSpec((tb * PEER_SLOTS,), lambda i: (i,), memory_space=pltpu.SMEM)
    row = pl.BlockSpec((tb, d), lambda i: (i, 0))
    slot_major = pl.BlockSpec((PEER_SLOTS, tb), lambda i: (0, i))
    table = pl.BlockSpec(memory_space=pltpu.VMEM)
    buf = pltpu.VMEM((ROW_PLANES * PLANE_STRIDE, 128), jnp.int32)
    act_t = pl.pallas_call(
        functools.partial(_peer_act_kernel, tb),
        grid=(n // tb,),
        in_specs=[smem, row, table],
        out_specs=slot_major,
        out_shape=jax.ShapeDtypeStruct((PEER_SLOTS, n), F32),
        scratch_shapes=[buf, buf],
        compiler_params=_cparams("arbitrary"),
        name="peer_act",
    )(idx_flat, h2, tbl_u)
    return pl.pallas_call(
        functools.partial(_peer_out_kernel, tb),
        grid=(n // tb,),
        in_specs=[smem, slot_major, slot_major, table],
        out_specs=row,
        out_shape=jax.ShapeDtypeStruct((n, d), F32),
        scratch_shapes=[buf, buf, pltpu.VMEM((PEER_SLOTS, tb), F32)],
        compiler_params=_cparams("arbitrary"),
        name="peer_out",
    )(idx_flat, act_t, gate_t, tbl_v)


def _final_kernel(x1_ref, f_ref, g2_ref, lg_ref, lb_ref, y_ref):
    y_ref[...] = _ln_rows(DN_ALPHA * x1_ref[...] + g2_ref[...] * f_ref[...]) * lg_ref[...] + lb_ref[...]


def _final(x1, f, g2, ln_g, ln_b, rows_per_mod, rb):
    n, d = x1.shape
    row = pl.BlockSpec((rb, d), lambda i: (i, 0))
    g2, g2_spec = _mod_spec(g2, rows_per_mod, rb, d)
    vec = pl.BlockSpec((1, d), lambda i: (0, 0))
    return pl.pallas_call(
        _final_kernel,
        grid=(n // rb,),
        in_specs=[row, row, g2_spec, vec, vec],
        out_specs=row,
        out_shape=jax.ShapeDtypeStruct((n, d), F32),
        compiler_params=_cparams("parallel"),
        name="final_ln",
    )(x1, f, g2, ln_g.reshape(1, d), ln_b.reshape(1, d))


ROW_BLOCK = 256
SCAN_BATCH = 4
SCAN_BLOCK = 128
SB_BLOCK = 512
SB_KEY_BLOCK = 256
SB_PAGES = 8
ROUTE_BLOCK = 128
PEER_BLOCK = 128


def _mixer_half(x, mods, shift_prev, wkv0, attend, wts, rows_per_mod, rb):
    bsz, t, d = x.shape
    n = bsz * t
    sh1, sc1, g1, sh2, sc2, _ = mods
    xr = x.reshape(n, d)
    p, q, k, v, kb, vb = _inproj(xr, sh1, sc1, wts["w_in"], rows_per_mod, rb)
    p3 = p.reshape(bsz, t, RWKV_PROJ)
    rw, wkv_new = _rwkv(p3, shift_prev, wkv0, wts["rwkv"], SCAN_BATCH, SCAN_BLOCK)
    sb = attend(q, k, v, kb, vb)
    x1, h2, qp = _outproj(rw.reshape(n, WIDTH), sb.reshape(n, WIDTH), xr, g1, sh2, sc2, wts["w_out"],
                          wts["ln1_g"], wts["ln1_b"], wts["w_pq"], rows_per_mod, rb)
    heads = lambda a: a.reshape(bsz, t, N_HEADS, HEAD_DIM)
    return x1, h2, qp, heads(k), heads(v), wkv_new, p3[:, -1]


def _layer_pair(xp, xs, mod_p, mod_s, cache_k, cache_v, page_table, state_wkv, state_shift, wts):
    bp, tp, d = xp.shape
    bs, ts, _ = xs.shape
    assert ts == 1, "the paged attention handles one new token per sequence"
    n_p = bp * tp

    def attend_p(q, k, v, kb, vb):
        r = lambda a: a.reshape(bp, tp, WIDTH)
        return _sb_prompt(r(q), r(kb), r(vb), wts["sb_bias"], SB_BLOCK, SB_KEY_BLOCK)

    def attend_s(q, k, v, kb, vb):
        heads = lambda a: a.astype(F32).reshape(bs, N_HEADS, HEAD_DIM)
        return _sb_sample(heads(q), heads(k), heads(v), wts["sb_bias"], cache_k, cache_v, page_table, SB_PAGES)

    zero_shift = jnp.zeros((bp, RWKV_PROJ), xp.dtype)
    zero_wkv = jnp.zeros((bp, N_HEADS, HEAD_DIM, HEAD_DIM), state_wkv.dtype)
    x1p, h2p, qpp, kp, vp, wp, sp = _mixer_half(xp, mod_p, zero_shift, zero_wkv, attend_p, wts, tp, ROW_BLOCK)
    x1s, h2s, qps, ks, vs, ws, ss = _mixer_half(xs, mod_s, state_shift, state_wkv, attend_s, wts, 1, bs)

    h2 = jnp.concatenate([h2p, h2s], axis=0)
    qp = jnp.concatenate([qpp, qps], axis=0)
    n = h2.shape[0]
    eidx, gate = _route(qp, wts["sub_keys"], ROUTE_BLOCK)
    idx_flat = eidx.transpose(2, 0, 1).reshape(-1) * ROW_PLANES
    f = _peer_experts(h2, idx_flat, gate.reshape(PEER_SLOTS, n), wts["peer_u"], wts["peer_v"], PEER_BLOCK)
    yp = _final(x1p, f[:n_p], mod_p[5], wts["ln2_g"], wts["ln2_b"], tp, ROW_BLOCK).reshape(bp, tp, d)
    ys = _final(x1s, f[n_p:], mod_s[5], wts["ln2_g"], wts["ln2_b"], 1, bs).reshape(bs, ts, d)
    return yp, ys, kp, vp, ks, vs, wp, ws, sp, ss


def kernel(x_prompt, x_sample, c_prompt, c_sample, cache_k, cache_v, page_table, state_wkv, state_shift,
           w_cond, b_cond, w_in, mu_shift, w0, w_up, a0, a_up, g_up, k_k, k_a, r_k, lnx_g, lnx_b,
           sb_bias, w_out, ln1_g, ln1_b, w_pq, sub_keys, peer_u, peer_v, ln2_g, ln2_b):
    depth = w_in.shape[0]
    bp, bs = c_prompt.shape[0], c_sample.shape[0]
    pad = (-(bp + bs)) % 8
    c_all = jnp.concatenate([c_prompt, c_sample, jnp.zeros((pad, c_prompt.shape[1]), c_prompt.dtype)], axis=0)
    yp, ys = x_prompt, x_sample
    outs = [[] for _ in range(8)]
    for l in range(depth):
        wts = dict(
            w_in=w_in[l].astype(BF16), w_out=w_out[l].astype(BF16), w_pq=w_pq[l].astype(BF16),
            rwkv=_rwkv_params(mu_shift[l], w0[l], w_up[l], a0[l], a_up[l], g_up[l], k_k[l], k_a[l], r_k[l],
                              lnx_g[l], lnx_b[l]),
            sb_bias=sb_bias[l], ln1_g=ln1_g[l], ln1_b=ln1_b[l], ln2_g=ln2_g[l], ln2_b=ln2_b[l],
            sub_keys=_sub_key_pairs(sub_keys[l]), peer_u=_pack_table(peer_u[l]), peer_v=_pack_table(peer_v[l]))
        mod = _cond(c_all, w_cond[l], b_cond[l])
        mod_p = jnp.split(mod[:bp], N_MOD, axis=-1)
        mod_s = jnp.split(mod[bp:bp + bs], N_MOD, axis=-1)
        res = _layer_pair(yp, ys, mod_p, mod_s, cache_k[l], cache_v[l], page_table, state_wkv[l], state_shift[l], wts)
        yp, ys = res[0], res[1]
        for acc, val in zip(outs, res[2:]):
            acc.append(val)
    return (yp, ys) + tuple(jnp.stack(o) for o in outs)
```

```python
import functools

import jax
import jax.numpy as jnp
from jax import lax
from jax.experimental import pallas as pl
from jax.experimental.pallas import tpu as pltpu

F32 = jnp.float32
BF16 = jnp.bfloat16

D_MODEL = 1024
HEAD_DIM = 64
N_HEADS = 8
WIDTH = N_HEADS * HEAD_DIM
W_LORA, A_LORA, G_LORA = 64, 64, 128
RWKV_PROJ = 3 * WIDTH + W_LORA + A_LORA + G_LORA
IN_COLS = RWKV_PROJ + 3 * WIDTH
GN_EPS = HEAD_DIM * 1e-5
LN_EPS = 1e-5
SB_SCALE = HEAD_DIM ** -0.5
PAGE_SIZE = 128
PEER_HEADS = 8
N_KEYS = 128
PEER_TOPK = 16
PEER_SLOTS = PEER_HEADS * PEER_TOPK
N_MOD = 6
DEPTH = 1
DN_ALPHA = (2 * DEPTH) ** 0.25

VMEM_LIMIT = 56 * 1024 * 1024


def _cparams(*sem):
    return pltpu.CompilerParams(dimension_semantics=sem, vmem_limit_bytes=VMEM_LIMIT)


def _ln_rows(x):
    mu = jnp.mean(x, axis=-1, keepdims=True)
    xc = x - mu
    var = jnp.mean(xc * xc, axis=-1, keepdims=True)
    return xc * lax.rsqrt(var + LN_EPS)


def _split_bf16(x):
    hi = x.astype(BF16)
    lo = (x - hi.astype(F32)).astype(BF16)
    return hi, lo


def _dot(a, b):
    return jnp.dot(a, b, preferred_element_type=F32)


def _dot_nt(a, b):
    return lax.dot_general(a, b, (((1,), (1,)), ((), ())), preferred_element_type=F32)


def _dot2(x, w_bf16):
    hi, lo = _split_bf16(x)
    return _dot(hi, w_bf16) + _dot(lo, w_bf16)


def _cond_kernel(c_ref, w_ref, b_ref, o_ref):
    c = c_ref[...]
    s = c * jax.nn.sigmoid(c)
    o_ref[...] = jnp.dot(s, w_ref[...], preferred_element_type=F32,
                         precision=lax.Precision.HIGHEST) + b_ref[...]


def _cond(c, w_cond, b_cond):
    n, d = c.shape
    cols = w_cond.shape[1]
    bn = 1024
    return pl.pallas_call(
        _cond_kernel,
        grid=(cols // bn,),
        in_specs=[pl.BlockSpec((n, d), lambda j: (0, 0)),
                  pl.BlockSpec((d, bn), lambda j: (0, j)),
                  pl.BlockSpec((1, bn), lambda j: (0, j))],
        out_specs=pl.BlockSpec((n, bn), lambda j: (0, j)),
        out_shape=jax.ShapeDtypeStruct((n, cols), F32),
        compiler_params=_cparams("parallel"),
        name="cond",
    )(c, w_cond, b_cond.reshape(1, cols))


def _inproj_kernel(x_ref, sh_ref, sc_ref, w_ref, p_ref, q_ref, k_ref, v_ref, kb_ref, vb_ref):
    h = _ln_rows(x_ref[...]) * (1.0 + sc_ref[...]) + sh_ref[...]
    hb = h.astype(BF16)
    p_ref[...] = _dot(hb, w_ref[:, :RWKV_PROJ])
    q_ref[...] = _dot(hb, w_ref[:, RWKV_PROJ:RWKV_PROJ + WIDTH]).astype(BF16)
    k = _dot(hb, w_ref[:, RWKV_PROJ + WIDTH:RWKV_PROJ + 2 * WIDTH])
    v = _dot(hb, w_ref[:, RWKV_PROJ + 2 * WIDTH:])
    k_ref[...] = k
    v_ref[...] = v
    kb_ref[...] = k.astype(BF16)
    vb_ref[...] = v.astype(BF16)


def _inproj(x, shift, scale, w_in_bf16, rows_per_mod, block_rows):
    n, d = x.shape
    rb = block_rows
    if rows_per_mod == 1:
        mod_spec = pl.BlockSpec((rb, d), lambda i: (i, 0))
    else:
        per = rows_per_mod // rb
        shift = shift.reshape(-1, 1, d)
        scale = scale.reshape(-1, 1, d)
        mod_spec = pl.BlockSpec((None, 1, d), lambda i: (i // per, 0, 0))
    row = lambda c: pl.BlockSpec((rb, c), lambda i: (i, 0))
    return pl.pallas_call(
        _inproj_kernel,
        grid=(n // rb,),
        in_specs=[row(d), mod_spec, mod_spec,
                  pl.BlockSpec((d, IN_COLS), lambda i: (0, 0))],
        out_specs=[row(RWKV_PROJ), row(WIDTH), row(WIDTH), row(WIDTH), row(WIDTH), row(WIDTH)],
        out_shape=[jax.ShapeDtypeStruct((n, RWKV_PROJ), F32),
                   jax.ShapeDtypeStruct((n, WIDTH), BF16),
                   jax.ShapeDtypeStruct((n, WIDTH), F32),
                   jax.ShapeDtypeStruct((n, WIDTH), F32),
                   jax.ShapeDtypeStruct((n, WIDTH), BF16),
                   jax.ShapeDtypeStruct((n, WIDTH), BF16)],
        compiler_params=_cparams("parallel"),
        name="inproj",
    )(x, shift, scale, w_in_bf16)


def _seg_ones(n):
    i = jnp.arange(n) // HEAD_DIM
    return (i[:, None] == i[None, :]).astype(BF16)


def _softplus(u):
    return jnp.maximum(u, 0.0) + jnp.log(1.0 + jnp.exp(-jnp.abs(u)))


def _rwkv_pre_kernel(has_prev, p_ref, prev_ref, mu_ref, w0_ref, wup_ref, a0_ref, aup_ref, gup_ref,
                     kk_ref, ka_ref, rk_ref, seg_ref,
                     okk_ref, odec_ref, ob_ref, okm_ref, ov_ref, owr_ref, obr_ref, okr_ref, og_ref, obon_ref,
                     carry_ref):
    hi = lax.Precision.HIGHEST
    pf = p_ref[...]
    if has_prev:
        prev = prev_ref[...]
    else:
        tb = pl.program_id(1)
        first = jnp.where(tb == 0, prev_ref[...], carry_ref[...])
        rows = lax.broadcasted_iota(jnp.int32, pf.shape, 0)
        prev = jnp.where(rows == 0, first, pltpu.roll(pf, 1, axis=0))
        carry_ref[...] = pf[pf.shape[0] - 1:, :]
    pm = pf + (prev - pf) * mu_ref[...]
    r = pm[:, :WIDTH]
    k = pm[:, WIDTH:2 * WIDTH]
    v = pm[:, 2 * WIDTH:3 * WIDTH]
    dwa = pm[:, 3 * WIDTH:3 * WIDTH + W_LORA + A_LORA]
    dg = pm[:, 3 * WIDTH + W_LORA + A_LORA:]
    seg = seg_ref[...]
    w = -_softplus(-(w0_ref[...] + jnp.dot(jnp.tanh(dwa), wup_ref[...], precision=hi,
                                           preferred_element_type=F32))) - 0.5
    dec = jnp.exp(-jnp.exp(w))
    a = jax.nn.sigmoid(a0_ref[...] + jnp.dot(dwa, aup_ref[...], precision=hi, preferred_element_type=F32))
    g = jnp.dot(jax.nn.sigmoid(dg), gup_ref[...], precision=hi, preferred_element_type=F32)
    kkr = k * kk_ref[...]
    kk = kkr * lax.rsqrt(jnp.maximum(_dot2(kkr * kkr, seg), 1e-24))
    km = k * (1.0 + (a - 1.0) * ka_ref[...])
    b = kk * a
    okk_ref[...] = kk
    odec_ref[...] = dec
    ob_ref[...] = b
    okm_ref[...] = km
    ov_ref[...] = v
    owr_ref[...] = dec * r
    obr_ref[...] = _dot2(b * r, seg)
    okr_ref[...] = _dot2(km * r, seg)
    og_ref[...] = g
    obon_ref[...] = _dot2(r * km * rk_ref[...], seg) * v


def _rwkv_pre(p, prev, has_prev, prm, tb):
    bsz, t, _ = p.shape
    blk = lambda c: pl.BlockSpec((None, tb, c), lambda i, j: (i, j, 0))
    full = lambda a: pl.BlockSpec(a.shape, lambda i, j: (0,) * a.ndim)
    prev_spec = blk(RWKV_PROJ) if has_prev else pl.BlockSpec((None, 1, RWKV_PROJ), lambda i, j: (i, 0, 0))
    params = [prm["mu"], prm["w0"], prm["wup"], prm["a0"], prm["aup"], prm["gup"],
              prm["k_k"], prm["k_a"], prm["r_k"], prm["seg512"]]
    return pl.pallas_call(
        functools.partial(_rwkv_pre_kernel, has_prev),
        grid=(bsz, t // tb),
        in_specs=[blk(RWKV_PROJ), prev_spec] + [full(a) for a in params],
        out_specs=[blk(WIDTH)] * 10,
        out_shape=[jax.ShapeDtypeStruct((bsz, t, WIDTH), F32)] * 10,
        scratch_shapes=[pltpu.VMEM((1, RWKV_PROJ), F32)],
        compiler_params=_cparams("parallel", "arbitrary"),
        name="rwkv_pre",
    )(p, prev, *params)


def _rwkv_scan_kernel(nb_count, tb, kk_ref, dec_ref, b_ref, km_ref, v_ref, wr_ref, br_ref, kr_ref,
                      g_ref, bon_ref, s0_ref, lg_ref, lb_ref, seg256_ref, seg512_ref,
                      out_ref, st_ref, s_scr, y_scr):
    step_blk = pl.program_id(1)

    @pl.when(step_blk == 0)
    def _():
        s_scr[...] = s0_ref[...]

    shape = (HEAD_DIM, WIDTH)
    ident = (lax.broadcasted_iota(jnp.int32, shape, 1) & (HEAD_DIM - 1)) == lax.broadcasted_iota(jnp.int32, shape, 0)
    seg = seg256_ref[...]
    half = WIDTH // 2

    def segsum(lhs):
        return jnp.concatenate([_dot(lhs[:, :half], seg), _dot(lhs[:, half:], seg)], axis=1)

    def step(t, carry):
        for nb in range(nb_count):
            row = lambda ref: ref[nb, pl.ds(t, 1), :]
            s = s_scr[nb]
            p_hi, p_lo = _split_bf16(s * row(kk_ref))
            dv = jnp.where(ident, row(v_ref), 0.0).astype(BF16)
            pr = (s * row(wr_ref)).astype(BF16)
            res = segsum(jnp.concatenate([p_hi, p_lo, dv, pr], axis=0))
            sa = res[:HEAD_DIM] + res[HEAD_DIM:2 * HEAD_DIM]
            vcol = res[2 * HEAD_DIM:3 * HEAD_DIM]
            ycol = res[3 * HEAD_DIM:] - sa * row(br_ref)
            s_scr[nb] = s * row(dec_ref) - sa * row(b_ref) + vcol * row(km_ref)
            y_scr[nb, pl.ds(t, 1), :] = (jnp.sum(jnp.where(ident, ycol, 0.0), axis=0, keepdims=True)
                                         + row(v_ref) * row(kr_ref))
        return carry

    lax.fori_loop(0, tb, step, 0)

    seg512 = seg512_ref[...]
    for nb in range(nb_count):
        y = y_scr[nb]
        mu = _dot2(y, seg512) * (1.0 / HEAD_DIM)
        yc = y - mu
        var = _dot2(yc * yc, seg512) * (1.0 / HEAD_DIM)
        yn = yc * lax.rsqrt(var + GN_EPS) * lg_ref[...] + lb_ref[...]
        out_ref[nb] = (yn + bon_ref[nb]) * g_ref[nb]

    @pl.when(step_blk == pl.num_programs(1) - 1)
    def _():
        st_ref[...] = s_scr[...]


def _rwkv_scan(pre, s0, prm, nb, tb):
    kk, dec, b, km, v, wr, br, kr, g, bon = pre
    bsz, t, _ = kk.shape
    blk = pl.BlockSpec((nb, tb, WIDTH), lambda i, j: (i, j, 0))
    sblk = pl.BlockSpec((nb, HEAD_DIM, WIDTH), lambda i, j: (i, 0, 0))
    full = lambda a: pl.BlockSpec(a.shape, lambda i, j: (0,) * a.ndim)
    params = [prm["lnx_g"], prm["lnx_b"], prm["seg256"], prm["seg512"]]
    return pl.pallas_call(
        functools.partial(_rwkv_scan_kernel, nb, tb),
        grid=(bsz // nb, t // tb),
        in_specs=[blk] * 10 + [sblk] + [full(a) for a in params],
        out_specs=[blk, sblk],
        out_shape=[jax.ShapeDtypeStruct((bsz, t, WIDTH), F32),
                   jax.ShapeDtypeStruct((bsz, HEAD_DIM, WIDTH), F32)],
        scratch_shapes=[pltpu.VMEM((nb, HEAD_DIM, WIDTH), F32), pltpu.VMEM((nb, tb, WIDTH), F32)],
        compiler_params=_cparams("parallel", "arbitrary"),
        name="rwkv_scan",
    )(kk, dec, b, km, v, wr, br, kr, g, bon, s0, *params)


def _rwkv_params(mu_shift, w0, w_up, a0, a_up, g_up, k_k, k_a, r_k, lnx_g, lnx_b):
    row = lambda a: a.reshape(1, -1).astype(F32)
    zeros = jnp.zeros((A_LORA, WIDTH), F32)
    return dict(mu=row(mu_shift), w0=row(w0), a0=row(a0), k_k=row(k_k), k_a=row(k_a), r_k=row(r_k),
                lnx_g=row(lnx_g), lnx_b=row(lnx_b), gup=g_up,
                wup=jnp.concatenate([w_up, zeros], axis=0), aup=jnp.concatenate([zeros, a_up], axis=0),
                seg256=_seg_ones(WIDTH // 2), seg512=_seg_ones(WIDTH))


def _state_to_rows(wkv):
    bsz = wkv.shape[0]
    return wkv.transpose(0, 2, 1, 3).reshape(bsz, HEAD_DIM, WIDTH)


def _rows_to_state(s):
    bsz = s.shape[0]
    return s.reshape(bsz, HEAD_DIM, N_HEADS, HEAD_DIM).transpose(0, 2, 1, 3)


def _rwkv(p, shift_prev, wkv0, prm, nb, tb):
    bsz, t, _ = p.shape
    if t == 1:
        pre = _rwkv_pre(p.reshape(1, bsz, RWKV_PROJ), shift_prev.reshape(1, bsz, RWKV_PROJ), True, prm, bsz)
        pre = [a.reshape(bsz, 1, WIDTH) for a in pre]
    else:
        pre = _rwkv_pre(p, shift_prev.reshape(bsz, 1, RWKV_PROJ), False, prm, tb)
    out, st = _rwkv_scan(pre, _state_to_rows(wkv0), prm, nb, min(tb, t))
    return out, _rows_to_state(st)


def _sb_tile(z, mask, carry, neg_ge, neg_ones):
    sp = _softplus(z)
    if mask is not None:
        sp = jnp.where(mask, sp, 0.0)
    spb = sp.astype(BF16)
    inc = _dot(spb, neg_ge)
    bk = z.shape[1]
    a = jnp.exp(z + inc + jnp.concatenate([carry] * (bk // carry.shape[1]), axis=1))
    if mask is not None:
        a = jnp.where(mask, a, 0.0)
    if neg_ones is None:
        return a.astype(BF16), carry + jnp.broadcast_to(inc[:, :1], carry.shape)
    return a.astype(BF16), carry + _dot(spb, neg_ones)


def _sb_prompt_kernel(bq, bk, bias_ref, q_ref, k_ref, v_ref, mge_ref, o_ref):
    hp = pl.program_id(1)
    qi = pl.program_id(2)
    lane = lax.broadcasted_iota(jnp.int32, (1, 2 * HEAD_DIM), 1)
    first = lane < HEAD_DIM
    q2 = q_ref[...] * jnp.asarray(SB_SCALE, BF16)
    zero = jnp.zeros_like(q2)
    q_heads = (jnp.where(first, q2, zero), jnp.where(first, zero, q2))
    biases = (bias_ref[2 * hp], bias_ref[2 * hp + 1])
    m_ge = mge_ref[...]
    per_q = bq // bk
    rows = lax.broadcasted_iota(jnp.int32, (bq, bk), 0)
    cols = lax.broadcasted_iota(jnp.int32, (bq, bk), 1)

    def tile(j, mask, state):
        acc, carries = state
        kblk = k_ref[pl.ds(pl.multiple_of(j * bk, bk), bk), :]
        vblk = v_ref[pl.ds(pl.multiple_of(j * bk, bk), bk), :]
        vzero = jnp.zeros_like(vblk)
        v_heads = (jnp.where(first, vblk, vzero), jnp.where(first, vzero, vblk))
        new_carries = []
        for e in range(2):
            z = _dot_nt(q_heads[e], kblk) + biases[e]
            a, c = _sb_tile(z, mask, carries[e], m_ge, None)
            acc = acc + _dot(a, v_heads[e])
            new_carries.append(c)
        return acc, tuple(new_carries)

    zc = jnp.zeros((bq, 2 * HEAD_DIM), F32)
    state = (zc, (zc, zc))
    for u in reversed(range(per_q)):
        state = tile(qi * per_q + u, cols + u * bk < rows, state)
    n_full = qi * per_q

    def full_tiles(i, s):
        for u in range(per_q):
            s = tile(n_full - 1 - i * per_q - u, None, s)
        return s

    state = lax.fori_loop(0, qi, full_tiles, state)
    o_ref[...] = state[0]


def _sb_prompt(q, k, v, sb_bias, bq, bk):
    bsz, t, _ = q.shape
    pair = 2 * HEAD_DIM
    idx = jnp.arange(bk)
    m_ge = -(idx[:, None] >= idx[None, :]).astype(BF16)
    return pl.pallas_call(
        functools.partial(_sb_prompt_kernel, bq, bk),
        grid=(bsz, WIDTH // pair, t // bq),
        in_specs=[pl.BlockSpec(memory_space=pltpu.SMEM),
                  pl.BlockSpec((None, bq, pair), lambda b, h, i: (b, i, h)),
                  pl.BlockSpec((None, t, pair), lambda b, h, i: (b, 0, h)),
                  pl.BlockSpec((None, t, pair), lambda b, h, i: (b, 0, h)),
                  pl.BlockSpec((bk, bk), lambda b, h, i: (0, 0))],
        out_specs=pl.BlockSpec((None, bq, pair), lambda b, h, i: (b, i, h)),
        out_shape=jax.ShapeDtypeStruct((bsz, t, WIDTH), F32),
        compiler_params=_cparams("parallel", "parallel", "arbitrary"),
        name="sb_prompt",
    )(sb_bias.astype(F32), q, k, v, m_ge)


def _sb_sample_kernel(pp, n_pages, pt_ref, q_ref, knew_ref, vnew_ref, bias_ref, mge_ref, ones_ref, *refs):
    k_refs, v_refs = refs[:pp], refs[pp:2 * pp]
    o_ref, carry_scr, acc_scr = refs[2 * pp:]
    g = pl.program_id(1)

    @pl.when(g == 0)
    def _():
        carry_scr[...] = jnp.zeros_like(carry_scr)
        acc_scr[...] = jnp.zeros_like(acc_scr)

    head_row = lax.broadcasted_iota(jnp.int32, (N_HEADS, PAGE_SIZE), 0)
    head_row_d = lax.broadcasted_iota(jnp.int32, (N_HEADS, HEAD_DIM), 0)
    q = q_ref[...]
    qb = (q * SB_SCALE).astype(BF16)
    bias = bias_ref[...]
    zs = []
    for u in range(pp):
        z = None
        for h in range(N_HEADS):
            zh = _dot(qb, k_refs[u][h].astype(BF16))
            z = zh if z is None else jnp.where(head_row == h, zh, z)
        zs.append(z + bias)
    z_all = jnp.concatenate(zs, axis=1)
    spb = _softplus(z_all).astype(BF16)
    incs, carries = [], []
    carry = carry_scr[...]
    for u in range(pp):
        page = spb[:, u * PAGE_SIZE:(u + 1) * PAGE_SIZE]
        incs.append(_dot(page, mge_ref[...]))
        carries.append(carry)
        carry = carry + _dot(page, ones_ref[...])
    carry_scr[...] = carry
    a_all = jnp.exp(z_all + jnp.concatenate(incs, axis=1) + jnp.concatenate(carries, axis=1)).astype(BF16)
    acc = acc_scr[...]
    for u in range(pp):
        a = a_all[:, u * PAGE_SIZE:(u + 1) * PAGE_SIZE]
        for h in range(N_HEADS):
            acc = acc + jnp.where(head_row_d == h, _dot_nt(a, v_refs[u][h].astype(BF16)), 0.0)
    acc_scr[...] = acc

    @pl.when(g == pl.num_programs(1) - 1)
    def _():
        past = n_pages * PAGE_SIZE
        z_new = jnp.sum(q * knew_ref[...], axis=1, keepdims=True) * SB_SCALE + bias[:, :1]
        a_new = jnp.where(past < past, jnp.exp(-_softplus(-z_new)), 0.0)
        o_ref[...] = acc + a_new * vnew_ref[...]


def _sb_sample(q, k_new, v_new, sb_bias, cache_k, cache_v, page_table, pp):
    bsz = q.shape[0]
    n_pages = page_table.shape[1]
    idx = jnp.arange(PAGE_SIZE)
    m_ge = -(idx[:, None] >= idx[None, :]).astype(BF16)
    ones_cols = -jnp.ones((PAGE_SIZE, PAGE_SIZE), BF16)
    bias = jnp.broadcast_to(sb_bias.astype(F32)[:, None], (N_HEADS, PAGE_SIZE))
    rows = lambda c: c.transpose(0, 2, 3, 1)
    row = pl.BlockSpec((None, N_HEADS, HEAD_DIM), lambda b, g, pt: (b, 0, 0))
    full = lambda a: pl.BlockSpec(a.shape, lambda b, g, pt: (0,) * a.ndim)

    def page_spec(u):
        return pl.BlockSpec((None, N_HEADS, HEAD_DIM, PAGE_SIZE),
                            lambda b, g, pt: (pt[b * n_pages + n_pages - 1 - (g * pp + u)], 0, 0, 0))

    grid_spec = pltpu.PrefetchScalarGridSpec(
        num_scalar_prefetch=1,
        grid=(bsz, n_pages // pp),
        in_specs=[row, row, row, full(bias), full(m_ge), full(ones_cols)]
        + [page_spec(u) for u in range(pp)] * 2,
        out_specs=row,
        scratch_shapes=[pltpu.VMEM((N_HEADS, PAGE_SIZE), F32), pltpu.VMEM((N_HEADS, HEAD_DIM), F32)],
    )
    return pl.pallas_call(
        functools.partial(_sb_sample_kernel, pp, n_pages),
        grid_spec=grid_spec,
        out_shape=jax.ShapeDtypeStruct((bsz, N_HEADS, HEAD_DIM), F32),
        compiler_params=_cparams("parallel", "arbitrary"),
        name="sb_sample",
    )(page_table.reshape(-1), q, k_new, v_new, bias, m_ge, ones_cols, *([rows(cache_k)] * pp), *([rows(cache_v)] * pp))


def _outproj_kernel(rw_ref, sb_ref, x_ref, g1_ref, sh2_ref, sc2_ref, wo_ref, lg_ref, lb_ref, wpq_ref,
                    x1_ref, h2_ref, qp_ref):
    mix = _dot(rw_ref[...].astype(BF16), wo_ref[:WIDTH, :]) + _dot(sb_ref[...].astype(BF16), wo_ref[WIDTH:, :])
    x1 = _ln_rows(DN_ALPHA * x_ref[...] + g1_ref[...] * mix) * lg_ref[...] + lb_ref[...]
    h2 = _ln_rows(x1) * (1.0 + sc2_ref[...]) + sh2_ref[...]
    x1_ref[...] = x1
    h2_ref[...] = h2
    qp_ref[...] = _dot(h2.astype(BF16), wpq_ref[...]).astype(BF16)


def _mod_spec(a, rows_per_mod, rb, d):
    if rows_per_mod == 1:
        return a, pl.BlockSpec((rb, d), lambda i: (i, 0))
    per = rows_per_mod // rb
    return a.reshape(-1, 1, d), pl.BlockSpec((None, 1, d), lambda i: (i // per, 0, 0))


def _outproj(rw, sb, x, g1, sh2, sc2, w_out_bf16, ln_g, ln_b, w_pq_bf16, rows_per_mod, rb):
    n, d = x.shape
    row = lambda c: pl.BlockSpec((rb, c), lambda i: (i, 0))
    full = lambda a: pl.BlockSpec(a.shape, lambda i: (0,) * a.ndim)
    mods, mod_specs = zip(*[_mod_spec(a, rows_per_mod, rb, d) for a in (g1, sh2, sc2)])
    lg, lb = ln_g.reshape(1, d), ln_b.reshape(1, d)
    return pl.pallas_call(
        _outproj_kernel,
        grid=(n // rb,),
        in_specs=[row(WIDTH), row(WIDTH), row(d), *mod_specs, full(w_out_bf16), full(lg), full(lb), full(w_pq_bf16)],
        out_specs=[row(d), row(d), row(d)],
        out_shape=[jax.ShapeDtypeStruct((n, d), F32), jax.ShapeDtypeStruct((n, d), F32),
                   jax.ShapeDtypeStruct((n, d), BF16)],
        compiler_params=_cparams("parallel"),
        name="outproj",
    )(rw, sb, x, *mods, w_out_bf16, lg, lb, w_pq_bf16)


def _take_top(x, ids, payload, count):
    vals, picked = [], []
    for _ in range(count):
        m = jnp.max(x, axis=0, keepdims=True)
        pos = jnp.min(jnp.where(x == m, ids, jnp.iinfo(jnp.int32).max), axis=0, keepdims=True)
        hit = ids == pos
        vals.append(m)
        picked.append(pos if payload is None else jnp.sum(jnp.where(hit, payload, 0), axis=0, keepdims=True))
        x = jnp.where(hit, -jnp.inf, x)
    return jnp.concatenate(vals, axis=0), jnp.concatenate(picked, axis=0)


def _pair_candidates(sv0, si0, sv1, si1):
    k = PEER_TOPK
    tokens = sv0.shape[1]
    wide = 4
    vals, ids, experts = [], [], []
    for a in range(wide):
        n = -(-(k // (a + 1)) // 8) * 8
        b_ids = lax.broadcasted_iota(jnp.int32, (n, tokens), 0)
        valid = b_ids < k // (a + 1)
        vals.append(jnp.where(valid, sv0[a:a + 1] + sv1[:n], -jnp.inf))
        ids.append(jnp.where(valid, a * k + b_ids, -1))
        experts.append(si0[a:a + 1] * N_KEYS + si1[:n])
    for b in range(k // (wide + 1)):
        last_a = k // (b + 1) - 1
        n = -(-(last_a + 1) // 8) * 8
        a_ids = lax.broadcasted_iota(jnp.int32, (n, tokens), 0)
        valid = (a_ids >= wide) & (a_ids <= last_a)
        vals.append(jnp.where(valid, sv0[:n] + sv1[b:b + 1], -jnp.inf))
        ids.append(jnp.where(valid, a_ids * k + b, -1))
        experts.append(si0[:n] * N_KEYS + si1[b:b + 1])
    return jnp.concatenate(vals, axis=0), jnp.concatenate(ids, axis=0), jnp.concatenate(experts, axis=0)


def _route_kernel(qp_ref, sk_ref, e_ref, g_ref):
    tokens = qp_ref.shape[0]
    scores = _dot_nt(sk_ref[...], qp_ref[...])
    key_ids = lax.broadcasted_iota(jnp.int32, (N_KEYS, tokens), 0)
    sv0, si0 = _take_top(scores[:N_KEYS], key_ids, None, PEER_TOPK)
    sv1, si1 = _take_top(scores[N_KEYS:], key_ids, None, PEER_TOPK)
    cand, cand_ids, cidx = _pair_candidates(sv0, si0, sv1, si1)
    top, eidx = _take_top(cand, cand_ids, cidx, PEER_TOPK)
    ex = jnp.exp(top - top[:1])
    e_ref[...] = eidx
    g_ref[...] = ex / jnp.sum(ex, axis=0, keepdims=True)


def _route(qp, sk_pairs, tb):
    n = qp.shape[0]
    blk = pl.BlockSpec((None, PEER_TOPK, tb), lambda i, h: (h, 0, i))
    return pl.pallas_call(
        _route_kernel,
        grid=(n // tb, PEER_HEADS),
        in_specs=[pl.BlockSpec((tb, 2 * HEAD_DIM), lambda i, h: (i, h)),
                  pl.BlockSpec((None, 2 * N_KEYS, 2 * HEAD_DIM), lambda i, h: (h, 0, 0))],
        out_specs=[blk, blk],
        out_shape=[jax.ShapeDtypeStruct((PEER_HEADS, PEER_TOPK, n), jnp.int32),
                   jax.ShapeDtypeStruct((PEER_HEADS, PEER_TOPK, n), F32)],
        compiler_params=_cparams("parallel", "parallel"),
        name="peer_route",
    )(qp, sk_pairs)


def _sub_key_pairs(sub_keys):
    z = jnp.zeros_like(sub_keys[:, 0])
    top = jnp.concatenate([sub_keys[:, 0], z], axis=-1)
    bot = jnp.concatenate([z, sub_keys[:, 1]], axis=-1)
    return jnp.concatenate([top, bot], axis=1).astype(BF16)


ROW_WORDS = D_MODEL // 2
ROW_PLANES = ROW_WORDS // 128
PLANE_STRIDE = PEER_SLOTS + 8


def _pack_table(w):
    bits = lax.bitcast_convert_type(w.astype(BF16), jnp.uint16).astype(jnp.uint32)
    packed = bits[:, :ROW_WORDS] | (bits[:, ROW_WORDS:] << 16)
    return lax.bitcast_convert_type(packed, jnp.int32).reshape(-1, 128)


def _gather_rows(idx_ref, base, tbl_ref, buf_ref):
    token_idx = idx_ref.at[pl.ds(base, PEER_SLOTS)]
    for m in range(PEER_SLOTS):
        start = pl.multiple_of(token_idx[m], ROW_PLANES)
        buf_ref[pl.ds(m, ROW_PLANES, stride=PLANE_STRIDE), :] = tbl_ref[pl.ds(start, ROW_PLANES), :]


def _plane_halves(buf_ref, j):
    words = buf_ref[pl.ds(j * PLANE_STRIDE, PEER_SLOTS), :]
    lo = lax.bitcast_convert_type(words << 16, F32)
    hi = lax.bitcast_convert_type(words & jnp.int32(-65536), F32)
    return lo, hi


PEER_GROUP = 2


def _token_groups(tb, idx_ref, tbl_ref, buf_ref, compute, init):
    _gather_rows(idx_ref, 0, tbl_ref, buf_ref.at[0])

    def group(i, carry):
        for pos in range(PEER_GROUP):
            t = PEER_GROUP * i + pos
            nxt = t + 1 if pos + 1 < PEER_GROUP else jnp.minimum(t + 1, tb - 1)
            _gather_rows(idx_ref, nxt * PEER_SLOTS, tbl_ref, buf_ref.at[(pos + 1) % PEER_GROUP])
            carry = compute(t, buf_ref.at[pos], pos, carry)
        return carry

    return lax.fori_loop(0, tb // PEER_GROUP, group, init)


def _peer_act_kernel(tb, idx_ref, h_ref, tbl_ref, act_ref, buf_ref, acc_ref):
    token_lane = lax.broadcasted_iota(jnp.int32, (PEER_SLOTS, tb), 1)
    acc_ref[...] = jnp.zeros_like(acc_ref)

    def finish(t, pos, acts):
        return jnp.where(token_lane == t, jnp.sum(acc_ref[pos], axis=1, keepdims=True), acts)

    def compute(t, buf, pos, acts):
        acts = finish(t - PEER_GROUP, pos, acts)
        hrow = h_ref[pl.ds(t, 1), :]
        acc = None
        for j in range(ROW_PLANES):
            lo, hi = _plane_halves(buf, j)
            term = (lo * hrow[:, j * 128:(j + 1) * 128]
                    + hi * hrow[:, ROW_WORDS + j * 128:ROW_WORDS + (j + 1) * 128])
            acc = term if acc is None else acc + term
        acc_ref[pos] = acc
        return acts

    acts = _token_groups(tb, idx_ref, tbl_ref, buf_ref, compute, jnp.zeros((PEER_SLOTS, tb), F32))
    for pos in range(PEER_GROUP):
        acts = finish(tb - PEER_GROUP + pos, pos, acts)
    act_ref[...] = acts


def _peer_out_kernel(tb, idx_ref, act_ref, gate_ref, tbl_ref, f_ref, buf_ref, coef_ref, col_ref):
    act = act_ref[...]
    coef_ref[...] = gate_ref[...] * (0.5 * act * (1.0 + lax.erf(act * (2.0 ** -0.5))))
    token_lane = lax.broadcasted_iota(jnp.int32, (PEER_SLOTS, tb), 1)

    def stage_coef(t, pos):
        col = jnp.sum(jnp.where(token_lane == t, coef_ref[...], 0.0), axis=1, keepdims=True)
        col_ref[pos] = jnp.broadcast_to(col, (PEER_SLOTS, 128))

    for pos in range(PEER_GROUP):
        stage_coef(pos, pos)

    def compute(t, buf, pos, carry):
        coef = col_ref[pos]
        los, his = [], []
        for j in range(ROW_PLANES):
            lo, hi = _plane_halves(buf, j)
            los.append(jnp.sum(lo * coef, axis=0, keepdims=True))
            his.append(jnp.sum(hi * coef, axis=0, keepdims=True))
        f_ref[pl.ds(t, 1), :] = jnp.concatenate(los + his, axis=1)
        stage_coef(t + PEER_GROUP, pos)
        return carry

    _token_groups(tb, idx_ref, tbl_ref, buf_ref, compute, 0)


def _peer_experts(h2, idx_flat, gate_t, tbl_u, tbl_v, tb):
    n, d = h2.shape
    smem = pl.BlockSpec((tb * PEER_SLOTS,), lambda i: (i,), memory_space=pltpu.SMEM)
    row = pl.BlockSpec((tb, d), lambda i: (i, 0))
    slot_major = pl.BlockSpec((PEER_SLOTS, tb), lambda i: (0, i))
    table = pl.BlockSpec(memory_space=pltpu.VMEM)
    buf = pltpu.VMEM((PEER_GROUP, ROW_PLANES * PLANE_STRIDE, 128), jnp.int32)
    staged = pltpu.VMEM((PEER_GROUP, PEER_SLOTS, 128), F32)
    act_t = pl.pallas_call(
        functools.partial(_peer_act_kernel, tb),
        grid=(n // tb,),
        in_specs=[smem, row, table],
        out_specs=slot_major,
        out_shape=jax.ShapeDtypeStruct((PEER_SLOTS, n), F32),
        scratch_shapes=[buf, staged],
        compiler_params=_cparams("arbitrary"),
        name="peer_act",
    )(idx_flat, h2, tbl_u)
    return pl.pallas_call(
        functools.partial(_peer_out_kernel, tb),
        grid=(n // tb,),
        in_specs=[smem, slot_major, slot_major, table],
        out_specs=row,
        out_shape=jax.ShapeDtypeStruct((n, d), F32),
        scratch_shapes=[buf, pltpu.VMEM((PEER_SLOTS, tb), F32), staged],
        compiler_params=_cparams("arbitrary"),
        name="peer_out",
    )(idx_flat, act_t, gate_t, tbl_v)


def _final_kernel(x1_ref, f_ref, g2_ref, lg_ref, lb_ref, y_ref):
    y_ref[...] = _ln_rows(DN_ALPHA * x1_ref[...] + g2_ref[...] * f_ref[...]) * lg_ref[...] + lb_ref[...]


def _final(x1, f, f_first_row, g2, ln_g, ln_b, rows_per_mod, rb):
    n, d = x1.shape
    row = pl.BlockSpec((rb, d), lambda i: (i, 0))
    first_blk = f_first_row // rb
    assert first_blk * rb == f_first_row
    f_spec = pl.BlockSpec((rb, d), lambda i: (i + first_blk, 0))
    g2, g2_spec = _mod_spec(g2, rows_per_mod, rb, d)
    vec = pl.BlockSpec((1, d), lambda i: (0, 0))
    return pl.pallas_call(
        _final_kernel,
        grid=(n // rb,),
        in_specs=[row, f_spec, g2_spec, vec, vec],
        out_specs=row,
        out_shape=jax.ShapeDtypeStruct((n, d), F32),
        compiler_params=_cparams("parallel"),
        name="final_ln",
    )(x1, f, g2, ln_g.reshape(1, d), ln_b.reshape(1, d))


ROW_BLOCK = 256
SCAN_BATCH = 4
SCAN_BLOCK = 128
SB_BLOCK = 512
SB_KEY_BLOCK = 256
SB_PAGES = 8
ROUTE_BLOCK = 128
PEER_BLOCK = 128


def _mixer_half(x, mods, shift_prev, wkv0, attend, wts, rows_per_mod, rb):
    bsz, t, d = x.shape
    n = bsz * t
    sh1, sc1, g1, sh2, sc2, _ = mods
    xr = x.reshape(n, d)
    p, q, k, v, kb, vb = _inproj(xr, sh1, sc1, wts["w_in"], rows_per_mod, rb)
    p3 = p.reshape(bsz, t, RWKV_PROJ)
    rw, wkv_new = _rwkv(p3, shift_prev, wkv0, wts["rwkv"], SCAN_BATCH, SCAN_BLOCK)
    sb = attend(q, k, v, kb, vb)
    x1, h2, qp = _outproj(rw.reshape(n, WIDTH), sb.reshape(n, WIDTH), xr, g1, sh2, sc2, wts["w_out"],
                          wts["ln1_g"], wts["ln1_b"], wts["w_pq"], rows_per_mod, rb)
    heads = lambda a: a.reshape(bsz, t, N_HEADS, HEAD_DIM)
    return x1, h2, qp, heads(k), heads(v), wkv_new, p3[:, -1]


def _layer_pair(xp, xs, mod_p, mod_s, cache_k, cache_v, page_table, state_wkv, state_shift, wts):
    bp, tp, d = xp.shape
    bs, ts, _ = xs.shape
    assert ts == 1, "the paged attention handles one new token per sequence"
    n_p = bp * tp

    def attend_p(q, k, v, kb, vb):
        r = lambda a: a.reshape(bp, tp, WIDTH)
        return _sb_prompt(r(q), r(kb), r(vb), wts["sb_bias"], SB_BLOCK, SB_KEY_BLOCK)

    def attend_s(q, k, v, kb, vb):
        heads = lambda a: a.astype(F32).reshape(bs, N_HEADS, HEAD_DIM)
        return _sb_sample(heads(q), heads(k), heads(v), wts["sb_bias"], cache_k, cache_v, page_table, SB_PAGES)

    zero_shift = jnp.zeros((bp, RWKV_PROJ), xp.dtype)
    zero_wkv = jnp.zeros((bp, N_HEADS, HEAD_DIM, HEAD_DIM), state_wkv.dtype)
    x1p, h2p, qpp, kp, vp, wp, sp = _mixer_half(xp, mod_p, zero_shift, zero_wkv, attend_p, wts, tp, ROW_BLOCK)
    x1s, h2s, qps, ks, vs, ws, ss = _mixer_half(xs, mod_s, state_shift, state_wkv, attend_s, wts, 1, bs)

    h2 = jnp.concatenate([h2p, h2s], axis=0)
    qp = jnp.concatenate([qpp, qps], axis=0)
    n = h2.shape[0]
    eidx, gate = _route(qp, wts["sub_keys"], ROUTE_BLOCK)
    idx_flat = eidx.transpose(2, 0, 1).reshape(-1) * ROW_PLANES
    f = _peer_experts(h2, idx_flat, gate.reshape(PEER_SLOTS, n), wts["peer_u"], wts["peer_v"], PEER_BLOCK)
    yp = _final(x1p, f, 0, mod_p[5], wts["ln2_g"], wts["ln2_b"], tp, ROW_BLOCK).reshape(bp, tp, d)
    ys = _final(x1s, f, n_p, mod_s[5], wts["ln2_g"], wts["ln2_b"], 1, bs).reshape(bs, ts, d)
    return yp, ys, kp, vp, ks, vs, wp, ws, sp, ss


def kernel(x_prompt, x_sample, c_prompt, c_sample, cache_k, cache_v, page_table, state_wkv, state_shift,
           w_cond, b_cond, w_in, mu_shift, w0, w_up, a0, a_up, g_up, k_k, k_a, r_k, lnx_g, lnx_b,
           sb_bias, w_out, ln1_g, ln1_b, w_pq, sub_keys, peer_u, peer_v, ln2_g, ln2_b):
    depth = w_in.shape[0]
    bp, bs = c_prompt.shape[0], c_sample.shape[0]
    pad = (-(bp + bs)) % 8
    c_all = jnp.concatenate([c_prompt, c_sample, jnp.zeros((pad, c_prompt.shape[1]), c_prompt.dtype)], axis=0)
    yp, ys = x_prompt, x_sample
    outs = [[] for _ in range(8)]
    for l in range(depth):
        wts = dict(
            w_in=w_in[l].astype(BF16), w_out=w_out[l].astype(BF16), w_pq=w_pq[l].astype(BF16),
            rwkv=_rwkv_params(mu_shift[l], w0[l], w_up[l], a0[l], a_up[l], g_up[l], k_k[l], k_a[l], r_k[l],
                              lnx_g[l], lnx_b[l]),
            sb_bias=sb_bias[l], ln1_g=ln1_g[l], ln1_b=ln1_b[l], ln2_g=ln2_g[l], ln2_b=ln2_b[l],
            sub_keys=_sub_key_pairs(sub_keys[l]), peer_u=_pack_table(peer_u[l]), peer_v=_pack_table(peer_v[l]))
        mod = _cond(c_all, w_cond[l], b_cond[l])
        mod_p = jnp.split(mod[:bp], N_MOD, axis=-1)
        mod_s = jnp.split(mod[bp:bp + bs], N_MOD, axis=-1)
        res = _layer_pair(yp, ys, mod_p, mod_s, cache_k[l], cache_v[l], page_table, state_wkv[l], state_shift[l], wts)
        yp, ys = res[0], res[1]
        for acc, val in zip(outs, res[2:]):
            acc.append(val)
    return (yp, ys) + tuple(jnp.stack(o) for o in outs)
```

```python
import functools

import jax
import jax.numpy as jnp
from jax import lax
from jax.experimental import pallas as pl
from jax.experimental.pallas import tpu as pltpu

F32 = jnp.float32
BF16 = jnp.bfloat16

D_MODEL = 1024
HEAD_DIM = 64
N_HEADS = 8
WIDTH = N_HEADS * HEAD_DIM
W_LORA, A_LORA, G_LORA = 64, 64, 128
RWKV_PROJ = 3 * WIDTH + W_LORA + A_LORA + G_LORA
IN_COLS = RWKV_PROJ + 3 * WIDTH
GN_EPS = HEAD_DIM * 1e-5
LN_EPS = 1e-5
SB_SCALE = HEAD_DIM ** -0.5
PAGE_SIZE = 128
PEER_HEADS = 8
N_KEYS = 128
PEER_TOPK = 16
PEER_SLOTS = PEER_HEADS * PEER_TOPK
N_MOD = 6
DEPTH = 1
DN_ALPHA = (2 * DEPTH) ** 0.25

VMEM_LIMIT = 56 * 1024 * 1024


def _cparams(*sem):
    return pltpu.CompilerParams(dimension_semantics=sem, vmem_limit_bytes=VMEM_LIMIT)


def _ln_rows(x):
    mu = jnp.mean(x, axis=-1, keepdims=True)
    xc = x - mu
    var = jnp.mean(xc * xc, axis=-1, keepdims=True)
    return xc * lax.rsqrt(var + LN_EPS)


def _split_bf16(x):
    hi = x.astype(BF16)
    lo = (x - hi.astype(F32)).astype(BF16)
    return hi, lo


def _dot(a, b):
    return jnp.dot(a, b, preferred_element_type=F32)


def _dot_nt(a, b):
    return lax.dot_general(a, b, (((1,), (1,)), ((), ())), preferred_element_type=F32)


def _dot2(x, w_bf16):
    hi, lo = _split_bf16(x)
    return _dot(hi, w_bf16) + _dot(lo, w_bf16)


def _cond_kernel(c_ref, w_ref, b_ref, o_ref):
    c = c_ref[...]
    s = c * jax.nn.sigmoid(c)
    o_ref[...] = jnp.dot(s, w_ref[...], preferred_element_type=F32,
                         precision=lax.Precision.HIGHEST) + b_ref[...]


def _cond(c, w_cond, b_cond):
    n, d = c.shape
    cols = w_cond.shape[1]
    bn = 1024
    return pl.pallas_call(
        _cond_kernel,
        grid=(cols // bn,),
        in_specs=[pl.BlockSpec((n, d), lambda j: (0, 0)),
                  pl.BlockSpec((d, bn), lambda j: (0, j)),
                  pl.BlockSpec((1, bn), lambda j: (0, j))],
        out_specs=pl.BlockSpec((n, bn), lambda j: (0, j)),
        out_shape=jax.ShapeDtypeStruct((n, cols), F32),
        compiler_params=_cparams("parallel"),
        name="cond",
    )(c, w_cond, b_cond.reshape(1, cols))


def _inproj_kernel(x_ref, sh_ref, sc_ref, w_ref, p_ref, q_ref, k_ref, v_ref, kb_ref, vb_ref):
    h = _ln_rows(x_ref[...]) * (1.0 + sc_ref[...]) + sh_ref[...]
    hb = h.astype(BF16)
    p_ref[...] = _dot(hb, w_ref[:, :RWKV_PROJ])
    q_ref[...] = _dot(hb, w_ref[:, RWKV_PROJ:RWKV_PROJ + WIDTH]).astype(BF16)
    k = _dot(hb, w_ref[:, RWKV_PROJ + WIDTH:RWKV_PROJ + 2 * WIDTH])
    v = _dot(hb, w_ref[:, RWKV_PROJ + 2 * WIDTH:])
    k_ref[...] = k
    v_ref[...] = v
    kb_ref[...] = k.astype(BF16)
    vb_ref[...] = v.astype(BF16)


def _inproj(x, shift, scale, w_in_bf16, rows_per_mod, block_rows):
    n, d = x.shape
    rb = block_rows
    if rows_per_mod == 1:
        mod_spec = pl.BlockSpec((rb, d), lambda i: (i, 0))
    else:
        per = rows_per_mod // rb
        shift = shift.reshape(-1, 1, d)
        scale = scale.reshape(-1, 1, d)
        mod_spec = pl.BlockSpec((None, 1, d), lambda i: (i // per, 0, 0))
    row = lambda c: pl.BlockSpec((rb, c), lambda i: (i, 0))
    return pl.pallas_call(
        _inproj_kernel,
        grid=(n // rb,),
        in_specs=[row(d), mod_spec, mod_spec,
                  pl.BlockSpec((d, IN_COLS), lambda i: (0, 0))],
        out_specs=[row(RWKV_PROJ), row(WIDTH), row(WIDTH), row(WIDTH), row(WIDTH), row(WIDTH)],
        out_shape=[jax.ShapeDtypeStruct((n, RWKV_PROJ), F32),
                   jax.ShapeDtypeStruct((n, WIDTH), BF16),
                   jax.ShapeDtypeStruct((n, WIDTH), F32),
                   jax.ShapeDtypeStruct((n, WIDTH), F32),
                   jax.ShapeDtypeStruct((n, WIDTH), BF16),
                   jax.ShapeDtypeStruct((n, WIDTH), BF16)],
        compiler_params=_cparams("parallel"),
        name="inproj",
    )(x, shift, scale, w_in_bf16)


def _seg_ones(n):
    i = jnp.arange(n) // HEAD_DIM
    return (i[:, None] == i[None, :]).astype(BF16)


def _softplus(u):
    return jnp.maximum(u, 0.0) + jnp.log(1.0 + jnp.exp(-jnp.abs(u)))


def _rwkv_pre_kernel(has_prev, chunked, p_ref, prev_ref, mu_ref, w0_ref, wup_ref, a0_ref, aup_ref, gup_ref,
                     kk_ref, ka_ref, rk_ref, seg_ref, *refs):
    out_refs, carry_ref = refs[:-1], refs[-1]
    hi = lax.Precision.HIGHEST
    pf = p_ref[...]
    if has_prev:
        prev = prev_ref[...]
    else:
        tb = pl.program_id(1)
        first = jnp.where(tb == 0, prev_ref[...], carry_ref[...])
        rows = lax.broadcasted_iota(jnp.int32, pf.shape, 0)
        prev = jnp.where(rows == 0, first, pltpu.roll(pf, 1, axis=0))
        carry_ref[...] = pf[pf.shape[0] - 1:, :]
    pm = pf + (prev - pf) * mu_ref[...]
    r = pm[:, :WIDTH]
    k = pm[:, WIDTH:2 * WIDTH]
    v = pm[:, 2 * WIDTH:3 * WIDTH]
    dwa = pm[:, 3 * WIDTH:3 * WIDTH + W_LORA + A_LORA]
    dg = pm[:, 3 * WIDTH + W_LORA + A_LORA:]
    seg = seg_ref[...]
    w = -_softplus(-(w0_ref[...] + jnp.dot(jnp.tanh(dwa), wup_ref[...], precision=hi,
                                           preferred_element_type=F32))) - 0.5
    log_dec = -jnp.exp(w)
    a = jax.nn.sigmoid(a0_ref[...] + jnp.dot(dwa, aup_ref[...], precision=hi, preferred_element_type=F32))
    g = jnp.dot(jax.nn.sigmoid(dg), gup_ref[...], precision=hi, preferred_element_type=F32)
    kkr = k * kk_ref[...]
    kk = kkr * lax.rsqrt(jnp.maximum(_dot2(kkr * kkr, seg), 1e-24))
    km = k * (1.0 + (a - 1.0) * ka_ref[...])
    b = kk * a
    bonus = _dot2(r * km * rk_ref[...], seg) * v
    if chunked:
        outs = (kk, log_dec, b, km, v, r, g, bonus)
    else:
        dec = jnp.exp(log_dec)
        outs = (kk, dec, b, km, v, dec * r, _dot2(b * r, seg), _dot2(km * r, seg), g, bonus)
    for ref, val in zip(out_refs, outs):
        ref[...] = val


def _rwkv_pre(p, prev, has_prev, chunked, prm, tb):
    bsz, t, _ = p.shape
    n_out = 8 if chunked else 10
    blk = lambda c: pl.BlockSpec((None, tb, c), lambda i, j: (i, j, 0))
    full = lambda a: pl.BlockSpec(a.shape, lambda i, j: (0,) * a.ndim)
    prev_spec = blk(RWKV_PROJ) if has_prev else pl.BlockSpec((None, 1, RWKV_PROJ), lambda i, j: (i, 0, 0))
    params = [prm["mu"], prm["w0"], prm["wup"], prm["a0"], prm["aup"], prm["gup"],
              prm["k_k"], prm["k_a"], prm["r_k"], prm["seg512"]]
    return pl.pallas_call(
        functools.partial(_rwkv_pre_kernel, has_prev, chunked),
        grid=(bsz, t // tb),
        in_specs=[blk(RWKV_PROJ), prev_spec] + [full(a) for a in params],
        out_specs=[blk(WIDTH)] * n_out,
        out_shape=[jax.ShapeDtypeStruct((bsz, t, WIDTH), F32)] * n_out,
        scratch_shapes=[pltpu.VMEM((1, RWKV_PROJ), F32)],
        compiler_params=_cparams("parallel", "arbitrary"),
        name="rwkv_pre",
    )(p, prev, *params)


def _rwkv_scan_kernel(nb_count, tb, kk_ref, dec_ref, b_ref, km_ref, v_ref, wr_ref, br_ref, kr_ref,
                      g_ref, bon_ref, s0_ref, lg_ref, lb_ref, seg256_ref, seg512_ref,
                      out_ref, st_ref, s_scr, y_scr):
    step_blk = pl.program_id(1)

    @pl.when(step_blk == 0)
    def _():
        s_scr[...] = s0_ref[...]

    shape = (HEAD_DIM, WIDTH)
    ident = (lax.broadcasted_iota(jnp.int32, shape, 1) & (HEAD_DIM - 1)) == lax.broadcasted_iota(jnp.int32, shape, 0)
    seg = seg256_ref[...]
    half = WIDTH // 2

    def segsum(lhs):
        return jnp.concatenate([_dot(lhs[:, :half], seg), _dot(lhs[:, half:], seg)], axis=1)

    def step(t, carry):
        for nb in range(nb_count):
            row = lambda ref: ref[nb, pl.ds(t, 1), :]
            s = s_scr[nb]
            p_hi, p_lo = _split_bf16(s * row(kk_ref))
            dv = jnp.where(ident, row(v_ref), 0.0).astype(BF16)
            pr = (s * row(wr_ref)).astype(BF16)
            res = segsum(jnp.concatenate([p_hi, p_lo, dv, pr], axis=0))
            sa = res[:HEAD_DIM] + res[HEAD_DIM:2 * HEAD_DIM]
            vcol = res[2 * HEAD_DIM:3 * HEAD_DIM]
            ycol = res[3 * HEAD_DIM:] - sa * row(br_ref)
            s_scr[nb] = s * row(dec_ref) - sa * row(b_ref) + vcol * row(km_ref)
            y_scr[nb, pl.ds(t, 1), :] = (jnp.sum(jnp.where(ident, ycol, 0.0), axis=0, keepdims=True)
                                         + row(v_ref) * row(kr_ref))
        return carry

    lax.fori_loop(0, tb, step, 0)

    seg512 = seg512_ref[...]
    for nb in range(nb_count):
        y = y_scr[nb]
        mu = _dot2(y, seg512) * (1.0 / HEAD_DIM)
        yc = y - mu
        var = _dot2(yc * yc, seg512) * (1.0 / HEAD_DIM)
        yn = yc * lax.rsqrt(var + GN_EPS) * lg_ref[...] + lb_ref[...]
        out_ref[nb] = (yn + bon_ref[nb]) * g_ref[nb]

    @pl.when(step_blk == pl.num_programs(1) - 1)
    def _():
        st_ref[...] = s_scr[...]


def _rwkv_scan(pre, s0, prm, nb, tb):
    kk, dec, b, km, v, wr, br, kr, g, bon = pre
    bsz, t, _ = kk.shape
    blk = pl.BlockSpec((nb, tb, WIDTH), lambda i, j: (i, j, 0))
    sblk = pl.BlockSpec((nb, HEAD_DIM, WIDTH), lambda i, j: (i, 0, 0))
    full = lambda a: pl.BlockSpec(a.shape, lambda i, j: (0,) * a.ndim)
    params = [prm["lnx_g"], prm["lnx_b"], prm["seg256"], prm["seg512"]]
    return pl.pallas_call(
        functools.partial(_rwkv_scan_kernel, nb, tb),
        grid=(bsz // nb, t // tb),
        in_specs=[blk] * 10 + [sblk] + [full(a) for a in params],
        out_specs=[blk, sblk],
        out_shape=[jax.ShapeDtypeStruct((bsz, t, WIDTH), F32),
                   jax.ShapeDtypeStruct((bsz, HEAD_DIM, WIDTH), F32)],
        scratch_shapes=[pltpu.VMEM((nb, HEAD_DIM, WIDTH), F32), pltpu.VMEM((nb, tb, WIDTH), F32)],
        compiler_params=_cparams("parallel", "arbitrary"),
        name="rwkv_scan",
    )(kk, dec, b, km, v, wr, br, kr, g, bon, s0, *params)


def _rwkv_params(mu_shift, w0, w_up, a0, a_up, g_up, k_k, k_a, r_k, lnx_g, lnx_b):
    row = lambda a: a.reshape(1, -1).astype(F32)
    zeros = jnp.zeros((A_LORA, WIDTH), F32)
    return dict(mu=row(mu_shift), w0=row(w0), a0=row(a0), k_k=row(k_k), k_a=row(k_a), r_k=row(r_k),
                lnx_g=row(lnx_g), lnx_b=row(lnx_b), gup=g_up,
                wup=jnp.concatenate([w_up, zeros], axis=0), aup=jnp.concatenate([zeros, a_up], axis=0),
                seg256=_seg_ones(WIDTH // 2), seg512=_seg_ones(WIDTH))


def _state_to_rows(wkv):
    bsz = wkv.shape[0]
    return wkv.transpose(0, 2, 1, 3).reshape(bsz, HEAD_DIM, WIDTH)


def _rows_to_state(s):
    bsz = s.shape[0]
    return s.reshape(bsz, HEAD_DIM, N_HEADS, HEAD_DIM).transpose(0, 2, 1, 3)


CHUNK = 64
PAIR = 2 * HEAD_DIM
N_PAIRS = WIDTH // PAIR
_NN = ((1,), (0,))
_NT = ((1,), (1,))
_TN = ((0,), (0,))


def _dot3(a, b, dims):
    ah, al = _split_bf16(a)
    bh, bl = _split_bf16(b)
    dg = lambda x, y: lax.dot_general(x, y, (dims, ((), ())), preferred_element_type=F32)
    return dg(ah, bh) + dg(ah, bl) + dg(al, bh)


def _by_head(x, first):
    zero = jnp.zeros_like(x)
    return jnp.concatenate([jnp.where(first, x, zero), jnp.where(first, zero, x)], axis=0)


def _fold_heads(x):
    c = x.shape[0] // 2
    return x[:c] + x[c:]


def _rwkv_chunk_prep_kernel(kk_ref, ld_ref, b_ref, km_ref, v_ref, r_ref, tri_ref,
                            w1_ref, w2_ref, qr_ref, arb_ref, y0_ref, bd_ref, sadd_ref, gc_ref):
    c = CHUNK
    ld = ld_ref[...]
    ld_hi, ld_lo = _split_bf16(ld)
    tri = tri_ref[...]
    cum = _dot(tri, ld_hi) + _dot(tri, ld_lo)
    last = cum[c - 1:, :]
    g_end = jnp.exp(last - cum)
    qa = kk_ref[...] * jnp.exp(cum - ld)
    qr = r_ref[...] * jnp.exp(cum)
    g_inv = jnp.exp(-cum)
    kb = b_ref[...] * g_inv
    kt = km_ref[...] * g_inv
    bd = b_ref[...] * g_end
    kd = km_ref[...] * g_end
    v = v_ref[...]
    qr_ref[...] = qr
    bd_ref[...] = bd
    gc_ref[...] = jnp.exp(last)

    first = lax.broadcasted_iota(jnp.int32, (1, PAIR), 1) < HEAD_DIM
    rows = lax.broadcasted_iota(jnp.int32, (PAIR, PAIR), 0)
    cols = lax.broadcasted_iota(jnp.int32, (PAIR, PAIR), 1)
    same = (rows // c) == (cols // c)
    strict = same & (cols < rows)
    incl = same & (cols <= rows)
    eye = jnp.where(rows == cols, 1.0, 0.0)
    pairs = range(N_PAIRS)
    sls = [slice(p * PAIR, (p + 1) * PAIR) for p in pairs]
    by_head = lambda x: [_by_head(x[:, sl], first) for sl in sls]
    qa2, qr2, kb2, kt2, v2 = by_head(qa), by_head(qr), by_head(kb), by_head(kt), by_head(v)
    n = [-jnp.where(strict, _dot3(qa2[p], kb2[p], _NT), 0.0) for p in pairs]
    aak = [jnp.where(strict, _dot3(qa2[p], kt2[p], _NT), 0.0) for p in pairs]
    for p in pairs:
        arb_ref[:, sls[p]] = _fold_heads(jnp.where(incl, _dot3(qr2[p], kb2[p], _NT), 0.0))
        ark = jnp.where(incl, _dot3(qr2[p], kt2[p], _NT), 0.0)
        y0_ref[:, sls[p]] = _fold_heads(_dot3(ark, v2[p], _NN))
        sadd_ref[p] = jnp.where(same, _dot3(v[:, sls[p]], kd[:, sls[p]], _TN), 0.0)
    t_inv = [eye + n[p] for p in pairs]
    power = n
    for _ in range(c.bit_length() - 2):
        power = [_dot(power[p].astype(BF16), power[p].astype(BF16)) for p in pairs]
        t_inv = [t_inv[p] + _dot(t_inv[p].astype(BF16), power[p].astype(BF16)) for p in pairs]
    rhs = [_dot3(aak[p], v2[p], _NN) for p in pairs]
    for p in pairs:
        w1_ref[:, sls[p]] = _fold_heads(_dot3(t_inv[p], qa2[p], _NN))
        w2_ref[:, sls[p]] = _fold_heads(_dot3(t_inv[p], rhs[p], _NN))


def _rwkv_chunk_scan_kernel(nb_count, w1_ref, w2_ref, qr_ref, arb_ref, y0_ref, bd_ref, sadd_ref, gc_ref,
                            g_ref, bon_ref, s0_ref, lg_ref, lb_ref, seg512_ref, out_ref, st_ref, sx_scr):
    c = CHUNK
    chunk_id = pl.program_id(1)
    first = lax.broadcasted_iota(jnp.int32, (1, PAIR), 1) < HEAD_DIM
    rows = lax.broadcasted_iota(jnp.int32, (PAIR, PAIR), 0)
    cols = lax.broadcasted_iota(jnp.int32, (PAIR, PAIR), 1)
    same = (rows // HEAD_DIM) == (cols // HEAD_DIM)

    @pl.when(chunk_id == 0)
    def _():
        for nb in range(nb_count):
            for p in range(N_PAIRS):
                sx_scr[nb, p] = _by_head(s0_ref[nb, :, p * PAIR:(p + 1) * PAIR], first)

    seg512 = seg512_ref[...]
    chains = [(nb, p, slice(p * PAIR, (p + 1) * PAIR)) for nb in range(nb_count) for p in range(N_PAIRS)]
    proj = [_dot3(jnp.concatenate([w1_ref[nb, :, sl], qr_ref[nb, :, sl]], axis=0), sx_scr[nb, p], _NT)
            for nb, p, sl in chains]
    us = [proj[i][:c] + w2_ref[nb, :, sl] for i, (nb, p, sl) in enumerate(chains)]
    ys = [proj[i][c:] - _dot3(arb_ref[nb, :, sl], _by_head(us[i], first), _NN) + y0_ref[nb, :, sl]
          for i, (nb, p, sl) in enumerate(chains)]
    for i, (nb, p, sl) in enumerate(chains):
        upd = jnp.where(same, _dot3(us[i], bd_ref[nb, :, sl], _TN), 0.0)
        sx_scr[nb, p] = sx_scr[nb, p] * gc_ref[nb, :, sl] - upd + sadd_ref[nb, p]
    for nb in range(nb_count):
        y = jnp.concatenate(ys[nb * N_PAIRS:(nb + 1) * N_PAIRS], axis=1)
        mu = _dot2(y, seg512) * (1.0 / HEAD_DIM)
        yc = y - mu
        var = _dot2(yc * yc, seg512) * (1.0 / HEAD_DIM)
        yn = yc * lax.rsqrt(var + GN_EPS) * lg_ref[...] + lb_ref[...]
        out_ref[nb] = (yn + bon_ref[nb]) * g_ref[nb]

    @pl.when(chunk_id == pl.num_programs(1) - 1)
    def _():
        for nb in range(nb_count):
            st_ref[nb] = jnp.concatenate([_fold_heads(sx_scr[nb, p]) for p in range(N_PAIRS)], axis=1)


def _rwkv_chunked(pre, s0, prm, nb):
    kk, ld, b, km, v, r, g, bon = pre
    bsz, t, _ = kk.shape
    n_chunks = t // CHUNK
    idx = jnp.arange(CHUNK)
    tri = (idx[None, :] <= idx[:, None]).astype(BF16)
    blk = pl.BlockSpec((None, CHUNK, WIDTH), lambda i, j: (i, j, 0))
    wide = jax.ShapeDtypeStruct((bsz, t, WIDTH), F32)
    w1, w2, qr, arb, y0, bd, sadd, gc = pl.pallas_call(
        _rwkv_chunk_prep_kernel,
        grid=(bsz, n_chunks),
        in_specs=[blk] * 6 + [pl.BlockSpec((CHUNK, CHUNK), lambda i, j: (0, 0))],
        out_specs=[blk] * 6 + [pl.BlockSpec((None, None, N_PAIRS, PAIR, PAIR), lambda i, j: (i, j, 0, 0, 0)),
                               pl.BlockSpec((None, None, 1, WIDTH), lambda i, j: (i, j, 0, 0))],
        out_shape=[wide] * 6 + [jax.ShapeDtypeStruct((bsz, n_chunks, N_PAIRS, PAIR, PAIR), F32),
                                jax.ShapeDtypeStruct((bsz, n_chunks, 1, WIDTH), F32)],
        compiler_params=_cparams("parallel", "parallel"),
        name="rwkv_chunk_prep",
    )(kk, ld, b, km, v, r, tri)
    nblk = pl.BlockSpec((nb, CHUNK, WIDTH), lambda i, j: (i, j, 0))
    sblk = pl.BlockSpec((nb, HEAD_DIM, WIDTH), lambda i, j: (i, 0, 0))
    full = lambda a: pl.BlockSpec(a.shape, lambda i, j: (0,) * a.ndim)
    params = [prm["lnx_g"], prm["lnx_b"], prm["seg512"]]
    return pl.pallas_call(
        functools.partial(_rwkv_chunk_scan_kernel, nb),
        grid=(bsz // nb, n_chunks),
        in_specs=[nblk] * 6
        + [pl.BlockSpec((nb, None, N_PAIRS, PAIR, PAIR), lambda i, j: (i, j, 0, 0, 0)),
           pl.BlockSpec((nb, None, 1, WIDTH), lambda i, j: (i, j, 0, 0)),
           nblk, nblk, sblk] + [full(a) for a in params],
        out_specs=[nblk, sblk],
        out_shape=[wide, jax.ShapeDtypeStruct((bsz, HEAD_DIM, WIDTH), F32)],
        scratch_shapes=[pltpu.VMEM((nb, N_PAIRS, PAIR, PAIR), F32)],
        compiler_params=_cparams("parallel", "arbitrary"),
        name="rwkv_chunk_scan",
    )(w1, w2, qr, arb, y0, bd, sadd, gc, g, bon, s0, *params)


def _rwkv(p, shift_prev, wkv0, prm, nb, tb):
    bsz, t, _ = p.shape
    s0 = _state_to_rows(wkv0)
    if t == 1:
        pre = _rwkv_pre(p.reshape(1, bsz, RWKV_PROJ), shift_prev.reshape(1, bsz, RWKV_PROJ), True, False, prm, bsz)
        out, st = _rwkv_scan([a.reshape(bsz, 1, WIDTH) for a in pre], s0, prm, nb, 1)
    else:
        pre = _rwkv_pre(p, shift_prev.reshape(bsz, 1, RWKV_PROJ), False, True, prm, tb)
        out, st = _rwkv_chunked(pre, s0, prm, nb)
    return out, _rows_to_state(st)


def _sb_tile(z, mask, carry, neg_ge, neg_ones):
    sp = _softplus(z)
    if mask is not None:
        sp = jnp.where(mask, sp, 0.0)
    spb = sp.astype(BF16)
    inc = _dot(spb, neg_ge)
    bk = z.shape[1]
    a = jnp.exp(z + inc + jnp.concatenate([carry] * (bk // carry.shape[1]), axis=1))
    if mask is not None:
        a = jnp.where(mask, a, 0.0)
    if neg_ones is None:
        return a.astype(BF16), carry + jnp.broadcast_to(inc[:, :1], carry.shape)
    return a.astype(BF16), carry + _dot(spb, neg_ones)


def _sb_prompt_kernel(bq, bk, bias_ref, q_ref, k_ref, v_ref, mge_ref, o_ref):
    hp = pl.program_id(1)
    qi = pl.program_id(2)
    lane = lax.broadcasted_iota(jnp.int32, (1, 2 * HEAD_DIM), 1)
    first = lane < HEAD_DIM
    q2 = q_ref[...] * jnp.asarray(SB_SCALE, BF16)
    zero = jnp.zeros_like(q2)
    q_heads = (jnp.where(first, q2, zero), jnp.where(first, zero, q2))
    biases = (bias_ref[2 * hp], bias_ref[2 * hp + 1])
    m_ge = mge_ref[...]
    per_q = bq // bk
    rows = lax.broadcasted_iota(jnp.int32, (bq, bk), 0)
    cols = lax.broadcasted_iota(jnp.int32, (bq, bk), 1)

    def tile(j, mask, state):
        acc, carries = state
        kblk = k_ref[pl.ds(pl.multiple_of(j * bk, bk), bk), :]
        vblk = v_ref[pl.ds(pl.multiple_of(j * bk, bk), bk), :]
        vzero = jnp.zeros_like(vblk)
        v_heads = (jnp.where(first, vblk, vzero), jnp.where(first, vzero, vblk))
        new_carries = []
        for e in range(2):
            z = _dot_nt(q_heads[e], kblk) + biases[e]
            a, c = _sb_tile(z, mask, carries[e], m_ge, None)
            acc = acc + _dot(a, v_heads[e])
            new_carries.append(c)
        return acc, tuple(new_carries)

    zc = jnp.zeros((bq, 2 * HEAD_DIM), F32)
    state = (zc, (zc, zc))
    for u in reversed(range(per_q)):
        state = tile(qi * per_q + u, cols + u * bk < rows, state)
    n_full = qi * per_q

    def full_tiles(i, s):
        for u in range(per_q):
            s = tile(n_full - 1 - i * per_q - u, None, s)
        return s

    state = lax.fori_loop(0, qi, full_tiles, state)
    o_ref[...] = state[0]


def _sb_prompt(q, k, v, sb_bias, bq, bk):
    bsz, t, _ = q.shape
    pair = 2 * HEAD_DIM
    idx = jnp.arange(bk)
    m_ge = -(idx[:, None] >= idx[None, :]).astype(BF16)
    return pl.pallas_call(
        functools.partial(_sb_prompt_kernel, bq, bk),
        grid=(bsz, WIDTH // pair, t // bq),
        in_specs=[pl.BlockSpec(memory_space=pltpu.SMEM),
                  pl.BlockSpec((None, bq, pair), lambda b, h, i: (b, i, h)),
                  pl.BlockSpec((None, t, pair), lambda b, h, i: (b, 0, h)),
                  pl.BlockSpec((None, t, pair), lambda b, h, i: (b, 0, h)),
                  pl.BlockSpec((bk, bk), lambda b, h, i: (0, 0))],
        out_specs=pl.BlockSpec((None, bq, pair), lambda b, h, i: (b, i, h)),
        out_shape=jax.ShapeDtypeStruct((bsz, t, WIDTH), F32),
        compiler_params=_cparams("parallel", "parallel", "arbitrary"),
        name="sb_prompt",
    )(sb_bias.astype(F32), q, k, v, m_ge)


def _sb_sample_kernel(pp, n_pages, pt_ref, q_ref, knew_ref, vnew_ref, bias_ref, mge_ref, ones_ref, *refs):
    k_refs, v_refs = refs[:pp], refs[pp:2 * pp]
    o_ref, carry_scr, acc_scr = refs[2 * pp:]
    g = pl.program_id(1)

    @pl.when(g == 0)
    def _():
        carry_scr[...] = jnp.zeros_like(carry_scr)
        acc_scr[...] = jnp.zeros_like(acc_scr)

    head_row = lax.broadcasted_iota(jnp.int32, (N_HEADS, PAGE_SIZE), 0)
    head_row_d = lax.broadcasted_iota(jnp.int32, (N_HEADS, HEAD_DIM), 0)
    q = q_ref[...]
    qb = (q * SB_SCALE).astype(BF16)
    bias = bias_ref[...]
    zs = []
    for u in range(pp):
        z = None
        for h in range(N_HEADS):
            zh = _dot(qb, k_refs[u][h].astype(BF16))
            z = zh if z is None else jnp.where(head_row == h, zh, z)
        zs.append(z + bias)
    z_all = jnp.concatenate(zs, axis=1)
    spb = _softplus(z_all).astype(BF16)
    incs, carries = [], []
    carry = carry_scr[...]
    for u in range(pp):
        page = spb[:, u * PAGE_SIZE:(u + 1) * PAGE_SIZE]
        incs.append(_dot(page, mge_ref[...]))
        carries.append(carry)
        carry = carry + _dot(page, ones_ref[...])
    carry_scr[...] = carry
    a_all = jnp.exp(z_all + jnp.concatenate(incs, axis=1) + jnp.concatenate(carries, axis=1)).astype(BF16)
    acc = acc_scr[...]
    for u in range(pp):
        a = a_all[:, u * PAGE_SIZE:(u + 1) * PAGE_SIZE]
        for h in range(N_HEADS):
            acc = acc + jnp.where(head_row_d == h, _dot_nt(a, v_refs[u][h].astype(BF16)), 0.0)
    acc_scr[...] = acc

    @pl.when(g == pl.num_programs(1) - 1)
    def _():
        past = n_pages * PAGE_SIZE
        z_new = jnp.sum(q * knew_ref[...], axis=1, keepdims=True) * SB_SCALE + bias[:, :1]
        a_new = jnp.where(past < past, jnp.exp(-_softplus(-z_new)), 0.0)
        o_ref[...] = acc + a_new * vnew_ref[...]


def _sb_sample(q, k_new, v_new, sb_bias, cache_k, cache_v, page_table, pp):
    bsz = q.shape[0]
    n_pages = page_table.shape[1]
    idx = jnp.arange(PAGE_SIZE)
    m_ge = -(idx[:, None] >= idx[None, :]).astype(BF16)
    ones_cols = -jnp.ones((PAGE_SIZE, PAGE_SIZE), BF16)
    bias = jnp.broadcast_to(sb_bias.astype(F32)[:, None], (N_HEADS, PAGE_SIZE))
    rows = lambda c: c.transpose(0, 2, 3, 1)
    row = pl.BlockSpec((None, N_HEADS, HEAD_DIM), lambda b, g, pt: (b, 0, 0))
    full = lambda a: pl.BlockSpec(a.shape, lambda b, g, pt: (0,) * a.ndim)

    def page_spec(u):
        return pl.BlockSpec((None, N_HEADS, HEAD_DIM, PAGE_SIZE),
                            lambda b, g, pt: (pt[b * n_pages + n_pages - 1 - (g * pp + u)], 0, 0, 0))

    grid_spec = pltpu.PrefetchScalarGridSpec(
        num_scalar_prefetch=1,
        grid=(bsz, n_pages // pp),
        in_specs=[row, row, row, full(bias), full(m_ge), full(ones_cols)]
        + [page_spec(u) for u in range(pp)] * 2,
        out_specs=row,
        scratch_shapes=[pltpu.VMEM((N_HEADS, PAGE_SIZE), F32), pltpu.VMEM((N_HEADS, HEAD_DIM), F32)],
    )
    return pl.pallas_call(
        functools.partial(_sb_sample_kernel, pp, n_pages),
        grid_spec=grid_spec,
        out_shape=jax.ShapeDtypeStruct((bsz, N_HEADS, HEAD_DIM), F32),
        compiler_params=_cparams("parallel", "arbitrary"),
        name="sb_sample",
    )(page_table.reshape(-1), q, k_new, v_new, bias, m_ge, ones_cols, *([rows(cache_k)] * pp), *([rows(cache_v)] * pp))


def _outproj_kernel(rw_ref, sb_ref, x_ref, g1_ref, sh2_ref, sc2_ref, wo_ref, lg_ref, lb_ref, wpq_ref,
                    x1_ref, h2_ref, qp_ref):
    mix = _dot(rw_ref[...].astype(BF16), wo_ref[:WIDTH, :]) + _dot(sb_ref[...].astype(BF16), wo_ref[WIDTH:, :])
    x1 = _ln_rows(DN_ALPHA * x_ref[...] + g1_ref[...] * mix) * lg_ref[...] + lb_ref[...]
    h2 = _ln_rows(x1) * (1.0 + sc2_ref[...]) + sh2_ref[...]
    x1_ref[...] = x1
    h2_ref[...] = h2
    qp_ref[...] = _dot(h2.astype(BF16), wpq_ref[...]).astype(BF16)


def _mod_spec(a, rows_per_mod, rb, d):
    if rows_per_mod == 1:
        return a, pl.BlockSpec((rb, d), lambda i: (i, 0))
    per = rows_per_mod // rb
    return a.reshape(-1, 1, d), pl.BlockSpec((None, 1, d), lambda i: (i // per, 0, 0))


def _outproj(rw, sb, x, g1, sh2, sc2, w_out_bf16, ln_g, ln_b, w_pq_bf16, rows_per_mod, rb):
    n, d = x.shape
    row = lambda c: pl.BlockSpec((rb, c), lambda i: (i, 0))
    full = lambda a: pl.BlockSpec(a.shape, lambda i: (0,) * a.ndim)
    mods, mod_specs = zip(*[_mod_spec(a, rows_per_mod, rb, d) for a in (g1, sh2, sc2)])
    lg, lb = ln_g.reshape(1, d), ln_b.reshape(1, d)
    return pl.pallas_call(
        _outproj_kernel,
        grid=(n // rb,),
        in_specs=[row(WIDTH), row(WIDTH), row(d), *mod_specs, full(w_out_bf16), full(lg), full(lb), full(w_pq_bf16)],
        out_specs=[row(d), row(d), row(d)],
        out_shape=[jax.ShapeDtypeStruct((n, d), F32), jax.ShapeDtypeStruct((n, d), F32),
                   jax.ShapeDtypeStruct((n, d), BF16)],
        compiler_params=_cparams("parallel"),
        name="outproj",
    )(rw, sb, x, *mods, w_out_bf16, lg, lb, w_pq_bf16)


def _take_top(x, ids, payload, count):
    vals, picked = [], []
    for _ in range(count):
        m = jnp.max(x, axis=0, keepdims=True)
        pos = jnp.min(jnp.where(x == m, ids, jnp.iinfo(jnp.int32).max), axis=0, keepdims=True)
        hit = ids == pos
        vals.append(m)
        picked.append(pos if payload is None else jnp.sum(jnp.where(hit, payload, 0), axis=0, keepdims=True))
        x = jnp.where(hit, -jnp.inf, x)
    return jnp.concatenate(vals, axis=0), jnp.concatenate(picked, axis=0)


def _pair_candidates(sv0, si0, sv1, si1):
    k = PEER_TOPK
    tokens = sv0.shape[1]
    wide = 4
    vals, ids, experts = [], [], []
    for a in range(wide):
        n = -(-(k // (a + 1)) // 8) * 8
        b_ids = lax.broadcasted_iota(jnp.int32, (n, tokens), 0)
        valid = b_ids < k // (a + 1)
        vals.append(jnp.where(valid, sv0[a:a + 1] + sv1[:n], -jnp.inf))
        ids.append(jnp.where(valid, a * k + b_ids, -1))
        experts.append(si0[a:a + 1] * N_KEYS + si1[:n])
    for b in range(k // (wide + 1)):
        last_a = k // (b + 1) - 1
        n = -(-(last_a + 1) // 8) * 8
        a_ids = lax.broadcasted_iota(jnp.int32, (n, tokens), 0)
        valid = (a_ids >= wide) & (a_ids <= last_a)
        vals.append(jnp.where(valid, sv0[:n] + sv1[b:b + 1], -jnp.inf))
        ids.append(jnp.where(valid, a_ids * k + b, -1))
        experts.append(si0[:n] * N_KEYS + si1[b:b + 1])
    return jnp.concatenate(vals, axis=0), jnp.concatenate(ids, axis=0), jnp.concatenate(experts, axis=0)


def _route_kernel(qp_ref, sk_ref, e_ref, g_ref):
    tokens = qp_ref.shape[0]
    scores = _dot_nt(sk_ref[...], qp_ref[...])
    key_ids = lax.broadcasted_iota(jnp.int32, (N_KEYS, tokens), 0)
    sv0, si0 = _take_top(scores[:N_KEYS], key_ids, None, PEER_TOPK)
    sv1, si1 = _take_top(scores[N_KEYS:], key_ids, None, PEER_TOPK)
    cand, cand_ids, cidx = _pair_candidates(sv0, si0, sv1, si1)
    top, eidx = _take_top(cand, cand_ids, cidx, PEER_TOPK)
    ex = jnp.exp(top - top[:1])
    e_ref[...] = eidx
    g_ref[...] = ex / jnp.sum(ex, axis=0, keepdims=True)


def _route(qp, sk_pairs, tb):
    n = qp.shape[0]
    blk = pl.BlockSpec((None, PEER_TOPK, tb), lambda i, h: (h, 0, i))
    return pl.pallas_call(
        _route_kernel,
        grid=(n // tb, PEER_HEADS),
        in_specs=[pl.BlockSpec((tb, 2 * HEAD_DIM), lambda i, h: (i, h)),
                  pl.BlockSpec((None, 2 * N_KEYS, 2 * HEAD_DIM), lambda i, h: (h, 0, 0))],
        out_specs=[blk, blk],
        out_shape=[jax.ShapeDtypeStruct((PEER_HEADS, PEER_TOPK, n), jnp.int32),
                   jax.ShapeDtypeStruct((PEER_HEADS, PEER_TOPK, n), F32)],
        compiler_params=_cparams("parallel", "parallel"),
        name="peer_route",
    )(qp, sk_pairs)


def _sub_key_pairs(sub_keys):
    z = jnp.zeros_like(sub_keys[:, 0])
    top = jnp.concatenate([sub_keys[:, 0], z], axis=-1)
    bot = jnp.concatenate([z, sub_keys[:, 1]], axis=-1)
    return jnp.concatenate([top, bot], axis=1).astype(BF16)


ROW_WORDS = D_MODEL // 2
ROW_PLANES = ROW_WORDS // 128
PLANE_STRIDE = PEER_SLOTS + 8


def _pack_table(w):
    bits = lax.bitcast_convert_type(w.astype(BF16), jnp.uint16).astype(jnp.uint32)
    packed = bits[:, :ROW_WORDS] | (bits[:, ROW_WORDS:] << 16)
    return lax.bitcast_convert_type(packed, jnp.int32).reshape(-1, 128)


def _gather_rows(idx_ref, base, tbl_ref, buf_ref):
    token_idx = idx_ref.at[pl.ds(base, PEER_SLOTS)]
    for m in range(PEER_SLOTS):
        start = pl.multiple_of(token_idx[m], ROW_PLANES)
        buf_ref[pl.ds(m, ROW_PLANES, stride=PLANE_STRIDE), :] = tbl_ref[pl.ds(start, ROW_PLANES), :]


def _plane_halves(buf_ref, j):
    words = buf_ref[pl.ds(j * PLANE_STRIDE, PEER_SLOTS), :]
    lo = lax.bitcast_convert_type(words << 16, F32)
    hi = lax.bitcast_convert_type(words & jnp.int32(-65536), F32)
    return lo, hi


PEER_GROUP = 2


def _token_groups(tb, idx_ref, tbl_ref, buf_ref, compute, init):
    _gather_rows(idx_ref, 0, tbl_ref, buf_ref.at[0])

    def group(i, carry):
        for pos in range(PEER_GROUP):
            t = PEER_GROUP * i + pos
            nxt = t + 1 if pos + 1 < PEER_GROUP else jnp.minimum(t + 1, tb - 1)
            _gather_rows(idx_ref, nxt * PEER_SLOTS, tbl_ref, buf_ref.at[(pos + 1) % PEER_GROUP])
            carry = compute(t, buf_ref.at[pos], pos, carry)
        return carry

    return lax.fori_loop(0, tb // PEER_GROUP, group, init)


def _peer_act_kernel(tb, idx_ref, h_ref, tbl_ref, act_ref, buf_ref, acc_ref):
    token_lane = lax.broadcasted_iota(jnp.int32, (PEER_SLOTS, tb), 1)
    acc_ref[...] = jnp.zeros_like(acc_ref)

    def finish(t, pos, acts):
        return jnp.where(token_lane == t, jnp.sum(acc_ref[pos], axis=1, keepdims=True), acts)

    def compute(t, buf, pos, acts):
        acts = finish(t - PEER_GROUP, pos, acts)
        hrow = h_ref[pl.ds(t, 1), :]
        acc = None
        for j in range(ROW_PLANES):
            lo, hi = _plane_halves(buf, j)
            term = (lo * hrow[:, j * 128:(j + 1) * 128]
                    + hi * hrow[:, ROW_WORDS + j * 128:ROW_WORDS + (j + 1) * 128])
            acc = term if acc is None else acc + term
        acc_ref[pos] = acc
        return acts

    acts = _token_groups(tb, idx_ref, tbl_ref, buf_ref, compute, jnp.zeros((PEER_SLOTS, tb), F32))
    for pos in range(PEER_GROUP):
        acts = finish(tb - PEER_GROUP + pos, pos, acts)
    act_ref[...] = acts


def _peer_out_kernel(tb, idx_ref, act_ref, gate_ref, tbl_ref, f_ref, buf_ref, coef_ref, col_ref):
    act = act_ref[...]
    coef_ref[...] = gate_ref[...] * (0.5 * act * (1.0 + lax.erf(act * (2.0 ** -0.5))))
    token_lane = lax.broadcasted_iota(jnp.int32, (PEER_SLOTS, tb), 1)

    def stage_coef(t, pos):
        col = jnp.sum(jnp.where(token_lane == t, coef_ref[...], 0.0), axis=1, keepdims=True)
        col_ref[pos] = jnp.broadcast_to(col, (PEER_SLOTS, 128))

    for pos in range(PEER_GROUP):
        stage_coef(pos, pos)

    def compute(t, buf, pos, carry):
        coef = col_ref[pos]
        los, his = [], []
        for j in range(ROW_PLANES):
            lo, hi = _plane_halves(buf, j)
            los.append(jnp.sum(lo * coef, axis=0, keepdims=True))
            his.append(jnp.sum(hi * coef, axis=0, keepdims=True))
        f_ref[pl.ds(t, 1), :] = jnp.concatenate(los + his, axis=1)
        stage_coef(t + PEER_GROUP, pos)
        return carry

    _token_groups(tb, idx_ref, tbl_ref, buf_ref, compute, 0)


def _peer_experts(h2, idx_flat, gate_t, tbl_u, tbl_v, tb):
    n, d = h2.shape
    smem = pl.BlockSpec((tb * PEER_SLOTS,), lambda i: (i,), memory_space=pltpu.SMEM)
    row = pl.BlockSpec((tb, d), lambda i: (i, 0))
    slot_major = pl.BlockSpec((PEER_SLOTS, tb), lambda i: (0, i))
    table = pl.BlockSpec(memory_space=pltpu.VMEM)
    buf = pltpu.VMEM((PEER_GROUP, ROW_PLANES * PLANE_STRIDE, 128), jnp.int32)
    staged = pltpu.VMEM((PEER_GROUP, PEER_SLOTS, 128), F32)
    act_t = pl.pallas_call(
        functools.partial(_peer_act_kernel, tb),
        grid=(n // tb,),
        in_specs=[smem, row, table],
        out_specs=slot_major,
        out_shape=jax.ShapeDtypeStruct((PEER_SLOTS, n), F32),
        scratch_shapes=[buf, staged],
        compiler_params=_cparams("arbitrary"),
        name="peer_act",
    )(idx_flat, h2, tbl_u)
    return pl.pallas_call(
        functools.partial(_peer_out_kernel, tb),
        grid=(n // tb,),
        in_specs=[smem, slot_major, slot_major, table],
        out_specs=row,
        out_shape=jax.ShapeDtypeStruct((n, d), F32),
        scratch_shapes=[buf, pltpu.VMEM((PEER_SLOTS, tb), F32), staged],
        compiler_params=_cparams("arbitrary"),
        name="peer_out",
    )(idx_flat, act_t, gate_t, tbl_v)


def _final_kernel(x1_ref, f_ref, g2_ref, lg_ref, lb_ref, y_ref):
    y_ref[...] = _ln_rows(DN_ALPHA * x1_ref[...] + g2_ref[...] * f_ref[...]) * lg_ref[...] + lb_ref[...]


def _final(x1, f, f_first_row, g2, ln_g, ln_b, rows_per_mod, rb):
    n, d = x1.shape
    row = pl.BlockSpec((rb, d), lambda i: (i, 0))
    first_blk = f_first_row // rb
    assert first_blk * rb == f_first_row
    f_spec = pl.BlockSpec((rb, d), lambda i: (i + first_blk, 0))
    g2, g2_spec = _mod_spec(g2, rows_per_mod, rb, d)
    vec = pl.BlockSpec((1, d), lambda i: (0, 0))
    return pl.pallas_call(
        _final_kernel,
        grid=(n // rb,),
        in_specs=[row, f_spec, g2_spec, vec, vec],
        out_specs=row,
        out_shape=jax.ShapeDtypeStruct((n, d), F32),
        compiler_params=_cparams("parallel"),
        name="final_ln",
    )(x1, f, g2, ln_g.reshape(1, d), ln_b.reshape(1, d))


ROW_BLOCK = 256
SCAN_BATCH = 4
SCAN_BLOCK = 128
SB_BLOCK = 512
SB_KEY_BLOCK = 256
SB_PAGES = 8
ROUTE_BLOCK = 128
PEER_BLOCK = 128


def _mixer_half(x, mods, shift_prev, wkv0, attend, wts, rows_per_mod, rb):
    bsz, t, d = x.shape
    n = bsz * t
    sh1, sc1, g1, sh2, sc2, _ = mods
    xr = x.reshape(n, d)
    p, q, k, v, kb, vb = _inproj(xr, sh1, sc1, wts["w_in"], rows_per_mod, rb)
    p3 = p.reshape(bsz, t, RWKV_PROJ)
    rw, wkv_new = _rwkv(p3, shift_prev, wkv0, wts["rwkv"], SCAN_BATCH, SCAN_BLOCK)
    sb = attend(q, k, v, kb, vb)
    x1, h2, qp = _outproj(rw.reshape(n, WIDTH), sb.reshape(n, WIDTH), xr, g1, sh2, sc2, wts["w_out"],
                          wts["ln1_g"], wts["ln1_b"], wts["w_pq"], rows_per_mod, rb)
    heads = lambda a: a.reshape(bsz, t, N_HEADS, HEAD_DIM)
    return x1, h2, qp, heads(k), heads(v), wkv_new, p3[:, -1]


def _layer_pair(xp, xs, mod_p, mod_s, cache_k, cache_v, page_table, state_wkv, state_shift, wts):
    bp, tp, d = xp.shape
    bs, ts, _ = xs.shape
    assert ts == 1, "the paged attention handles one new token per sequence"
    n_p = bp * tp

    def attend_p(q, k, v, kb, vb):
        r = lambda a: a.reshape(bp, tp, WIDTH)
        return _sb_prompt(r(q), r(kb), r(vb), wts["sb_bias"], SB_BLOCK, SB_KEY_BLOCK)

    def attend_s(q, k, v, kb, vb):
        heads = lambda a: a.astype(F32).reshape(bs, N_HEADS, HEAD_DIM)
        return _sb_sample(heads(q), heads(k), heads(v), wts["sb_bias"], cache_k, cache_v, page_table, SB_PAGES)

    zero_shift = jnp.zeros((bp, RWKV_PROJ), xp.dtype)
    zero_wkv = jnp.zeros((bp, N_HEADS, HEAD_DIM, HEAD_DIM), state_wkv.dtype)
    x1p, h2p, qpp, kp, vp, wp, sp = _mixer_half(xp, mod_p, zero_shift, zero_wkv, attend_p, wts, tp, ROW_BLOCK)
    x1s, h2s, qps, ks, vs, ws, ss = _mixer_half(xs, mod_s, state_shift, state_wkv, attend_s, wts, 1, bs)

    h2 = jnp.concatenate([h2p, h2s], axis=0)
    qp = jnp.concatenate([qpp, qps], axis=0)
    n = h2.shape[0]
    eidx, gate = _route(qp, wts["sub_keys"], ROUTE_BLOCK)
    idx_flat = eidx.transpose(2, 0, 1).reshape(-1) * ROW_PLANES
    f = _peer_experts(h2, idx_flat, gate.reshape(PEER_SLOTS, n), wts["peer_u"], wts["peer_v"], PEER_BLOCK)
    yp = _final(x1p, f, 0, mod_p[5], wts["ln2_g"], wts["ln2_b"], tp, ROW_BLOCK).reshape(bp, tp, d)
    ys = _final(x1s, f, n_p, mod_s[5], wts["ln2_g"], wts["ln2_b"], 1, bs).reshape(bs, ts, d)
    return yp, ys, kp, vp, ks, vs, wp, ws, sp, ss


def kernel(x_prompt, x_sample, c_prompt, c_sample, cache_k, cache_v, page_table, state_wkv, state_shift,
           w_cond, b_cond, w_in, mu_shift, w0, w_up, a0, a_up, g_up, k_k, k_a, r_k, lnx_g, lnx_b,
           sb_bias, w_out, ln1_g, ln1_b, w_pq, sub_keys, peer_u, peer_v, ln2_g, ln2_b):
    depth = w_in.shape[0]
    bp, bs = c_prompt.shape[0], c_sample.shape[0]
    pad = (-(bp + bs)) % 8
    c_all = jnp.concatenate([c_prompt, c_sample, jnp.zeros((pad, c_prompt.shape[1]), c_prompt.dtype)], axis=0)
    yp, ys = x_prompt, x_sample
    outs = [[] for _ in range(8)]
    for l in range(depth):
        wts = dict(
            w_in=w_in[l].astype(BF16), w_out=w_out[l].astype(BF16), w_pq=w_pq[l].astype(BF16),
            rwkv=_rwkv_params(mu_shift[l], w0[l], w_up[l], a0[l], a_up[l], g_up[l], k_k[l], k_a[l], r_k[l],
                              lnx_g[l], lnx_b[l]),
            sb_bias=sb_bias[l], ln1_g=ln1_g[l], ln1_b=ln1_b[l], ln2_g=ln2_g[l], ln2_b=ln2_b[l],
            sub_keys=_sub_key_pairs(sub_keys[l]), peer_u=_pack_table(peer_u[l]), peer_v=_pack_table(peer_v[l]))
        mod = _cond(c_all, w_cond[l], b_cond[l])
        mod_p = jnp.split(mod[:bp], N_MOD, axis=-1)
        mod_s = jnp.split(mod[bp:bp + bs], N_MOD, axis=-1)
        res = _layer_pair(yp, ys, mod_p, mod_s, cache_k[l], cache_v[l], page_table, state_wkv[l], state_shift[l], wts)
        yp, ys = res[0], res[1]
        for acc, val in zip(outs, res[2:]):
            acc.append(val)
    return (yp, ys) + tuple(jnp.stack(o) for o in outs)
```

```python
import functools

import jax
import jax.numpy as jnp
from jax import lax
from jax.experimental import pallas as pl
from jax.experimental.pallas import tpu as pltpu

F32 = jnp.float32
BF16 = jnp.bfloat16

D_MODEL = 1024
HEAD_DIM = 64
N_HEADS = 8
WIDTH = N_HEADS * HEAD_DIM
W_LORA, A_LORA, G_LORA = 64, 64, 128
RWKV_PROJ = 3 * WIDTH + W_LORA + A_LORA + G_LORA
IN_COLS = RWKV_PROJ + 3 * WIDTH
GN_EPS = HEAD_DIM * 1e-5
LN_EPS = 1e-5
SB_SCALE = HEAD_DIM ** -0.5
PAGE_SIZE = 128
PEER_HEADS = 8
N_KEYS = 128
PEER_TOPK = 16
PEER_SLOTS = PEER_HEADS * PEER_TOPK
N_MOD = 6
DEPTH = 1
DN_ALPHA = (2 * DEPTH) ** 0.25

VMEM_LIMIT = 56 * 1024 * 1024


def _cparams(*sem):
    return pltpu.CompilerParams(dimension_semantics=sem, vmem_limit_bytes=VMEM_LIMIT)


def _ln_rows(x):
    mu = jnp.mean(x, axis=-1, keepdims=True)
    xc = x - mu
    var = jnp.mean(xc * xc, axis=-1, keepdims=True)
    return xc * lax.rsqrt(var + LN_EPS)


def _split_bf16(x):
    hi = x.astype(BF16)
    lo = (x - hi.astype(F32)).astype(BF16)
    return hi, lo


def _dot(a, b):
    return jnp.dot(a, b, preferred_element_type=F32)


def _dot_nt(a, b):
    return lax.dot_general(a, b, (((1,), (1,)), ((), ())), preferred_element_type=F32)


def _dot2(x, w_bf16):
    hi, lo = _split_bf16(x)
    return _dot(hi, w_bf16) + _dot(lo, w_bf16)


def _cond_kernel(c_ref, w_ref, b_ref, o_ref):
    c = c_ref[...]
    s = c * jax.nn.sigmoid(c)
    o_ref[...] = jnp.dot(s, w_ref[...], preferred_element_type=F32,
                         precision=lax.Precision.HIGHEST) + b_ref[...]


def _cond(c, w_cond, b_cond):
    n, d = c.shape
    cols = w_cond.shape[1]
    bn = 1024
    return pl.pallas_call(
        _cond_kernel,
        grid=(cols // bn,),
        in_specs=[pl.BlockSpec((n, d), lambda j: (0, 0)),
                  pl.BlockSpec((d, bn), lambda j: (0, j)),
                  pl.BlockSpec((1, bn), lambda j: (0, j))],
        out_specs=pl.BlockSpec((n, bn), lambda j: (0, j)),
        out_shape=jax.ShapeDtypeStruct((n, cols), F32),
        compiler_params=_cparams("parallel"),
        name="cond",
    )(c, w_cond, b_cond.reshape(1, cols))


def _inproj_kernel(x_ref, sh_ref, sc_ref, w_ref, p_ref, q_ref, k_ref, v_ref, kb_ref, vb_ref):
    h = _ln_rows(x_ref[...]) * (1.0 + sc_ref[...]) + sh_ref[...]
    hb = h.astype(BF16)
    p_ref[...] = _dot(hb, w_ref[:, :RWKV_PROJ])
    q_ref[...] = _dot(hb, w_ref[:, RWKV_PROJ:RWKV_PROJ + WIDTH]).astype(BF16)
    k = _dot(hb, w_ref[:, RWKV_PROJ + WIDTH:RWKV_PROJ + 2 * WIDTH])
    v = _dot(hb, w_ref[:, RWKV_PROJ + 2 * WIDTH:])
    k_ref[...] = k
    v_ref[...] = v
    kb_ref[...] = k.astype(BF16)
    vb_ref[...] = v.astype(BF16)


def _inproj(x, shift, scale, w_in_bf16, rows_per_mod, block_rows):
    n, d = x.shape
    rb = block_rows
    if rows_per_mod == 1:
        mod_spec = pl.BlockSpec((rb, d), lambda i: (i, 0))
    else:
        per = rows_per_mod // rb
        shift = shift.reshape(-1, 1, d)
        scale = scale.reshape(-1, 1, d)
        mod_spec = pl.BlockSpec((None, 1, d), lambda i: (i // per, 0, 0))
    row = lambda c: pl.BlockSpec((rb, c), lambda i: (i, 0))
    return pl.pallas_call(
        _inproj_kernel,
        grid=(n // rb,),
        in_specs=[row(d), mod_spec, mod_spec,
                  pl.BlockSpec((d, IN_COLS), lambda i: (0, 0))],
        out_specs=[row(RWKV_PROJ), row(WIDTH), row(WIDTH), row(WIDTH), row(WIDTH), row(WIDTH)],
        out_shape=[jax.ShapeDtypeStruct((n, RWKV_PROJ), F32),
                   jax.ShapeDtypeStruct((n, WIDTH), BF16),
                   jax.ShapeDtypeStruct((n, WIDTH), F32),
                   jax.ShapeDtypeStruct((n, WIDTH), F32),
                   jax.ShapeDtypeStruct((n, WIDTH), BF16),
                   jax.ShapeDtypeStruct((n, WIDTH), BF16)],
        compiler_params=_cparams("parallel"),
        name="inproj",
    )(x, shift, scale, w_in_bf16)


def _seg_ones(n):
    i = jnp.arange(n) // HEAD_DIM
    return (i[:, None] == i[None, :]).astype(BF16)


def _softplus(u):
    return jnp.maximum(u, 0.0) + jnp.log(1.0 + jnp.exp(-jnp.abs(u)))


def _rwkv_pre_kernel(has_prev, chunked, p_ref, prev_ref, mu_ref, w0_ref, wup_ref, a0_ref, aup_ref, gup_ref,
                     kk_ref, ka_ref, rk_ref, seg_ref, *refs):
    out_refs, carry_ref = refs[:-1], refs[-1]
    hi = lax.Precision.HIGHEST
    pf = p_ref[...]
    if has_prev:
        prev = prev_ref[...]
    else:
        tb = pl.program_id(1)
        first = jnp.where(tb == 0, prev_ref[...], carry_ref[...])
        rows = lax.broadcasted_iota(jnp.int32, pf.shape, 0)
        prev = jnp.where(rows == 0, first, pltpu.roll(pf, 1, axis=0))
        carry_ref[...] = pf[pf.shape[0] - 1:, :]
    pm = pf + (prev - pf) * mu_ref[...]
    r = pm[:, :WIDTH]
    k = pm[:, WIDTH:2 * WIDTH]
    v = pm[:, 2 * WIDTH:3 * WIDTH]
    dwa = pm[:, 3 * WIDTH:3 * WIDTH + W_LORA + A_LORA]
    dg = pm[:, 3 * WIDTH + W_LORA + A_LORA:]
    seg = seg_ref[...]
    w = -_softplus(-(w0_ref[...] + jnp.dot(jnp.tanh(dwa), wup_ref[...], precision=hi,
                                           preferred_element_type=F32))) - 0.5
    log_dec = -jnp.exp(w)
    a = jax.nn.sigmoid(a0_ref[...] + jnp.dot(dwa, aup_ref[...], precision=hi, preferred_element_type=F32))
    g = jnp.dot(jax.nn.sigmoid(dg), gup_ref[...], precision=hi, preferred_element_type=F32)
    kkr = k * kk_ref[...]
    kk = kkr * lax.rsqrt(jnp.maximum(_dot2(kkr * kkr, seg), 1e-24))
    km = k * (1.0 + (a - 1.0) * ka_ref[...])
    b = kk * a
    bonus = _dot2(r * km * rk_ref[...], seg) * v
    if chunked:
        outs = (kk, log_dec, b, km, v, r, g, bonus)
    else:
        dec = jnp.exp(log_dec)
        outs = (kk, dec, b, km, v, dec * r, _dot2(b * r, seg), _dot2(km * r, seg), g, bonus)
    for ref, val in zip(out_refs, outs):
        ref[...] = val


def _rwkv_pre(p, prev, has_prev, chunked, prm, tb):
    bsz, t, _ = p.shape
    n_out = 8 if chunked else 10
    blk = lambda c: pl.BlockSpec((None, tb, c), lambda i, j: (i, j, 0))
    full = lambda a: pl.BlockSpec(a.shape, lambda i, j: (0,) * a.ndim)
    prev_spec = blk(RWKV_PROJ) if has_prev else pl.BlockSpec((None, 1, RWKV_PROJ), lambda i, j: (i, 0, 0))
    params = [prm["mu"], prm["w0"], prm["wup"], prm["a0"], prm["aup"], prm["gup"],
              prm["k_k"], prm["k_a"], prm["r_k"], prm["seg512"]]
    return pl.pallas_call(
        functools.partial(_rwkv_pre_kernel, has_prev, chunked),
        grid=(bsz, t // tb),
        in_specs=[blk(RWKV_PROJ), prev_spec] + [full(a) for a in params],
        out_specs=[blk(WIDTH)] * n_out,
        out_shape=[jax.ShapeDtypeStruct((bsz, t, WIDTH), F32)] * n_out,
        scratch_shapes=[pltpu.VMEM((1, RWKV_PROJ), F32)],
        compiler_params=_cparams("parallel", "arbitrary"),
        name="rwkv_pre",
    )(p, prev, *params)


def _rwkv_scan_kernel(nb_count, tb, kk_ref, dec_ref, b_ref, km_ref, v_ref, wr_ref, br_ref, kr_ref,
                      g_ref, bon_ref, s0_ref, lg_ref, lb_ref, seg256_ref, seg512_ref,
                      out_ref, st_ref, s_scr, y_scr):
    step_blk = pl.program_id(1)

    @pl.when(step_blk == 0)
    def _():
        s_scr[...] = s0_ref[...]

    shape = (HEAD_DIM, WIDTH)
    ident = (lax.broadcasted_iota(jnp.int32, shape, 1) & (HEAD_DIM - 1)) == lax.broadcasted_iota(jnp.int32, shape, 0)
    seg = seg256_ref[...]
    half = WIDTH // 2

    def segsum(lhs):
        return jnp.concatenate([_dot(lhs[:, :half], seg), _dot(lhs[:, half:], seg)], axis=1)

    def step(t, carry):
        for nb in range(nb_count):
            row = lambda ref: ref[nb, pl.ds(t, 1), :]
            s = s_scr[nb]
            p_hi, p_lo = _split_bf16(s * row(kk_ref))
            dv = jnp.where(ident, row(v_ref), 0.0).astype(BF16)
            pr = (s * row(wr_ref)).astype(BF16)
            res = segsum(jnp.concatenate([p_hi, p_lo, dv, pr], axis=0))
            sa = res[:HEAD_DIM] + res[HEAD_DIM:2 * HEAD_DIM]
            vcol = res[2 * HEAD_DIM:3 * HEAD_DIM]
            ycol = res[3 * HEAD_DIM:] - sa * row(br_ref)
            s_scr[nb] = s * row(dec_ref) - sa * row(b_ref) + vcol * row(km_ref)
            y_scr[nb, pl.ds(t, 1), :] = (jnp.sum(jnp.where(ident, ycol, 0.0), axis=0, keepdims=True)
                                         + row(v_ref) * row(kr_ref))
        return carry

    lax.fori_loop(0, tb, step, 0)

    seg512 = seg512_ref[...]
    for nb in range(nb_count):
        y = y_scr[nb]
        mu = _dot2(y, seg512) * (1.0 / HEAD_DIM)
        yc = y - mu
        var = _dot2(yc * yc, seg512) * (1.0 / HEAD_DIM)
        yn = yc * lax.rsqrt(var + GN_EPS) * lg_ref[...] + lb_ref[...]
        out_ref[nb] = (yn + bon_ref[nb]) * g_ref[nb]

    @pl.when(step_blk == pl.num_programs(1) - 1)
    def _():
        st_ref[...] = s_scr[...]


def _rwkv_scan(pre, s0, prm, nb, tb):
    kk, dec, b, km, v, wr, br, kr, g, bon = pre
    bsz, t, _ = kk.shape
    blk = pl.BlockSpec((nb, tb, WIDTH), lambda i, j: (i, j, 0))
    sblk = pl.BlockSpec((nb, HEAD_DIM, WIDTH), lambda i, j: (i, 0, 0))
    full = lambda a: pl.BlockSpec(a.shape, lambda i, j: (0,) * a.ndim)
    params = [prm["lnx_g"], prm["lnx_b"], prm["seg256"], prm["seg512"]]
    return pl.pallas_call(
        functools.partial(_rwkv_scan_kernel, nb, tb),
        grid=(bsz // nb, t // tb),
        in_specs=[blk] * 10 + [sblk] + [full(a) for a in params],
        out_specs=[blk, sblk],
        out_shape=[jax.ShapeDtypeStruct((bsz, t, WIDTH), F32),
                   jax.ShapeDtypeStruct((bsz, HEAD_DIM, WIDTH), F32)],
        scratch_shapes=[pltpu.VMEM((nb, HEAD_DIM, WIDTH), F32), pltpu.VMEM((nb, tb, WIDTH), F32)],
        compiler_params=_cparams("parallel", "arbitrary"),
        name="rwkv_scan",
    )(kk, dec, b, km, v, wr, br, kr, g, bon, s0, *params)


def _rwkv_params(mu_shift, w0, w_up, a0, a_up, g_up, k_k, k_a, r_k, lnx_g, lnx_b):
    row = lambda a: a.reshape(1, -1).astype(F32)
    zeros = jnp.zeros((A_LORA, WIDTH), F32)
    return dict(mu=row(mu_shift), w0=row(w0), a0=row(a0), k_k=row(k_k), k_a=row(k_a), r_k=row(r_k),
                lnx_g=row(lnx_g), lnx_b=row(lnx_b), gup=g_up,
                wup=jnp.concatenate([w_up, zeros], axis=0), aup=jnp.concatenate([zeros, a_up], axis=0),
                seg256=_seg_ones(WIDTH // 2), seg512=_seg_ones(WIDTH))


def _state_to_rows(wkv):
    bsz = wkv.shape[0]
    return wkv.transpose(0, 2, 1, 3).reshape(bsz, HEAD_DIM, WIDTH)


def _rows_to_state(s):
    bsz = s.shape[0]
    return s.reshape(bsz, HEAD_DIM, N_HEADS, HEAD_DIM).transpose(0, 2, 1, 3)


CHUNK = 64
PREP_CHUNKS = 4
PAIR = 2 * HEAD_DIM
N_PAIRS = WIDTH // PAIR
_NN = ((1,), (0,))
_NT = ((1,), (1,))
_TN = ((0,), (0,))


def _dot3(a, b, dims):
    ah, al = _split_bf16(a)
    bh, bl = _split_bf16(b)
    dg = lambda x, y: lax.dot_general(x, y, (dims, ((), ())), preferred_element_type=F32)
    return dg(ah, bh) + dg(ah, bl) + dg(al, bh)


def _by_head(x, first):
    zero = jnp.zeros_like(x)
    return jnp.concatenate([jnp.where(first, x, zero), jnp.where(first, zero, x)], axis=0)


def _fold_heads(x):
    c = x.shape[0] // 2
    return x[:c] + x[c:]


def _rwkv_chunk_prep_kernel(kk_ref, ld_ref, b_ref, km_ref, v_ref, r_ref, tri_ref,
                            w1_ref, w2_ref, qr_ref, arb_ref, y0_ref, bd_ref, sadd_ref, gc_ref):
    c = CHUNK
    tri = tri_ref[...]
    first = lax.broadcasted_iota(jnp.int32, (1, PAIR), 1) < HEAD_DIM
    rows = lax.broadcasted_iota(jnp.int32, (PAIR, PAIR), 0)
    cols = lax.broadcasted_iota(jnp.int32, (PAIR, PAIR), 1)
    same = (rows // c) == (cols // c)
    strict = same & (cols < rows)
    incl = same & (cols <= rows)
    eye = jnp.where(rows == cols, 1.0, 0.0)

    units = []
    for cc in range(kk_ref.shape[0] // c):
        rs = slice(cc * c, (cc + 1) * c)
        ld = ld_ref[rs, :]
        ld_hi, ld_lo = _split_bf16(ld)
        cum = _dot(tri, ld_hi) + _dot(tri, ld_lo)
        last = cum[c - 1:, :]
        g_end = jnp.exp(last - cum)
        g_inv = jnp.exp(-cum)
        qa = kk_ref[rs, :] * jnp.exp(cum - ld)
        qr = r_ref[rs, :] * jnp.exp(cum)
        kb, kt = b_ref[rs, :] * g_inv, km_ref[rs, :] * g_inv
        bd, kd = b_ref[rs, :] * g_end, km_ref[rs, :] * g_end
        v = v_ref[rs, :]
        qr_ref[rs, :] = qr
        bd_ref[rs, :] = bd
        gc_ref[cc] = jnp.exp(last)
        for p in range(N_PAIRS):
            sl = slice(p * PAIR, (p + 1) * PAIR)
            units.append(dict(cc=cc, p=p, rs=rs, sl=sl, v=v[:, sl], kd=kd[:, sl],
                              **{k: _by_head(x[:, sl], first)
                                 for k, x in dict(qa2=qa, qr2=qr, kb2=kb, kt2=kt, v2=v).items()}))
    n = [-jnp.where(strict, _dot3(u["qa2"], u["kb2"], _NT), 0.0) for u in units]
    aak = [jnp.where(strict, _dot3(u["qa2"], u["kt2"], _NT), 0.0) for u in units]
    for u in units:
        arb_ref[u["rs"], u["sl"]] = _fold_heads(jnp.where(incl, _dot3(u["qr2"], u["kb2"], _NT), 0.0))
        ark = jnp.where(incl, _dot3(u["qr2"], u["kt2"], _NT), 0.0)
        y0_ref[u["rs"], u["sl"]] = _fold_heads(_dot3(ark, u["v2"], _NN))
        sadd_ref[u["cc"], u["p"]] = jnp.where(same, _dot3(u["v"], u["kd"], _TN), 0.0)
    t_inv = [eye + x for x in n]
    power = n
    for _ in range(c.bit_length() - 2):
        power = [_dot(x.astype(BF16), x.astype(BF16)) for x in power]
        t_inv = [t + _dot(t.astype(BF16), x.astype(BF16)) for t, x in zip(t_inv, power)]
    rhs = [_dot3(a, u["v2"], _NN) for a, u in zip(aak, units)]
    for t, r, u in zip(t_inv, rhs, units):
        w1_ref[u["rs"], u["sl"]] = _fold_heads(_dot3(t, u["qa2"], _NN))
        w2_ref[u["rs"], u["sl"]] = _fold_heads(_dot3(t, r, _NN))


def _rwkv_chunk_scan_kernel(nb_count, w1_ref, w2_ref, qr_ref, arb_ref, y0_ref, bd_ref, sadd_ref, gc_ref,
                            g_ref, bon_ref, s0_ref, lg_ref, lb_ref, seg512_ref, out_ref, st_ref, sx_scr):
    c = CHUNK
    chunk_id = pl.program_id(1)
    first = lax.broadcasted_iota(jnp.int32, (1, PAIR), 1) < HEAD_DIM
    rows = lax.broadcasted_iota(jnp.int32, (PAIR, PAIR), 0)
    cols = lax.broadcasted_iota(jnp.int32, (PAIR, PAIR), 1)
    same = (rows // HEAD_DIM) == (cols // HEAD_DIM)

    @pl.when(chunk_id == 0)
    def _():
        for nb in range(nb_count):
            for p in range(N_PAIRS):
                sx_scr[nb, p] = _by_head(s0_ref[nb, :, p * PAIR:(p + 1) * PAIR], first)

    seg512 = seg512_ref[...]
    chains = [(nb, p, slice(p * PAIR, (p + 1) * PAIR)) for nb in range(nb_count) for p in range(N_PAIRS)]
    proj = [_dot3(jnp.concatenate([w1_ref[nb, :, sl], qr_ref[nb, :, sl]], axis=0), sx_scr[nb, p], _NT)
            for nb, p, sl in chains]
    us = [proj[i][:c] + w2_ref[nb, :, sl] for i, (nb, p, sl) in enumerate(chains)]
    ys = [proj[i][c:] - _dot3(arb_ref[nb, :, sl], _by_head(us[i], first), _NN) + y0_ref[nb, :, sl]
          for i, (nb, p, sl) in enumerate(chains)]
    for i, (nb, p, sl) in enumerate(chains):
        upd = jnp.where(same, _dot3(us[i], bd_ref[nb, :, sl], _TN), 0.0)
        sx_scr[nb, p] = sx_scr[nb, p] * gc_ref[nb, :, sl] - upd + sadd_ref[nb, p]
    for nb in range(nb_count):
        y = jnp.concatenate(ys[nb * N_PAIRS:(nb + 1) * N_PAIRS], axis=1)
        mu = _dot2(y, seg512) * (1.0 / HEAD_DIM)
        yc = y - mu
        var = _dot2(yc * yc, seg512) * (1.0 / HEAD_DIM)
        yn = yc * lax.rsqrt(var + GN_EPS) * lg_ref[...] + lb_ref[...]
        out_ref[nb] = (yn + bon_ref[nb]) * g_ref[nb]

    @pl.when(chunk_id == pl.num_programs(1) - 1)
    def _():
        for nb in range(nb_count):
            st_ref[nb] = jnp.concatenate([_fold_heads(sx_scr[nb, p]) for p in range(N_PAIRS)], axis=1)


def _rwkv_chunked(pre, s0, prm, nb):
    kk, ld, b, km, v, r, g, bon = pre
    bsz, t, _ = kk.shape
    n_chunks = t // CHUNK
    idx = jnp.arange(CHUNK)
    tri = (idx[None, :] <= idx[:, None]).astype(BF16)
    cps = PREP_CHUNKS
    blk = pl.BlockSpec((None, cps * CHUNK, WIDTH), lambda i, j: (i, j, 0))
    wide = jax.ShapeDtypeStruct((bsz, t, WIDTH), F32)
    w1, w2, qr, arb, y0, bd, sadd, gc = pl.pallas_call(
        _rwkv_chunk_prep_kernel,
        grid=(bsz, n_chunks // cps),
        in_specs=[blk] * 6 + [pl.BlockSpec((CHUNK, CHUNK), lambda i, j: (0, 0))],
        out_specs=[blk] * 6 + [pl.BlockSpec((None, cps, N_PAIRS, PAIR, PAIR), lambda i, j: (i, j, 0, 0, 0)),
                               pl.BlockSpec((None, cps, 1, WIDTH), lambda i, j: (i, j, 0, 0))],
        out_shape=[wide] * 6 + [jax.ShapeDtypeStruct((bsz, n_chunks, N_PAIRS, PAIR, PAIR), F32),
                                jax.ShapeDtypeStruct((bsz, n_chunks, 1, WIDTH), F32)],
        compiler_params=_cparams("parallel", "parallel"),
        name="rwkv_chunk_prep",
    )(kk, ld, b, km, v, r, tri)
    nblk = pl.BlockSpec((nb, CHUNK, WIDTH), lambda i, j: (i, j, 0))
    sblk = pl.BlockSpec((nb, HEAD_DIM, WIDTH), lambda i, j: (i, 0, 0))
    full = lambda a: pl.BlockSpec(a.shape, lambda i, j: (0,) * a.ndim)
    params = [prm["lnx_g"], prm["lnx_b"], prm["seg512"]]
    return pl.pallas_call(
        functools.partial(_rwkv_chunk_scan_kernel, nb),
        grid=(bsz // nb, n_chunks),
        in_specs=[nblk] * 6
        + [pl.BlockSpec((nb, None, N_PAIRS, PAIR, PAIR), lambda i, j: (i, j, 0, 0, 0)),
           pl.BlockSpec((nb, None, 1, WIDTH), lambda i, j: (i, j, 0, 0)),
           nblk, nblk, sblk] + [full(a) for a in params],
        out_specs=[nblk, sblk],
        out_shape=[wide, jax.ShapeDtypeStruct((bsz, HEAD_DIM, WIDTH), F32)],
        scratch_shapes=[pltpu.VMEM((nb, N_PAIRS, PAIR, PAIR), F32)],
        compiler_params=_cparams("parallel", "arbitrary"),
        name="rwkv_chunk_scan",
    )(w1, w2, qr, arb, y0, bd, sadd, gc, g, bon, s0, *params)


def _rwkv(p, shift_prev, wkv0, prm, nb, tb):
    bsz, t, _ = p.shape
    s0 = _state_to_rows(wkv0)
    if t == 1:
        pre = _rwkv_pre(p.reshape(1, bsz, RWKV_PROJ), shift_prev.reshape(1, bsz, RWKV_PROJ), True, False, prm, bsz)
        out, st = _rwkv_scan([a.reshape(bsz, 1, WIDTH) for a in pre], s0, prm, nb, 1)
    else:
        pre = _rwkv_pre(p, shift_prev.reshape(bsz, 1, RWKV_PROJ), False, True, prm, tb)
        out, st = _rwkv_chunked(pre, s0, prm, nb)
    return out, _rows_to_state(st)


def _sb_tile(z, mask, carry, neg_ge, neg_ones):
    sp = _softplus(z)
    if mask is not None:
        sp = jnp.where(mask, sp, 0.0)
    spb = sp.astype(BF16)
    inc = _dot(spb, neg_ge)
    bk = z.shape[1]
    a = jnp.exp(z + inc + jnp.concatenate([carry] * (bk // carry.shape[1]), axis=1))
    if mask is not None:
        a = jnp.where(mask, a, 0.0)
    if neg_ones is None:
        return a.astype(BF16), carry + jnp.broadcast_to(inc[:, :1], carry.shape)
    return a.astype(BF16), carry + _dot(spb, neg_ones)


def _sb_prompt_kernel(bq, bk, bias_ref, q_ref, k_ref, v_ref, mge_ref, o_ref):
    hp = pl.program_id(1)
    qi = pl.program_id(2)
    lane = lax.broadcasted_iota(jnp.int32, (1, 2 * HEAD_DIM), 1)
    first = lane < HEAD_DIM
    q2 = q_ref[...] * jnp.asarray(SB_SCALE, BF16)
    zero = jnp.zeros_like(q2)
    q_heads = (jnp.where(first, q2, zero), jnp.where(first, zero, q2))
    biases = (bias_ref[2 * hp], bias_ref[2 * hp + 1])
    m_ge = mge_ref[...]
    per_q = bq // bk
    rows = lax.broadcasted_iota(jnp.int32, (bq, bk), 0)
    cols = lax.broadcasted_iota(jnp.int32, (bq, bk), 1)

    def tile(j, mask, state):
        acc, carries = state
        kblk = k_ref[pl.ds(pl.multiple_of(j * bk, bk), bk), :]
        vblk = v_ref[pl.ds(pl.multiple_of(j * bk, bk), bk), :]
        vzero = jnp.zeros_like(vblk)
        v_heads = (jnp.where(first, vblk, vzero), jnp.where(first, vzero, vblk))
        new_carries = []
        for e in range(2):
            z = _dot_nt(q_heads[e], kblk) + biases[e]
            a, c = _sb_tile(z, mask, carries[e], m_ge, None)
            acc = acc + _dot(a, v_heads[e])
            new_carries.append(c)
        return acc, tuple(new_carries)

    zc = jnp.zeros((bq, 2 * HEAD_DIM), F32)
    state = (zc, (zc, zc))
    for u in reversed(range(per_q)):
        state = tile(qi * per_q + u, cols + u * bk < rows, state)
    n_full = qi * per_q

    def full_tiles(i, s):
        for u in range(per_q):
            s = tile(n_full - 1 - i * per_q - u, None, s)
        return s

    state = lax.fori_loop(0, qi, full_tiles, state)
    o_ref[...] = state[0]


def _sb_prompt(q, k, v, sb_bias, bq, bk):
    bsz, t, _ = q.shape
    pair = 2 * HEAD_DIM
    idx = jnp.arange(bk)
    m_ge = -(idx[:, None] >= idx[None, :]).astype(BF16)
    return pl.pallas_call(
        functools.partial(_sb_prompt_kernel, bq, bk),
        grid=(bsz, WIDTH // pair, t // bq),
        in_specs=[pl.BlockSpec(memory_space=pltpu.SMEM),
                  pl.BlockSpec((None, bq, pair), lambda b, h, i: (b, i, h)),
                  pl.BlockSpec((None, t, pair), lambda b, h, i: (b, 0, h)),
                  pl.BlockSpec((None, t, pair), lambda b, h, i: (b, 0, h)),
                  pl.BlockSpec((bk, bk), lambda b, h, i: (0, 0))],
        out_specs=pl.BlockSpec((None, bq, pair), lambda b, h, i: (b, i, h)),
        out_shape=jax.ShapeDtypeStruct((bsz, t, WIDTH), F32),
        compiler_params=_cparams("parallel", "parallel", "arbitrary"),
        name="sb_prompt",
    )(sb_bias.astype(F32), q, k, v, m_ge)


def _sb_sample_kernel(pp, n_pages, pt_ref, q_ref, knew_ref, vnew_ref, bias_ref, mge_ref, ones_ref, *refs):
    k_refs, v_refs = refs[:pp], refs[pp:2 * pp]
    o_ref, carry_scr, acc_scr = refs[2 * pp:]
    g = pl.program_id(1)

    @pl.when(g == 0)
    def _():
        carry_scr[...] = jnp.zeros_like(carry_scr)
        acc_scr[...] = jnp.zeros_like(acc_scr)

    head_row = lax.broadcasted_iota(jnp.int32, (N_HEADS, PAGE_SIZE), 0)
    head_row_d = lax.broadcasted_iota(jnp.int32, (N_HEADS, HEAD_DIM), 0)
    q = q_ref[...]
    qb = (q * SB_SCALE).astype(BF16)
    bias = bias_ref[...]
    zs = []
    for u in range(pp):
        z = None
        for h in range(N_HEADS):
            zh = _dot(qb, k_refs[u][h].astype(BF16))
            z = zh if z is None else jnp.where(head_row == h, zh, z)
        zs.append(z + bias)
    z_all = jnp.concatenate(zs, axis=1)
    spb = _softplus(z_all).astype(BF16)
    incs, carries = [], []
    carry = carry_scr[...]
    for u in range(pp):
        page = spb[:, u * PAGE_SIZE:(u + 1) * PAGE_SIZE]
        incs.append(_dot(page, mge_ref[...]))
        carries.append(carry)
        carry = carry + _dot(page, ones_ref[...])
    carry_scr[...] = carry
    a_all = jnp.exp(z_all + jnp.concatenate(incs, axis=1) + jnp.concatenate(carries, axis=1)).astype(BF16)
    acc = acc_scr[...]
    for u in range(pp):
        a = a_all[:, u * PAGE_SIZE:(u + 1) * PAGE_SIZE]
        for h in range(N_HEADS):
            acc = acc + jnp.where(head_row_d == h, _dot_nt(a, v_refs[u][h].astype(BF16)), 0.0)
    acc_scr[...] = acc

    @pl.when(g == pl.num_programs(1) - 1)
    def _():
        past = n_pages * PAGE_SIZE
        z_new = jnp.sum(q * knew_ref[...], axis=1, keepdims=True) * SB_SCALE + bias[:, :1]
        a_new = jnp.where(past < past, jnp.exp(-_softplus(-z_new)), 0.0)
        o_ref[...] = acc + a_new * vnew_ref[...]


def _sb_sample(q, k_new, v_new, sb_bias, cache_k, cache_v, page_table, pp):
    bsz = q.shape[0]
    n_pages = page_table.shape[1]
    idx = jnp.arange(PAGE_SIZE)
    m_ge = -(idx[:, None] >= idx[None, :]).astype(BF16)
    ones_cols = -jnp.ones((PAGE_SIZE, PAGE_SIZE), BF16)
    bias = jnp.broadcast_to(sb_bias.astype(F32)[:, None], (N_HEADS, PAGE_SIZE))
    rows = lambda c: c.transpose(0, 2, 3, 1)
    row = pl.BlockSpec((None, N_HEADS, HEAD_DIM), lambda b, g, pt: (b, 0, 0))
    full = lambda a: pl.BlockSpec(a.shape, lambda b, g, pt: (0,) * a.ndim)

    def page_spec(u):
        return pl.BlockSpec((None, N_HEADS, HEAD_DIM, PAGE_SIZE),
                            lambda b, g, pt: (pt[b * n_pages + n_pages - 1 - (g * pp + u)], 0, 0, 0))

    grid_spec = pltpu.PrefetchScalarGridSpec(
        num_scalar_prefetch=1,
        grid=(bsz, n_pages // pp),
        in_specs=[row, row, row, full(bias), full(m_ge), full(ones_cols)]
        + [page_spec(u) for u in range(pp)] * 2,
        out_specs=row,
        scratch_shapes=[pltpu.VMEM((N_HEADS, PAGE_SIZE), F32), pltpu.VMEM((N_HEADS, HEAD_DIM), F32)],
    )
    return pl.pallas_call(
        functools.partial(_sb_sample_kernel, pp, n_pages),
        grid_spec=grid_spec,
        out_shape=jax.ShapeDtypeStruct((bsz, N_HEADS, HEAD_DIM), F32),
        compiler_params=_cparams("parallel", "arbitrary"),
        name="sb_sample",
    )(page_table.reshape(-1), q, k_new, v_new, bias, m_ge, ones_cols, *([rows(cache_k)] * pp), *([rows(cache_v)] * pp))


def _outproj_kernel(rw_ref, sb_ref, x_ref, g1_ref, sh2_ref, sc2_ref, wo_ref, lg_ref, lb_ref, wpq_ref,
                    x1_ref, h2_ref, qp_ref):
    mix = _dot(rw_ref[...].astype(BF16), wo_ref[:WIDTH, :]) + _dot(sb_ref[...].astype(BF16), wo_ref[WIDTH:, :])
    x1 = _ln_rows(DN_ALPHA * x_ref[...] + g1_ref[...] * mix) * lg_ref[...] + lb_ref[...]
    h2 = _ln_rows(x1) * (1.0 + sc2_ref[...]) + sh2_ref[...]
    x1_ref[...] = x1
    h2_ref[...] = h2
    qp_ref[...] = _dot(h2.astype(BF16), wpq_ref[...]).astype(BF16)


def _mod_spec(a, rows_per_mod, rb, d):
    if rows_per_mod == 1:
        return a, pl.BlockSpec((rb, d), lambda i: (i, 0))
    per = rows_per_mod // rb
    return a.reshape(-1, 1, d), pl.BlockSpec((None, 1, d), lambda i: (i // per, 0, 0))


def _outproj(rw, sb, x, g1, sh2, sc2, w_out_bf16, ln_g, ln_b, w_pq_bf16, rows_per_mod, rb):
    n, d = x.shape
    row = lambda c: pl.BlockSpec((rb, c), lambda i: (i, 0))
    full = lambda a: pl.BlockSpec(a.shape, lambda i: (0,) * a.ndim)
    mods, mod_specs = zip(*[_mod_spec(a, rows_per_mod, rb, d) for a in (g1, sh2, sc2)])
    lg, lb = ln_g.reshape(1, d), ln_b.reshape(1, d)
    return pl.pallas_call(
        _outproj_kernel,
        grid=(n // rb,),
        in_specs=[row(WIDTH), row(WIDTH), row(d), *mod_specs, full(w_out_bf16), full(lg), full(lb), full(w_pq_bf16)],
        out_specs=[row(d), row(d), row(d)],
        out_shape=[jax.ShapeDtypeStruct((n, d), F32), jax.ShapeDtypeStruct((n, d), F32),
                   jax.ShapeDtypeStruct((n, d), BF16)],
        compiler_params=_cparams("parallel"),
        name="outproj",
    )(rw, sb, x, *mods, w_out_bf16, lg, lb, w_pq_bf16)


def _take_top(x, ids, payload, count):
    vals, picked = [], []
    for _ in range(count):
        m = jnp.max(x, axis=0, keepdims=True)
        pos = jnp.min(jnp.where(x == m, ids, jnp.iinfo(jnp.int32).max), axis=0, keepdims=True)
        hit = ids == pos
        vals.append(m)
        picked.append(pos if payload is None else jnp.sum(jnp.where(hit, payload, 0), axis=0, keepdims=True))
        x = jnp.where(hit, -jnp.inf, x)
    return jnp.concatenate(vals, axis=0), jnp.concatenate(picked, axis=0)


def _pair_candidates(sv0, si0, sv1, si1):
    k = PEER_TOPK
    tokens = sv0.shape[1]
    wide = 4
    vals, ids, experts = [], [], []
    for a in range(wide):
        n = -(-(k // (a + 1)) // 8) * 8
        b_ids = lax.broadcasted_iota(jnp.int32, (n, tokens), 0)
        valid = b_ids < k // (a + 1)
        vals.append(jnp.where(valid, sv0[a:a + 1] + sv1[:n], -jnp.inf))
        ids.append(jnp.where(valid, a * k + b_ids, -1))
        experts.append(si0[a:a + 1] * N_KEYS + si1[:n])
    for b in range(k // (wide + 1)):
        last_a = k // (b + 1) - 1
        n = -(-(last_a + 1) // 8) * 8
        a_ids = lax.broadcasted_iota(jnp.int32, (n, tokens), 0)
        valid = (a_ids >= wide) & (a_ids <= last_a)
        vals.append(jnp.where(valid, sv0[:n] + sv1[b:b + 1], -jnp.inf))
        ids.append(jnp.where(valid, a_ids * k + b, -1))
        experts.append(si0[:n] * N_KEYS + si1[b:b + 1])
    return jnp.concatenate(vals, axis=0), jnp.concatenate(ids, axis=0), jnp.concatenate(experts, axis=0)


def _route_kernel(qp_ref, sk_ref, e_ref, g_ref):
    tokens = qp_ref.shape[0]
    scores = _dot_nt(sk_ref[...], qp_ref[...])
    key_ids = lax.broadcasted_iota(jnp.int32, (N_KEYS, tokens), 0)
    sv0, si0 = _take_top(scores[:N_KEYS], key_ids, None, PEER_TOPK)
    sv1, si1 = _take_top(scores[N_KEYS:], key_ids, None, PEER_TOPK)
    cand, cand_ids, cidx = _pair_candidates(sv0, si0, sv1, si1)
    top, eidx = _take_top(cand, cand_ids, cidx, PEER_TOPK)
    ex = jnp.exp(top - top[:1])
    e_ref[...] = eidx
    g_ref[...] = ex / jnp.sum(ex, axis=0, keepdims=True)


def _route(qp, sk_pairs, tb):
    n = qp.shape[0]
    blk = pl.BlockSpec((None, PEER_TOPK, tb), lambda i, h: (h, 0, i))
    return pl.pallas_call(
        _route_kernel,
        grid=(n // tb, PEER_HEADS),
        in_specs=[pl.BlockSpec((tb, 2 * HEAD_DIM), lambda i, h: (i, h)),
                  pl.BlockSpec((None, 2 * N_KEYS, 2 * HEAD_DIM), lambda i, h: (h, 0, 0))],
        out_specs=[blk, blk],
        out_shape=[jax.ShapeDtypeStruct((PEER_HEADS, PEER_TOPK, n), jnp.int32),
                   jax.ShapeDtypeStruct((PEER_HEADS, PEER_TOPK, n), F32)],
        compiler_params=_cparams("parallel", "parallel"),
        name="peer_route",
    )(qp, sk_pairs)


def _sub_key_pairs(sub_keys):
    z = jnp.zeros_like(sub_keys[:, 0])
    top = jnp.concatenate([sub_keys[:, 0], z], axis=-1)
    bot = jnp.concatenate([z, sub_keys[:, 1]], axis=-1)
    return jnp.concatenate([top, bot], axis=1).astype(BF16)


ROW_WORDS = D_MODEL // 2
ROW_PLANES = ROW_WORDS // 128
PLANE_STRIDE = PEER_SLOTS + 8


def _pack_table(w):
    bits = lax.bitcast_convert_type(w.astype(BF16), jnp.uint16).astype(jnp.uint32)
    packed = bits[:, :ROW_WORDS] | (bits[:, ROW_WORDS:] << 16)
    return lax.bitcast_convert_type(packed, jnp.int32).reshape(-1, 128)


def _gather_rows(idx_ref, first_token, tbl_ref, buf_ref, bank):
    token_idx = [idx_ref.at[pl.ds((first_token + w) * PEER_SLOTS, PEER_SLOTS)] for w in range(PEER_WIDTH)]
    for m in range(PEER_SLOTS):
        for w in range(PEER_WIDTH):
            start = pl.multiple_of(token_idx[w][m], ROW_PLANES)
            buf_ref[bank * PEER_WIDTH + w, pl.ds(m, ROW_PLANES, stride=PLANE_STRIDE), :] = (
                tbl_ref[pl.ds(start, ROW_PLANES), :])


def _plane_halves(buf_ref, j):
    words = buf_ref[pl.ds(j * PLANE_STRIDE, PEER_SLOTS), :]
    lo = lax.bitcast_convert_type(words << 16, F32)
    hi = lax.bitcast_convert_type(words & jnp.int32(-65536), F32)
    return lo, hi


PEER_WIDTH = 1
PEER_GROUP = 2 * PEER_WIDTH


def _token_groups(tb, idx_ref, tbl_ref, buf_ref, compute, init):
    _gather_rows(idx_ref, 0, tbl_ref, buf_ref, 0)

    def group(i, carry):
        t0 = PEER_GROUP * i
        for bank in range(2):
            nxt = t0 + PEER_WIDTH if bank == 0 else jnp.minimum(t0 + PEER_GROUP, tb - PEER_WIDTH)
            _gather_rows(idx_ref, nxt, tbl_ref, buf_ref, 1 - bank)
            for w in range(PEER_WIDTH):
                pos = bank * PEER_WIDTH + w
                carry = compute(t0 + pos, buf_ref.at[pos], pos, carry)
        return carry

    return lax.fori_loop(0, tb // PEER_GROUP, group, init)


def _peer_act_kernel(tb, idx_ref, h_ref, tbl_ref, act_ref, buf_ref, acc_ref):
    token_lane = lax.broadcasted_iota(jnp.int32, (PEER_SLOTS, tb), 1)
    acc_ref[...] = jnp.zeros_like(acc_ref)

    def finish(t, pos, acts):
        return jnp.where(token_lane == t, jnp.sum(acc_ref[pos], axis=1, keepdims=True), acts)

    def compute(t, buf, pos, acts):
        acts = finish(t - PEER_GROUP, pos, acts)
        hrow = h_ref[pl.ds(t, 1), :]
        acc = None
        for j in range(ROW_PLANES):
            lo, hi = _plane_halves(buf, j)
            term = (lo * hrow[:, j * 128:(j + 1) * 128]
                    + hi * hrow[:, ROW_WORDS + j * 128:ROW_WORDS + (j + 1) * 128])
            acc = term if acc is None else acc + term
        acc_ref[pos] = acc
        return acts

    acts = _token_groups(tb, idx_ref, tbl_ref, buf_ref, compute, jnp.zeros((PEER_SLOTS, tb), F32))
    for pos in range(PEER_GROUP):
        acts = finish(tb - PEER_GROUP + pos, pos, acts)
    act_ref[...] = acts


def _peer_out_kernel(tb, idx_ref, act_ref, gate_ref, tbl_ref, f_ref, buf_ref, coef_ref, col_ref):
    act = act_ref[...]
    coef_ref[...] = gate_ref[...] * (0.5 * act * (1.0 + lax.erf(act * (2.0 ** -0.5))))
    token_lane = lax.broadcasted_iota(jnp.int32, (PEER_SLOTS, tb), 1)

    def stage_coef(t, pos):
        col = jnp.sum(jnp.where(token_lane == t, coef_ref[...], 0.0), axis=1, keepdims=True)
        col_ref[pos] = jnp.broadcast_to(col, (PEER_SLOTS, 128))

    for pos in range(PEER_GROUP):
        stage_coef(pos, pos)

    def compute(t, buf, pos, carry):
        coef = col_ref[pos]
        los, his = [], []
        for j in range(ROW_PLANES):
            lo, hi = _plane_halves(buf, j)
            los.append(jnp.sum(lo * coef, axis=0, keepdims=True))
            his.append(jnp.sum(hi * coef, axis=0, keepdims=True))
        f_ref[pl.ds(t, 1), :] = jnp.concatenate(los + his, axis=1)
        stage_coef(t + PEER_GROUP, pos)
        return carry

    _token_groups(tb, idx_ref, tbl_ref, buf_ref, compute, 0)


def _peer_experts(h2, idx_flat, gate_t, tbl_u, tbl_v, tb):
    n, d = h2.shape
    smem = pl.BlockSpec((tb * PEER_SLOTS,), lambda i: (i,), memory_space=pltpu.SMEM)
    row = pl.BlockSpec((tb, d), lambda i: (i, 0))
    slot_major = pl.BlockSpec((PEER_SLOTS, tb), lambda i: (0, i))
    table = pl.BlockSpec(memory_space=pltpu.VMEM)
    buf = pltpu.VMEM((PEER_GROUP, ROW_PLANES * PLANE_STRIDE, 128), jnp.int32)
    staged = pltpu.VMEM((PEER_GROUP, PEER_SLOTS, 128), F32)
    act_t = pl.pallas_call(
        functools.partial(_peer_act_kernel, tb),
        grid=(n // tb,),
        in_specs=[smem, row, table],
        out_specs=slot_major,
        out_shape=jax.ShapeDtypeStruct((PEER_SLOTS, n), F32),
        scratch_shapes=[buf, staged],
        compiler_params=_cparams("arbitrary"),
        name="peer_act",
    )(idx_flat, h2, tbl_u)
    return pl.pallas_call(
        functools.partial(_peer_out_kernel, tb),
        grid=(n // tb,),
        in_specs=[smem, slot_major, slot_major, table],
        out_specs=row,
        out_shape=jax.ShapeDtypeStruct((n, d), F32),
        scratch_shapes=[buf, pltpu.VMEM((PEER_SLOTS, tb), F32), staged],
        compiler_params=_cparams("arbitrary"),
        name="peer_out",
    )(idx_flat, act_t, gate_t, tbl_v)


def _final_kernel(x1_ref, f_ref, g2_ref, lg_ref, lb_ref, y_ref):
    y_ref[...] = _ln_rows(DN_ALPHA * x1_ref[...] + g2_ref[...] * f_ref[...]) * lg_ref[...] + lb_ref[...]


def _final(x1, f, f_first_row, g2, ln_g, ln_b, rows_per_mod, rb):
    n, d = x1.shape
    row = pl.BlockSpec((rb, d), lambda i: (i, 0))
    first_blk = f_first_row // rb
    assert first_blk * rb == f_first_row
    f_spec = pl.BlockSpec((rb, d), lambda i: (i + first_blk, 0))
    g2, g2_spec = _mod_spec(g2, rows_per_mod, rb, d)
    vec = pl.BlockSpec((1, d), lambda i: (0, 0))
    return pl.pallas_call(
        _final_kernel,
        grid=(n // rb,),
        in_specs=[row, f_spec, g2_spec, vec, vec],
        out_specs=row,
        out_shape=jax.ShapeDtypeStruct((n, d), F32),
        compiler_params=_cparams("parallel"),
        name="final_ln",
    )(x1, f, g2, ln_g.reshape(1, d), ln_b.reshape(1, d))


ROW_BLOCK = 256
SCAN_BATCH = 4
SCAN_BLOCK = 128
SB_BLOCK = 1024
SB_KEY_BLOCK = 256
SB_PAGES = 16
ROUTE_BLOCK = 128
PEER_BLOCK = 128


def _mixer_half(x, mods, shift_prev, wkv0, attend, wts, rows_per_mod, rb):
    bsz, t, d = x.shape
    n = bsz * t
    sh1, sc1, g1, sh2, sc2, _ = mods
    xr = x.reshape(n, d)
    p, q, k, v, kb, vb = _inproj(xr, sh1, sc1, wts["w_in"], rows_per_mod, rb)
    p3 = p.reshape(bsz, t, RWKV_PROJ)
    rw, wkv_new = _rwkv(p3, shift_prev, wkv0, wts["rwkv"], SCAN_BATCH, SCAN_BLOCK)
    sb = attend(q, k, v, kb, vb)
    x1, h2, qp = _outproj(rw.reshape(n, WIDTH), sb.reshape(n, WIDTH), xr, g1, sh2, sc2, wts["w_out"],
                          wts["ln1_g"], wts["ln1_b"], wts["w_pq"], rows_per_mod, rb)
    heads = lambda a: a.reshape(bsz, t, N_HEADS, HEAD_DIM)
    return x1, h2, qp, heads(k), heads(v), wkv_new, p3[:, -1]


def _layer_pair(xp, xs, mod_p, mod_s, cache_k, cache_v, page_table, state_wkv, state_shift, wts):
    bp, tp, d = xp.shape
    bs, ts, _ = xs.shape
    assert ts == 1, "the paged attention handles one new token per sequence"
    n_p = bp * tp

    def attend_p(q, k, v, kb, vb):
        r = lambda a: a.reshape(bp, tp, WIDTH)
        return _sb_prompt(r(q), r(kb), r(vb), wts["sb_bias"], SB_BLOCK, SB_KEY_BLOCK)

    def attend_s(q, k, v, kb, vb):
        heads = lambda a: a.astype(F32).reshape(bs, N_HEADS, HEAD_DIM)
        return _sb_sample(heads(q), heads(k), heads(v), wts["sb_bias"], cache_k, cache_v, page_table, SB_PAGES)

    zero_shift = jnp.zeros((bp, RWKV_PROJ), xp.dtype)
    zero_wkv = jnp.zeros((bp, N_HEADS, HEAD_DIM, HEAD_DIM), state_wkv.dtype)
    x1p, h2p, qpp, kp, vp, wp, sp = _mixer_half(xp, mod_p, zero_shift, zero_wkv, attend_p, wts, tp, ROW_BLOCK)
    x1s, h2s, qps, ks, vs, ws, ss = _mixer_half(xs, mod_s, state_shift, state_wkv, attend_s, wts, 1, bs)

    h2 = jnp.concatenate([h2p, h2s], axis=0)
    qp = jnp.concatenate([qpp, qps], axis=0)
    n = h2.shape[0]
    eidx, gate = _route(qp, wts["sub_keys"], ROUTE_BLOCK)
    idx_flat = eidx.transpose(2, 0, 1).reshape(-1) * ROW_PLANES
    f = _peer_experts(h2, idx_flat, gate.reshape(PEER_SLOTS, n), wts["peer_u"], wts["peer_v"], PEER_BLOCK)
    yp = _final(x1p, f, 0, mod_p[5], wts["ln2_g"], wts["ln2_b"], tp, ROW_BLOCK).reshape(bp, tp, d)
    ys = _final(x1s, f, n_p, mod_s[5], wts["ln2_g"], wts["ln2_b"], 1, bs).reshape(bs, ts, d)
    return yp, ys, kp, vp, ks, vs, wp, ws, sp, ss


def kernel(x_prompt, x_sample, c_prompt, c_sample, cache_k, cache_v, page_table, state_wkv, state_shift,
           w_cond, b_cond, w_in, mu_shift, w0, w_up, a0, a_up, g_up, k_k, k_a, r_k, lnx_g, lnx_b,
           sb_bias, w_out, ln1_g, ln1_b, w_pq, sub_keys, peer_u, peer_v, ln2_g, ln2_b):
    depth = w_in.shape[0]
    bp, bs = c_prompt.shape[0], c_sample.shape[0]
    pad = (-(bp + bs)) % 8
    c_all = jnp.concatenate([c_prompt, c_sample, jnp.zeros((pad, c_prompt.shape[1]), c_prompt.dtype)], axis=0)
    yp, ys = x_prompt, x_sample
    outs = [[] for _ in range(8)]
    for l in range(depth):
        wts = dict(
            w_in=w_in[l].astype(BF16), w_out=w_out[l].astype(BF16), w_pq=w_pq[l].astype(BF16),
            rwkv=_rwkv_params(mu_shift[l], w0[l], w_up[l], a0[l], a_up[l], g_up[l], k_k[l], k_a[l], r_k[l],
                              lnx_g[l], lnx_b[l]),
            sb_bias=sb_bias[l], ln1_g=ln1_g[l], ln1_b=ln1_b[l], ln2_g=ln2_g[l], ln2_b=ln2_b[l],
            sub_keys=_sub_key_pairs(sub_keys[l]), peer_u=_pack_table(peer_u[l]), peer_v=_pack_table(peer_v[l]))
        mod = _cond(c_all, w_cond[l], b_cond[l])
        mod_p = jnp.split(mod[:bp], N_MOD, axis=-1)
        mod_s = jnp.split(mod[bp:bp + bs], N_MOD, axis=-1)
        res = _layer_pair(yp, ys, mod_p, mod_s, cache_k[l], cache_v[l], page_table, state_wkv[l], state_shift[l], wts)
        yp, ys = res[0], res[1]
        for acc, val in zip(outs, res[2:]):
            acc.append(val)
    return (yp, ys) + tuple(jnp.stack(o) for o in outs)
```

```python
import functools

import jax
import jax.numpy as jnp
from jax import lax
from jax.experimental import pallas as pl
from jax.experimental.pallas import tpu as pltpu

F32 = jnp.float32
BF16 = jnp.bfloat16

D_MODEL = 1024
HEAD_DIM = 64
N_HEADS = 8
WIDTH = N_HEADS * HEAD_DIM
W_LORA, A_LORA, G_LORA = 64, 64, 128
RWKV_PROJ = 3 * WIDTH + W_LORA + A_LORA + G_LORA
IN_COLS = RWKV_PROJ + 3 * WIDTH
GN_EPS = HEAD_DIM * 1e-5
LN_EPS = 1e-5
SB_SCALE = HEAD_DIM ** -0.5
PAGE_SIZE = 128
PEER_HEADS = 8
N_KEYS = 128
PEER_TOPK = 16
PEER_SLOTS = PEER_HEADS * PEER_TOPK
N_MOD = 6
DEPTH = 1
DN_ALPHA = (2 * DEPTH) ** 0.25

VMEM_LIMIT = 56 * 1024 * 1024


def _cparams(*sem):
    return pltpu.CompilerParams(dimension_semantics=sem, vmem_limit_bytes=VMEM_LIMIT)


def _ln_rows(x):
    mu = jnp.mean(x, axis=-1, keepdims=True)
    xc = x - mu
    var = jnp.mean(xc * xc, axis=-1, keepdims=True)
    return xc * lax.rsqrt(var + LN_EPS)


def _split_bf16(x):
    hi = x.astype(BF16)
    lo = (x - hi.astype(F32)).astype(BF16)
    return hi, lo


def _dot(a, b):
    return jnp.dot(a, b, preferred_element_type=F32)


def _dot_nt(a, b):
    return lax.dot_general(a, b, (((1,), (1,)), ((), ())), preferred_element_type=F32)


def _dot2(x, w_bf16):
    hi, lo = _split_bf16(x)
    return _dot(hi, w_bf16) + _dot(lo, w_bf16)


def _cond_kernel(c_ref, w_ref, b_ref, o_ref):
    c = c_ref[...]
    s = c * jax.nn.sigmoid(c)
    o_ref[...] = jnp.dot(s, w_ref[...], preferred_element_type=F32,
                         precision=lax.Precision.HIGHEST) + b_ref[...]


def _cond(c, w_cond, b_cond):
    n, d = c.shape
    cols = w_cond.shape[1]
    bn = 1024
    return pl.pallas_call(
        _cond_kernel,
        grid=(cols // bn,),
        in_specs=[pl.BlockSpec((n, d), lambda j: (0, 0)),
                  pl.BlockSpec((d, bn), lambda j: (0, j)),
                  pl.BlockSpec((1, bn), lambda j: (0, j))],
        out_specs=pl.BlockSpec((n, bn), lambda j: (0, j)),
        out_shape=jax.ShapeDtypeStruct((n, cols), F32),
        compiler_params=_cparams("parallel"),
        name="cond",
    )(c, w_cond, b_cond.reshape(1, cols))


def _inproj_kernel(x_ref, sh_ref, sc_ref, w_ref, p_ref, q_ref, k_ref, v_ref, kb_ref, vb_ref):
    h = _ln_rows(x_ref[...]) * (1.0 + sc_ref[...]) + sh_ref[...]
    hb = h.astype(BF16)
    p_ref[...] = _dot(hb, w_ref[:, :RWKV_PROJ])
    q_ref[...] = _dot(hb, w_ref[:, RWKV_PROJ:RWKV_PROJ + WIDTH]).astype(BF16)
    k = _dot(hb, w_ref[:, RWKV_PROJ + WIDTH:RWKV_PROJ + 2 * WIDTH])
    v = _dot(hb, w_ref[:, RWKV_PROJ + 2 * WIDTH:])
    k_ref[...] = k
    v_ref[...] = v
    kb_ref[...] = k.astype(BF16)
    vb_ref[...] = v.astype(BF16)


def _inproj(x, shift, scale, w_in_bf16, rows_per_mod, block_rows):
    n, d = x.shape
    rb = block_rows
    shift, mod_spec = _mod_spec(shift, rows_per_mod, rb, d)
    scale, _ = _mod_spec(scale, rows_per_mod, rb, d)
    row = lambda c: pl.BlockSpec((rb, c), lambda i: (i, 0))
    return pl.pallas_call(
        _inproj_kernel,
        grid=(n // rb,),
        in_specs=[row(d), mod_spec, mod_spec,
                  pl.BlockSpec((d, IN_COLS), lambda i: (0, 0))],
        out_specs=[row(RWKV_PROJ), row(WIDTH), row(WIDTH), row(WIDTH), row(WIDTH), row(WIDTH)],
        out_shape=[jax.ShapeDtypeStruct((n, RWKV_PROJ), F32),
                   jax.ShapeDtypeStruct((n, WIDTH), BF16),
                   jax.ShapeDtypeStruct((n, WIDTH), F32),
                   jax.ShapeDtypeStruct((n, WIDTH), F32),
                   jax.ShapeDtypeStruct((n, WIDTH), BF16),
                   jax.ShapeDtypeStruct((n, WIDTH), BF16)],
        compiler_params=_cparams("parallel"),
        name="inproj",
    )(x, shift, scale, w_in_bf16)


def _seg_ones(n):
    i = jnp.arange(n) // HEAD_DIM
    return (i[:, None] == i[None, :]).astype(BF16)


def _softplus(u):
    return jnp.maximum(u, 0.0) + jnp.log(1.0 + jnp.exp(-jnp.abs(u)))


def _rwkv_pre_kernel(has_prev, chunked, p_ref, prev_ref, mu_ref, w0_ref, wup_ref, a0_ref, aup_ref, gup_ref,
                     kk_ref, ka_ref, rk_ref, seg_ref, *refs):
    out_refs, carry_ref = refs[:-1], refs[-1]
    hi = lax.Precision.HIGHEST
    pf = p_ref[...]
    if has_prev:
        prev = prev_ref[...]
    else:
        tb = pl.program_id(1)
        first = jnp.where(tb == 0, prev_ref[...], carry_ref[...])
        rows = lax.broadcasted_iota(jnp.int32, pf.shape, 0)
        prev = jnp.where(rows == 0, first, pltpu.roll(pf, 1, axis=0))
        carry_ref[...] = pf[pf.shape[0] - 1:, :]
    pm = pf + (prev - pf) * mu_ref[...]
    r = pm[:, :WIDTH]
    k = pm[:, WIDTH:2 * WIDTH]
    v = pm[:, 2 * WIDTH:3 * WIDTH]
    dwa = pm[:, 3 * WIDTH:3 * WIDTH + W_LORA + A_LORA]
    dg = pm[:, 3 * WIDTH + W_LORA + A_LORA:]
    seg = seg_ref[...]
    w = -_softplus(-(w0_ref[...] + jnp.dot(jnp.tanh(dwa), wup_ref[...], precision=hi,
                                           preferred_element_type=F32))) - 0.5
    log_dec = -jnp.exp(w)
    a = jax.nn.sigmoid(a0_ref[...] + jnp.dot(dwa, aup_ref[...], precision=hi, preferred_element_type=F32))
    g = jnp.dot(jax.nn.sigmoid(dg), gup_ref[...], precision=hi, preferred_element_type=F32)
    kkr = k * kk_ref[...]
    kk = kkr * lax.rsqrt(jnp.maximum(_dot2(kkr * kkr, seg), 1e-24))
    km = k * (1.0 + (a - 1.0) * ka_ref[...])
    b = kk * a
    bonus = _dot2(r * km * rk_ref[...], seg) * v
    if chunked:
        outs = (kk, log_dec, b, km, v, r, g, bonus)
    else:
        dec = jnp.exp(log_dec)
        outs = (kk, dec, b, km, v, dec * r, _dot2(b * r, seg), _dot2(km * r, seg), g, bonus)
    for ref, val in zip(out_refs, outs):
        ref[...] = val


def _rwkv_pre(p, prev, has_prev, chunked, prm, tb):
    bsz, t, _ = p.shape
    n_out = 8 if chunked else 10
    blk = lambda c: pl.BlockSpec((None, tb, c), lambda i, j: (i, j, 0))
    full = lambda a: pl.BlockSpec(a.shape, lambda i, j: (0,) * a.ndim)
    prev_spec = blk(RWKV_PROJ) if has_prev else pl.BlockSpec((None, 1, RWKV_PROJ), lambda i, j: (i, 0, 0))
    params = [prm["mu"], prm["w0"], prm["wup"], prm["a0"], prm["aup"], prm["gup"],
              prm["k_k"], prm["k_a"], prm["r_k"], prm["seg512"]]
    return pl.pallas_call(
        functools.partial(_rwkv_pre_kernel, has_prev, chunked),
        grid=(bsz, t // tb),
        in_specs=[blk(RWKV_PROJ), prev_spec] + [full(a) for a in params],
        out_specs=[blk(WIDTH)] * n_out,
        out_shape=[jax.ShapeDtypeStruct((bsz, t, WIDTH), F32)] * n_out,
        scratch_shapes=[pltpu.VMEM((1, RWKV_PROJ), F32)],
        compiler_params=_cparams("parallel", "arbitrary"),
        name="rwkv_pre",
    )(p, prev, *params)


def _rwkv_scan_kernel(nb_count, tb, kk_ref, dec_ref, b_ref, km_ref, v_ref, wr_ref, br_ref, kr_ref,
                      g_ref, bon_ref, s0_ref, lg_ref, lb_ref, seg256_ref, seg512_ref,
                      out_ref, st_ref, s_scr, y_scr):
    step_blk = pl.program_id(1)

    @pl.when(step_blk == 0)
    def _():
        s_scr[...] = s0_ref[...]

    shape = (HEAD_DIM, WIDTH)
    ident = (lax.broadcasted_iota(jnp.int32, shape, 1) & (HEAD_DIM - 1)) == lax.broadcasted_iota(jnp.int32, shape, 0)
    seg = seg256_ref[...]
    half = WIDTH // 2

    def segsum(lhs):
        return jnp.concatenate([_dot(lhs[:, :half], seg), _dot(lhs[:, half:], seg)], axis=1)

    def step(t, carry):
        for nb in range(nb_count):
            row = lambda ref: ref[nb, pl.ds(t, 1), :]
            s = s_scr[nb]
            p_hi, p_lo = _split_bf16(s * row(kk_ref))
            dv = jnp.where(ident, row(v_ref), 0.0).astype(BF16)
            pr = (s * row(wr_ref)).astype(BF16)
            res = segsum(jnp.concatenate([p_hi, p_lo, dv, pr], axis=0))
            sa = res[:HEAD_DIM] + res[HEAD_DIM:2 * HEAD_DIM]
            vcol = res[2 * HEAD_DIM:3 * HEAD_DIM]
            ycol = res[3 * HEAD_DIM:] - sa * row(br_ref)
            s_scr[nb] = s * row(dec_ref) - sa * row(b_ref) + vcol * row(km_ref)
            y_scr[nb, pl.ds(t, 1), :] = (jnp.sum(jnp.where(ident, ycol, 0.0), axis=0, keepdims=True)
                                         + row(v_ref) * row(kr_ref))
        return carry

    lax.fori_loop(0, tb, step, 0)

    seg512 = seg512_ref[...]
    for nb in range(nb_count):
        y = y_scr[nb]
        mu = _dot2(y, seg512) * (1.0 / HEAD_DIM)
        yc = y - mu
        var = _dot2(yc * yc, seg512) * (1.0 / HEAD_DIM)
        yn = yc * lax.rsqrt(var + GN_EPS) * lg_ref[...] + lb_ref[...]
        out_ref[nb] = (yn + bon_ref[nb]) * g_ref[nb]

    @pl.when(step_blk == pl.num_programs(1) - 1)
    def _():
        st_ref[...] = s_scr[...]


def _rwkv_scan(pre, s0, prm, nb, tb):
    kk, dec, b, km, v, wr, br, kr, g, bon = pre
    bsz, t, _ = kk.shape
    blk = pl.BlockSpec((nb, tb, WIDTH), lambda i, j: (i, j, 0))
    sblk = pl.BlockSpec((nb, HEAD_DIM, WIDTH), lambda i, j: (i, 0, 0))
    full = lambda a: pl.BlockSpec(a.shape, lambda i, j: (0,) * a.ndim)
    params = [prm["lnx_g"], prm["lnx_b"], prm["seg256"], prm["seg512"]]
    return pl.pallas_call(
        functools.partial(_rwkv_scan_kernel, nb, tb),
        grid=(bsz // nb, t // tb),
        in_specs=[blk] * 10 + [sblk] + [full(a) for a in params],
        out_specs=[blk, sblk],
        out_shape=[jax.ShapeDtypeStruct((bsz, t, WIDTH), F32),
                   jax.ShapeDtypeStruct((bsz, HEAD_DIM, WIDTH), F32)],
        scratch_shapes=[pltpu.VMEM((nb, HEAD_DIM, WIDTH), F32), pltpu.VMEM((nb, tb, WIDTH), F32)],
        compiler_params=_cparams("parallel", "arbitrary"),
        name="rwkv_scan",
    )(kk, dec, b, km, v, wr, br, kr, g, bon, s0, *params)


def _rwkv_params(mu_shift, w0, w_up, a0, a_up, g_up, k_k, k_a, r_k, lnx_g, lnx_b):
    row = lambda a: a.reshape(1, -1).astype(F32)
    zeros = jnp.zeros((A_LORA, WIDTH), F32)
    return dict(mu=row(mu_shift), w0=row(w0), a0=row(a0), k_k=row(k_k), k_a=row(k_a), r_k=row(r_k),
                lnx_g=row(lnx_g), lnx_b=row(lnx_b), gup=g_up,
                wup=jnp.concatenate([w_up, zeros], axis=0), aup=jnp.concatenate([zeros, a_up], axis=0),
                seg256=_seg_ones(WIDTH // 2), seg512=_seg_ones(WIDTH))


def _state_to_rows(wkv):
    bsz = wkv.shape[0]
    return wkv.transpose(0, 2, 1, 3).reshape(bsz, HEAD_DIM, WIDTH)


def _rows_to_state(s):
    bsz = s.shape[0]
    return s.reshape(bsz, HEAD_DIM, N_HEADS, HEAD_DIM).transpose(0, 2, 1, 3)


CHUNK = 64
PREP_CHUNKS = 4
PAIR = 2 * HEAD_DIM
N_PAIRS = WIDTH // PAIR
_NN = ((1,), (0,))
_NT = ((1,), (1,))
_TN = ((0,), (0,))


def _dot3(a, b, dims):
    ah, al = _split_bf16(a)
    bh, bl = _split_bf16(b)
    dg = lambda x, y: lax.dot_general(x, y, (dims, ((), ())), preferred_element_type=F32)
    return dg(ah, bh) + dg(ah, bl) + dg(al, bh)


def _dot1(a, b, dims):
    return lax.dot_general(a.astype(BF16), b.astype(BF16), (dims, ((), ())), preferred_element_type=F32)


def _by_head(x, first):
    zero = jnp.zeros_like(x)
    return jnp.concatenate([jnp.where(first, x, zero), jnp.where(first, zero, x)], axis=0)


def _fold_heads(x):
    c = x.shape[0] // 2
    return x[:c] + x[c:]


def _rwkv_chunk_prep_kernel(kk_ref, ld_ref, b_ref, km_ref, v_ref, r_ref, tri_ref,
                            w1_ref, w2_ref, qr_ref, arb_ref, y0_ref, bd_ref, sadd_ref, gc_ref):
    c = CHUNK
    tri = tri_ref[...]
    first = lax.broadcasted_iota(jnp.int32, (1, PAIR), 1) < HEAD_DIM
    rows = lax.broadcasted_iota(jnp.int32, (PAIR, PAIR), 0)
    cols = lax.broadcasted_iota(jnp.int32, (PAIR, PAIR), 1)
    same = (rows // c) == (cols // c)
    strict = same & (cols < rows)
    incl = same & (cols <= rows)
    eye = jnp.where(rows == cols, 1.0, 0.0)

    units = []
    for cc in range(kk_ref.shape[0] // c):
        rs = slice(cc * c, (cc + 1) * c)
        ld = ld_ref[rs, :]
        ld_hi, ld_lo = _split_bf16(ld)
        cum = _dot(tri, ld_hi) + _dot(tri, ld_lo)
        last = cum[c - 1:, :]
        g_end = jnp.exp(last - cum)
        g_inv = jnp.exp(-cum)
        qa = kk_ref[rs, :] * jnp.exp(cum - ld)
        qr = r_ref[rs, :] * jnp.exp(cum)
        kb, kt = b_ref[rs, :] * g_inv, km_ref[rs, :] * g_inv
        bd, kd = b_ref[rs, :] * g_end, km_ref[rs, :] * g_end
        v = v_ref[rs, :]
        qr_ref[rs, :] = qr
        bd_ref[rs, :] = bd
        gc_ref[cc] = jnp.exp(last)
        for p in range(N_PAIRS):
            sl = slice(p * PAIR, (p + 1) * PAIR)
            units.append(dict(cc=cc, p=p, rs=rs, sl=sl, v=v[:, sl], kd=kd[:, sl],
                              **{k: _by_head(x[:, sl], first)
                                 for k, x in dict(qa2=qa, qr2=qr, kb2=kb, kt2=kt, v2=v).items()}))
    n = [-jnp.where(strict, _dot3(u["qa2"], u["kb2"], _NT), 0.0) for u in units]
    aak = [jnp.where(strict, _dot3(u["qa2"], u["kt2"], _NT), 0.0) for u in units]
    for u in units:
        arb_ref[u["rs"], u["sl"]] = _fold_heads(jnp.where(incl, _dot1(u["qr2"], u["kb2"], _NT), 0.0))
        ark = jnp.where(incl, _dot1(u["qr2"], u["kt2"], _NT), 0.0)
        y0_ref[u["rs"], u["sl"]] = _fold_heads(_dot1(ark, u["v2"], _NN))
        sadd_ref[u["cc"], u["p"]] = jnp.where(same, _dot3(u["v"], u["kd"], _TN), 0.0)
    t_inv = [eye + x for x in n]
    power = n
    for _ in range(c.bit_length() - 2):
        power = [_dot(x.astype(BF16), x.astype(BF16)) for x in power]
        t_inv = [t + _dot(t.astype(BF16), x.astype(BF16)) for t, x in zip(t_inv, power)]
    rhs = [_dot3(a, u["v2"], _NN) for a, u in zip(aak, units)]
    for t, r, u in zip(t_inv, rhs, units):
        w1_ref[u["rs"], u["sl"]] = _fold_heads(_dot3(t, u["qa2"], _NN))
        w2_ref[u["rs"], u["sl"]] = _fold_heads(_dot3(t, r, _NN))


def _rwkv_chunk_scan_kernel(nb_count, w1_ref, w2_ref, qr_ref, arb_ref, y0_ref, bd_ref, sadd_ref, gc_ref,
                            g_ref, bon_ref, s0_ref, lg_ref, lb_ref, seg512_ref, out_ref, st_ref, sx_scr):
    c = CHUNK
    chunk_id = pl.program_id(1)
    first = lax.broadcasted_iota(jnp.int32, (1, PAIR), 1) < HEAD_DIM
    rows = lax.broadcasted_iota(jnp.int32, (PAIR, PAIR), 0)
    cols = lax.broadcasted_iota(jnp.int32, (PAIR, PAIR), 1)
    same = (rows // HEAD_DIM) == (cols // HEAD_DIM)

    @pl.when(chunk_id == 0)
    def _():
        for nb in range(nb_count):
            for p in range(N_PAIRS):
                sx_scr[nb, p] = _by_head(s0_ref[nb, :, p * PAIR:(p + 1) * PAIR], first)

    seg512 = seg512_ref[...]
    chains = [(nb, p, slice(p * PAIR, (p + 1) * PAIR)) for nb in range(nb_count) for p in range(N_PAIRS)]
    proj = [_dot3(jnp.concatenate([w1_ref[nb, :, sl], qr_ref[nb, :, sl]], axis=0), sx_scr[nb, p], _NT)
            for nb, p, sl in chains]
    us = [proj[i][:c] + w2_ref[nb, :, sl] for i, (nb, p, sl) in enumerate(chains)]
    ys = [proj[i][c:] - _dot1(arb_ref[nb, :, sl], _by_head(us[i], first), _NN) + y0_ref[nb, :, sl]
          for i, (nb, p, sl) in enumerate(chains)]
    for i, (nb, p, sl) in enumerate(chains):
        upd = jnp.where(same, _dot3(us[i], bd_ref[nb, :, sl], _TN), 0.0)
        sx_scr[nb, p] = sx_scr[nb, p] * gc_ref[nb, :, sl] - upd + sadd_ref[nb, p]
    for nb in range(nb_count):
        y = jnp.concatenate(ys[nb * N_PAIRS:(nb + 1) * N_PAIRS], axis=1)
        mu = _dot2(y, seg512) * (1.0 / HEAD_DIM)
        yc = y - mu
        var = _dot2(yc * yc, seg512) * (1.0 / HEAD_DIM)
        yn = yc * lax.rsqrt(var + GN_EPS) * lg_ref[...] + lb_ref[...]
        out_ref[nb] = (yn + bon_ref[nb]) * g_ref[nb]

    @pl.when(chunk_id == pl.num_programs(1) - 1)
    def _():
        for nb in range(nb_count):
            st_ref[nb] = jnp.concatenate([_fold_heads(sx_scr[nb, p]) for p in range(N_PAIRS)], axis=1)


def _rwkv_chunked(pre, s0, prm, nb):
    kk, ld, b, km, v, r, g, bon = pre
    bsz, t, _ = kk.shape
    n_chunks = t // CHUNK
    idx = jnp.arange(CHUNK)
    tri = (idx[None, :] <= idx[:, None]).astype(BF16)
    cps = PREP_CHUNKS
    blk = pl.BlockSpec((None, cps * CHUNK, WIDTH), lambda i, j: (i, j, 0))
    wide = jax.ShapeDtypeStruct((bsz, t, WIDTH), F32)
    w1, w2, qr, arb, y0, bd, sadd, gc = pl.pallas_call(
        _rwkv_chunk_prep_kernel,
        grid=(bsz, n_chunks // cps),
        in_specs=[blk] * 6 + [pl.BlockSpec((CHUNK, CHUNK), lambda i, j: (0, 0))],
        out_specs=[blk] * 6 + [pl.BlockSpec((None, cps, N_PAIRS, PAIR, PAIR), lambda i, j: (i, j, 0, 0, 0)),
                               pl.BlockSpec((None, cps, 1, WIDTH), lambda i, j: (i, j, 0, 0))],
        out_shape=[wide] * 6 + [jax.ShapeDtypeStruct((bsz, n_chunks, N_PAIRS, PAIR, PAIR), F32),
                                jax.ShapeDtypeStruct((bsz, n_chunks, 1, WIDTH), F32)],
        compiler_params=_cparams("parallel", "parallel"),
        name="rwkv_chunk_prep",
    )(kk, ld, b, km, v, r, tri)
    nblk = pl.BlockSpec((nb, CHUNK, WIDTH), lambda i, j: (i, j, 0))
    sblk = pl.BlockSpec((nb, HEAD_DIM, WIDTH), lambda i, j: (i, 0, 0))
    full = lambda a: pl.BlockSpec(a.shape, lambda i, j: (0,) * a.ndim)
    params = [prm["lnx_g"], prm["lnx_b"], prm["seg512"]]
    return pl.pallas_call(
        functools.partial(_rwkv_chunk_scan_kernel, nb),
        grid=(bsz // nb, n_chunks),
        in_specs=[nblk] * 6
        + [pl.BlockSpec((nb, None, N_PAIRS, PAIR, PAIR), lambda i, j: (i, j, 0, 0, 0)),
           pl.BlockSpec((nb, None, 1, WIDTH), lambda i, j: (i, j, 0, 0)),
           nblk, nblk, sblk] + [full(a) for a in params],
        out_specs=[nblk, sblk],
        out_shape=[wide, jax.ShapeDtypeStruct((bsz, HEAD_DIM, WIDTH), F32)],
        scratch_shapes=[pltpu.VMEM((nb, N_PAIRS, PAIR, PAIR), F32)],
        compiler_params=_cparams("parallel", "arbitrary"),
        name="rwkv_chunk_scan",
    )(w1, w2, qr, arb, y0, bd, sadd, gc, g, bon, s0, *params)


def _rwkv(p, shift_prev, wkv0, prm, nb, tb):
    bsz, t, _ = p.shape
    s0 = _state_to_rows(wkv0)
    if t == 1:
        pre = _rwkv_pre(p.reshape(1, bsz, RWKV_PROJ), shift_prev.reshape(1, bsz, RWKV_PROJ), True, False, prm, bsz)
        out, st = _rwkv_scan([a.reshape(bsz, 1, WIDTH) for a in pre], s0, prm, nb, 1)
    else:
        pre = _rwkv_pre(p, shift_prev.reshape(bsz, 1, RWKV_PROJ), False, True, prm, tb)
        out, st = _rwkv_chunked(pre, s0, prm, nb)
    return out, _rows_to_state(st)


def _sb_tile(z, mask, carry, neg_ge, neg_ones):
    sp = _softplus(z)
    if mask is not None:
        sp = jnp.where(mask, sp, 0.0)
    spb = sp.astype(BF16)
    inc = _dot(spb, neg_ge)
    bk = z.shape[1]
    a = jnp.exp(z + inc + jnp.concatenate([carry] * (bk // carry.shape[1]), axis=1))
    if mask is not None:
        a = jnp.where(mask, a, 0.0)
    if neg_ones is None:
        return a.astype(BF16), carry + jnp.broadcast_to(inc[:, :1], carry.shape)
    return a.astype(BF16), carry + _dot(spb, neg_ones)


def _sb_prompt_kernel(bq, bk, bias_ref, q_ref, k_ref, v_ref, mge_ref, o_ref):
    hp = pl.program_id(1)
    qi = pl.program_id(2)
    lane = lax.broadcasted_iota(jnp.int32, (1, 2 * HEAD_DIM), 1)
    first = lane < HEAD_DIM
    q2 = q_ref[...] * jnp.asarray(SB_SCALE, BF16)
    zero = jnp.zeros_like(q2)
    q_heads = (jnp.where(first, q2, zero), jnp.where(first, zero, q2))
    biases = (bias_ref[2 * hp], bias_ref[2 * hp + 1])
    m_ge = mge_ref[...]
    per_q = bq // bk
    rows = lax.broadcasted_iota(jnp.int32, (bq, bk), 0)
    cols = lax.broadcasted_iota(jnp.int32, (bq, bk), 1)

    def tile(j, mask, state):
        acc, carries = state
        kblk = k_ref[pl.ds(pl.multiple_of(j * bk, bk), bk), :]
        vblk = v_ref[pl.ds(pl.multiple_of(j * bk, bk), bk), :]
        vzero = jnp.zeros_like(vblk)
        v_heads = (jnp.where(first, vblk, vzero), jnp.where(first, vzero, vblk))
        new_carries = []
        for e in range(2):
            z = _dot_nt(q_heads[e], kblk) + biases[e]
            a, c = _sb_tile(z, mask, carries[e], m_ge, None)
            acc = acc + _dot(a, v_heads[e])
            new_carries.append(c)
        return acc, tuple(new_carries)

    zc = jnp.zeros((bq, 2 * HEAD_DIM), F32)
    state = (zc, (zc, zc))
    for u in reversed(range(per_q)):
        state = tile(qi * per_q + u, cols + u * bk < rows, state)
    n_full = qi * per_q

    def full_tiles(i, s):
        for u in range(per_q):
            s = tile(n_full - 1 - i * per_q - u, None, s)
        return s

    state = lax.fori_loop(0, qi, full_tiles, state)
    o_ref[...] = state[0]


def _sb_prompt(q, k, v, sb_bias, bq, bk):
    bsz, t, _ = q.shape
    pair = 2 * HEAD_DIM
    idx = jnp.arange(bk)
    m_ge = -(idx[:, None] >= idx[None, :]).astype(BF16)
    return pl.pallas_call(
        functools.partial(_sb_prompt_kernel, bq, bk),
        grid=(bsz, WIDTH // pair, t // bq),
        in_specs=[pl.BlockSpec(memory_space=pltpu.SMEM),
                  pl.BlockSpec((None, bq, pair), lambda b, h, i: (b, i, h)),
                  pl.BlockSpec((None, t, pair), lambda b, h, i: (b, 0, h)),
                  pl.BlockSpec((None, t, pair), lambda b, h, i: (b, 0, h)),
                  pl.BlockSpec((bk, bk), lambda b, h, i: (0, 0))],
        out_specs=pl.BlockSpec((None, bq, pair), lambda b, h, i: (b, i, h)),
        out_shape=jax.ShapeDtypeStruct((bsz, t, WIDTH), F32),
        compiler_params=_cparams("parallel", "parallel", "arbitrary"),
        name="sb_prompt",
    )(sb_bias.astype(F32), q, k, v, m_ge)


def _sb_sample_kernel(pp, n_pages, pt_ref, q_ref, knew_ref, vnew_ref, bias_ref, mge_ref, ones_ref, *refs):
    k_refs, v_refs = refs[:pp], refs[pp:2 * pp]
    o_ref, carry_scr, acc_scr = refs[2 * pp:]
    g = pl.program_id(1)

    @pl.when(g == 0)
    def _():
        carry_scr[...] = jnp.zeros_like(carry_scr)
        acc_scr[...] = jnp.zeros_like(acc_scr)

    head_row = lax.broadcasted_iota(jnp.int32, (N_HEADS, PAGE_SIZE), 0)
    head_row_d = lax.broadcasted_iota(jnp.int32, (N_HEADS, HEAD_DIM), 0)
    q = q_ref[...]
    qb = (q * SB_SCALE).astype(BF16)
    bias = bias_ref[...]
    zs = []
    for u in range(pp):
        z = None
        for h in range(N_HEADS):
            zh = _dot(qb, k_refs[u][h].astype(BF16))
            z = zh if z is None else jnp.where(head_row == h, zh, z)
        zs.append(z + bias)
    z_all = jnp.concatenate(zs, axis=1)
    spb = _softplus(z_all).astype(BF16)
    incs, carries = [], []
    carry = carry_scr[...]
    for u in range(pp):
        page = spb[:, u * PAGE_SIZE:(u + 1) * PAGE_SIZE]
        incs.append(_dot(page, mge_ref[...]))
        carries.append(carry)
        carry = carry + _dot(page, ones_ref[...])
    carry_scr[...] = carry
    a_all = jnp.exp(z_all + jnp.concatenate(incs, axis=1) + jnp.concatenate(carries, axis=1)).astype(BF16)
    acc = acc_scr[...]
    for u in range(pp):
        a = a_all[:, u * PAGE_SIZE:(u + 1) * PAGE_SIZE]
        for h in range(N_HEADS):
            acc = acc + jnp.where(head_row_d == h, _dot_nt(a, v_refs[u][h].astype(BF16)), 0.0)
    acc_scr[...] = acc

    @pl.when(g == pl.num_programs(1) - 1)
    def _():
        past = n_pages * PAGE_SIZE
        z_new = jnp.sum(q * knew_ref[...], axis=1, keepdims=True) * SB_SCALE + bias[:, :1]
        a_new = jnp.where(past < past, jnp.exp(-_softplus(-z_new)), 0.0)
        o_ref[...] = acc + a_new * vnew_ref[...]


def _sb_sample(q, k_new, v_new, sb_bias, cache_k, cache_v, page_table, pp):
    bsz = q.shape[0]
    n_pages = page_table.shape[1]
    idx = jnp.arange(PAGE_SIZE)
    m_ge = -(idx[:, None] >= idx[None, :]).astype(BF16)
    ones_cols = -jnp.ones((PAGE_SIZE, PAGE_SIZE), BF16)
    bias = jnp.broadcast_to(sb_bias.astype(F32)[:, None], (N_HEADS, PAGE_SIZE))
    rows = lambda c: c.transpose(0, 2, 3, 1)
    row = pl.BlockSpec((None, N_HEADS, HEAD_DIM), lambda b, g, pt: (b, 0, 0))
    full = lambda a: pl.BlockSpec(a.shape, lambda b, g, pt: (0,) * a.ndim)

    def page_spec(u):
        return pl.BlockSpec((None, N_HEADS, HEAD_DIM, PAGE_SIZE),
                            lambda b, g, pt: (pt[b * n_pages + n_pages - 1 - (g * pp + u)], 0, 0, 0))

    grid_spec = pltpu.PrefetchScalarGridSpec(
        num_scalar_prefetch=1,
        grid=(bsz, n_pages // pp),
        in_specs=[row, row, row, full(bias), full(m_ge), full(ones_cols)]
        + [page_spec(u) for u in range(pp)] * 2,
        out_specs=row,
        scratch_shapes=[pltpu.VMEM((N_HEADS, PAGE_SIZE), F32), pltpu.VMEM((N_HEADS, HEAD_DIM), F32)],
    )
    return pl.pallas_call(
        functools.partial(_sb_sample_kernel, pp, n_pages),
        grid_spec=grid_spec,
        out_shape=jax.ShapeDtypeStruct((bsz, N_HEADS, HEAD_DIM), F32),
        compiler_params=_cparams("parallel", "arbitrary"),
        name="sb_sample",
    )(page_table.reshape(-1), q, k_new, v_new, bias, m_ge, ones_cols, *([rows(cache_k)] * pp), *([rows(cache_v)] * pp))


def _outproj_kernel(rw_ref, sb_ref, x_ref, g1_ref, sh2_ref, sc2_ref, wo_ref, lg_ref, lb_ref, wpq_ref,
                    x1_ref, h2_ref, qp_ref):
    mix = _dot(rw_ref[...].astype(BF16), wo_ref[:WIDTH, :]) + _dot(sb_ref[...].astype(BF16), wo_ref[WIDTH:, :])
    x1 = _ln_rows(DN_ALPHA * x_ref[...] + g1_ref[...] * mix) * lg_ref[...] + lb_ref[...]
    h2 = _ln_rows(x1) * (1.0 + sc2_ref[...]) + sh2_ref[...]
    x1_ref[...] = x1
    h2_ref[...] = h2
    qp_ref[...] = _dot(h2.astype(BF16), wpq_ref[...]).astype(BF16)


def _mod_spec(a, rows_per_mod, rb, d):
    if rows_per_mod == 1:
        return a, pl.BlockSpec((rb, d), lambda i: (i, 0))
    per = rows_per_mod // rb
    return a.reshape(-1, 1, d), pl.BlockSpec((None, 1, d), lambda i: (i // per, 0, 0))


def _outproj(rw, sb, x, g1, sh2, sc2, w_out_bf16, ln_g, ln_b, w_pq_bf16, rows_per_mod, rb):
    n, d = x.shape
    row = lambda c: pl.BlockSpec((rb, c), lambda i: (i, 0))
    full = lambda a: pl.BlockSpec(a.shape, lambda i: (0,) * a.ndim)
    mods, mod_specs = zip(*[_mod_spec(a, rows_per_mod, rb, d) for a in (g1, sh2, sc2)])
    lg, lb = ln_g.reshape(1, d), ln_b.reshape(1, d)
    return pl.pallas_call(
        _outproj_kernel,
        grid=(n // rb,),
        in_specs=[row(WIDTH), row(WIDTH), row(d), *mod_specs, full(w_out_bf16), full(lg), full(lb), full(w_pq_bf16)],
        out_specs=[row(d), row(d), row(d)],
        out_shape=[jax.ShapeDtypeStruct((n, d), F32), jax.ShapeDtypeStruct((n, d), F32),
                   jax.ShapeDtypeStruct((n, d), BF16)],
        compiler_params=_cparams("parallel"),
        name="outproj",
    )(rw, sb, x, *mods, w_out_bf16, lg, lb, w_pq_bf16)


def _take_top(x, ids, payload, count):
    vals, picked = [], []
    for _ in range(count):
        m = jnp.max(x, axis=0, keepdims=True)
        pos = jnp.min(jnp.where(x == m, ids, jnp.iinfo(jnp.int32).max), axis=0, keepdims=True)
        hit = ids == pos
        vals.append(m)
        picked.append(pos if payload is None else jnp.sum(jnp.where(hit, payload, 0), axis=0, keepdims=True))
        x = jnp.where(hit, -jnp.inf, x)
    return jnp.concatenate(vals, axis=0), jnp.concatenate(picked, axis=0)


def _pair_candidates(sv0, si0, sv1, si1):
    k = PEER_TOPK
    tokens = sv0.shape[1]
    wide = 4
    vals, ids, experts = [], [], []
    for a in range(wide):
        n = -(-(k // (a + 1)) // 8) * 8
        b_ids = lax.broadcasted_iota(jnp.int32, (n, tokens), 0)
        valid = b_ids < k // (a + 1)
        vals.append(jnp.where(valid, sv0[a:a + 1] + sv1[:n], -jnp.inf))
        ids.append(jnp.where(valid, a * k + b_ids, -1))
        experts.append(si0[a:a + 1] * N_KEYS + si1[:n])
    for b in range(k // (wide + 1)):
        last_a = k // (b + 1) - 1
        n = -(-(last_a + 1) // 8) * 8
        a_ids = lax.broadcasted_iota(jnp.int32, (n, tokens), 0)
        valid = (a_ids >= wide) & (a_ids <= last_a)
        vals.append(jnp.where(valid, sv0[:n] + sv1[b:b + 1], -jnp.inf))
        ids.append(jnp.where(valid, a_ids * k + b, -1))
        experts.append(si0[:n] * N_KEYS + si1[b:b + 1])
    return jnp.concatenate(vals, axis=0), jnp.concatenate(ids, axis=0), jnp.concatenate(experts, axis=0)


def _route_kernel(qp_ref, sk_ref, e_ref, g_ref):
    tokens = qp_ref.shape[0]
    scores = _dot_nt(sk_ref[...], qp_ref[...])
    key_ids = lax.broadcasted_iota(jnp.int32, (N_KEYS, tokens), 0)
    sv0, si0 = _take_top(scores[:N_KEYS], key_ids, None, PEER_TOPK)
    sv1, si1 = _take_top(scores[N_KEYS:], key_ids, None, PEER_TOPK)
    cand, cand_ids, cidx = _pair_candidates(sv0, si0, sv1, si1)
    top, eidx = _take_top(cand, cand_ids, cidx, PEER_TOPK)
    ex = jnp.exp(top - top[:1])
    e_ref[...] = eidx
    g_ref[...] = ex / jnp.sum(ex, axis=0, keepdims=True)


def _route(qp, sk_pairs, tb):
    n = qp.shape[0]
    blk = pl.BlockSpec((None, PEER_TOPK, tb), lambda i, h: (h, 0, i))
    return pl.pallas_call(
        _route_kernel,
        grid=(n // tb, PEER_HEADS),
        in_specs=[pl.BlockSpec((tb, 2 * HEAD_DIM), lambda i, h: (i, h)),
                  pl.BlockSpec((None, 2 * N_KEYS, 2 * HEAD_DIM), lambda i, h: (h, 0, 0))],
        out_specs=[blk, blk],
        out_shape=[jax.ShapeDtypeStruct((PEER_HEADS, PEER_TOPK, n), jnp.int32),
                   jax.ShapeDtypeStruct((PEER_HEADS, PEER_TOPK, n), F32)],
        compiler_params=_cparams("parallel", "parallel"),
        name="peer_route",
    )(qp, sk_pairs)


def _sub_key_pairs(sub_keys):
    z = jnp.zeros_like(sub_keys[:, 0])
    top = jnp.concatenate([sub_keys[:, 0], z], axis=-1)
    bot = jnp.concatenate([z, sub_keys[:, 1]], axis=-1)
    return jnp.concatenate([top, bot], axis=1).astype(BF16)


ROW_WORDS = D_MODEL // 2
ROW_PLANES = ROW_WORDS // 128
PLANE_STRIDE = PEER_SLOTS + 8


def _pack_table(w):
    bits = lax.bitcast_convert_type(w.astype(BF16), jnp.uint16).astype(jnp.uint32)
    packed = bits[:, :ROW_WORDS] | (bits[:, ROW_WORDS:] << 16)
    return lax.bitcast_convert_type(packed, jnp.int32).reshape(-1, 128)


def _gather_rows(idx_ref, first_token, tbl_ref, buf_ref, bank):
    token_idx = [idx_ref.at[pl.ds((first_token + w) * PEER_SLOTS, PEER_SLOTS)] for w in range(PEER_WIDTH)]
    for m in range(PEER_SLOTS):
        for w in range(PEER_WIDTH):
            start = pl.multiple_of(token_idx[w][m], ROW_PLANES)
            buf_ref[bank * PEER_WIDTH + w, pl.ds(m, ROW_PLANES, stride=PLANE_STRIDE), :] = (
                tbl_ref[pl.ds(start, ROW_PLANES), :])


def _plane_halves(buf_ref, j):
    words = buf_ref[pl.ds(j * PLANE_STRIDE, PEER_SLOTS), :]
    lo = lax.bitcast_convert_type(words << 16, F32)
    hi = lax.bitcast_convert_type(words & jnp.int32(-65536), F32)
    return lo, hi


PEER_WIDTH = 1
PEER_GROUP = 2 * PEER_WIDTH


def _token_groups(tb, idx_ref, tbl_ref, buf_ref, compute, init):
    _gather_rows(idx_ref, 0, tbl_ref, buf_ref, 0)

    def group(i, carry):
        t0 = PEER_GROUP * i
        for bank in range(2):
            nxt = t0 + PEER_WIDTH if bank == 0 else jnp.minimum(t0 + PEER_GROUP, tb - PEER_WIDTH)
            _gather_rows(idx_ref, nxt, tbl_ref, buf_ref, 1 - bank)
            for w in range(PEER_WIDTH):
                pos = bank * PEER_WIDTH + w
                carry = compute(t0 + pos, buf_ref.at[pos], pos, carry)
        return carry

    return lax.fori_loop(0, tb // PEER_GROUP, group, init)


def _peer_act_kernel(tb, idx_ref, h_ref, tbl_ref, act_ref, buf_ref, acc_ref):
    token_lane = lax.broadcasted_iota(jnp.int32, (PEER_SLOTS, tb), 1)
    acc_ref[...] = jnp.zeros_like(acc_ref)

    def finish(t, pos, acts):
        return jnp.where(token_lane == t, jnp.sum(acc_ref[pos], axis=1, keepdims=True), acts)

    def compute(t, buf, pos, acts):
        acts = finish(t - PEER_GROUP, pos, acts)
        hrow = h_ref[pl.ds(t, 1), :]
        acc = None
        for j in range(ROW_PLANES):
            lo, hi = _plane_halves(buf, j)
            term = (lo * hrow[:, j * 128:(j + 1) * 128]
                    + hi * hrow[:, ROW_WORDS + j * 128:ROW_WORDS + (j + 1) * 128])
            acc = term if acc is None else acc + term
        acc_ref[pos] = acc
        return acts

    acts = _token_groups(tb, idx_ref, tbl_ref, buf_ref, compute, jnp.zeros((PEER_SLOTS, tb), F32))
    for pos in range(PEER_GROUP):
        acts = finish(tb - PEER_GROUP + pos, pos, acts)
    act_ref[...] = acts


def _peer_out_kernel(tb, idx_ref, act_ref, gate_ref, tbl_ref, f_ref, buf_ref, coef_ref, col_ref):
    act = act_ref[...]
    coef_ref[...] = gate_ref[...] * (0.5 * act * (1.0 + lax.erf(act * (2.0 ** -0.5))))
    token_lane = lax.broadcasted_iota(jnp.int32, (PEER_SLOTS, tb), 1)

    def stage_coef(t, pos):
        col = jnp.sum(jnp.where(token_lane == t, coef_ref[...], 0.0), axis=1, keepdims=True)
        col_ref[pos] = jnp.broadcast_to(col, (PEER_SLOTS, 128))

    for pos in range(PEER_GROUP):
        stage_coef(pos, pos)

    def compute(t, buf, pos, carry):
        coef = col_ref[pos]
        los, his = [], []
        for j in range(ROW_PLANES):
            lo, hi = _plane_halves(buf, j)
            los.append(jnp.sum(lo * coef, axis=0, keepdims=True))
            his.append(jnp.sum(hi * coef, axis=0, keepdims=True))
        f_ref[pl.ds(t, 1), :] = jnp.concatenate(los + his, axis=1)
        stage_coef(t + PEER_GROUP, pos)
        return carry

    _token_groups(tb, idx_ref, tbl_ref, buf_ref, compute, 0)


def _peer_experts(h2, idx_flat, gate_t, tbl_u, tbl_v, tb):
    n, d = h2.shape
    smem = pl.BlockSpec((tb * PEER_SLOTS,), lambda i: (i,), memory_space=pltpu.SMEM)
    row = pl.BlockSpec((tb, d), lambda i: (i, 0))
    slot_major = pl.BlockSpec((PEER_SLOTS, tb), lambda i: (0, i))
    table = pl.BlockSpec(memory_space=pltpu.VMEM)
    buf = pltpu.VMEM((PEER_GROUP, ROW_PLANES * PLANE_STRIDE, 128), jnp.int32)
    staged = pltpu.VMEM((PEER_GROUP, PEER_SLOTS, 128), F32)
    act_t = pl.pallas_call(
        functools.partial(_peer_act_kernel, tb),
        grid=(n // tb,),
        in_specs=[smem, row, table],
        out_specs=slot_major,
        out_shape=jax.ShapeDtypeStruct((PEER_SLOTS, n), F32),
        scratch_shapes=[buf, staged],
        compiler_params=_cparams("arbitrary"),
        name="peer_act",
    )(idx_flat, h2, tbl_u)
    return pl.pallas_call(
        functools.partial(_peer_out_kernel, tb),
        grid=(n // tb,),
        in_specs=[smem, slot_major, slot_major, table],
        out_specs=row,
        out_shape=jax.ShapeDtypeStruct((n, d), F32),
        scratch_shapes=[buf, pltpu.VMEM((PEER_SLOTS, tb), F32), staged],
        compiler_params=_cparams("arbitrary"),
        name="peer_out",
    )(idx_flat, act_t, gate_t, tbl_v)


def _final_kernel(x1_ref, f_ref, g2_ref, lg_ref, lb_ref, y_ref):
    y_ref[...] = _ln_rows(DN_ALPHA * x1_ref[...] + g2_ref[...] * f_ref[...]) * lg_ref[...] + lb_ref[...]


def _final(x1, f, g2, ln_g, ln_b, rows_per_mod, rb):
    n, d = x1.shape
    row = pl.BlockSpec((rb, d), lambda i: (i, 0))
    g2, g2_spec = _mod_spec(g2, rows_per_mod, rb, d)
    vec = pl.BlockSpec((1, d), lambda i: (0, 0))
    return pl.pallas_call(
        _final_kernel,
        grid=(n // rb,),
        in_specs=[row, row, g2_spec, vec, vec],
        out_specs=row,
        out_shape=jax.ShapeDtypeStruct((n, d), F32),
        compiler_params=_cparams("parallel"),
        name="final_ln",
    )(x1, f, g2, ln_g.reshape(1, d), ln_b.reshape(1, d))


ROW_BLOCK = 256
SCAN_BATCH = 4
SCAN_BLOCK = 128
SB_BLOCK = 1024
SB_KEY_BLOCK = 256
SB_PAGES = 16
ROUTE_BLOCK = 128
PEER_BLOCK = 128


def _mixer_half(x, mods, shift_prev, wkv0, attend, wts, rows_per_mod, rb):
    bsz, t, d = x.shape
    n = bsz * t
    sh1, sc1, g1, sh2, sc2, _ = mods
    xr = x.reshape(n, d)
    p, q, k, v, kb, vb = _inproj(xr, sh1, sc1, wts["w_in"], rows_per_mod, rb)
    p3 = p.reshape(bsz, t, RWKV_PROJ)
    rw, wkv_new = _rwkv(p3, shift_prev, wkv0, wts["rwkv"], SCAN_BATCH, SCAN_BLOCK)
    sb = attend(q, k, v, kb, vb)
    x1, h2, qp = _outproj(rw.reshape(n, WIDTH), sb.reshape(n, WIDTH), xr, g1, sh2, sc2, wts["w_out"],
                          wts["ln1_g"], wts["ln1_b"], wts["w_pq"], rows_per_mod, rb)
    heads = lambda a: a.reshape(bsz, t, N_HEADS, HEAD_DIM)
    return x1, h2, qp, heads(k), heads(v), wkv_new, p3[:, -1]


def _layer_pair(xp, xs, mod_p, mod_s, cache_k, cache_v, page_table, state_wkv, state_shift, wts):
    bp, tp, d = xp.shape
    bs, ts, _ = xs.shape
    assert ts == 1, "the paged attention handles one new token per sequence"
    n_p = bp * tp

    def attend_p(q, k, v, kb, vb):
        r = lambda a: a.reshape(bp, tp, WIDTH)
        return _sb_prompt(r(q), r(kb), r(vb), wts["sb_bias"], SB_BLOCK, SB_KEY_BLOCK)

    def attend_s(q, k, v, kb, vb):
        heads = lambda a: a.astype(F32).reshape(bs, N_HEADS, HEAD_DIM)
        return _sb_sample(heads(q), heads(k), heads(v), wts["sb_bias"], cache_k, cache_v, page_table, SB_PAGES)

    zero_shift = jnp.zeros((bp, RWKV_PROJ), xp.dtype)
    zero_wkv = jnp.zeros((bp, N_HEADS, HEAD_DIM, HEAD_DIM), state_wkv.dtype)
    x1p, h2p, qpp, kp, vp, wp, sp = _mixer_half(xp, mod_p, zero_shift, zero_wkv, attend_p, wts, tp, ROW_BLOCK)
    x1s, h2s, qps, ks, vs, ws, ss = _mixer_half(xs, mod_s, state_shift, state_wkv, attend_s, wts, 1, bs)

    def channel_mixer(h2, qp):
        n = h2.shape[0]
        eidx, gate = _route(qp, wts["sub_keys"], ROUTE_BLOCK)
        idx_flat = eidx.transpose(2, 0, 1).reshape(-1) * ROW_PLANES
        return _peer_experts(h2, idx_flat, gate.reshape(PEER_SLOTS, n), wts["peer_u"], wts["peer_v"], PEER_BLOCK)

    yp = _final(x1p, channel_mixer(h2p, qpp), mod_p[5], wts["ln2_g"], wts["ln2_b"], tp, ROW_BLOCK).reshape(bp, tp, d)
    ys = _final(x1s, channel_mixer(h2s, qps), mod_s[5], wts["ln2_g"], wts["ln2_b"], 1, bs).reshape(bs, ts, d)
    return yp, ys, kp, vp, ks, vs, wp, ws, sp, ss


def kernel(x_prompt, x_sample, c_prompt, c_sample, cache_k, cache_v, page_table, state_wkv, state_shift,
           w_cond, b_cond, w_in, mu_shift, w0, w_up, a0, a_up, g_up, k_k, k_a, r_k, lnx_g, lnx_b,
           sb_bias, w_out, ln1_g, ln1_b, w_pq, sub_keys, peer_u, peer_v, ln2_g, ln2_b):
    depth = w_in.shape[0]
    bp, bs = c_prompt.shape[0], c_sample.shape[0]
    pad = (-(bp + bs)) % 8
    c_all = jnp.concatenate([c_prompt, c_sample, jnp.zeros((pad, c_prompt.shape[1]), c_prompt.dtype)], axis=0)
    yp, ys = x_prompt, x_sample
    outs = [[] for _ in range(8)]
    for l in range(depth):
        wts = dict(
            w_in=w_in[l].astype(BF16), w_out=w_out[l].astype(BF16), w_pq=w_pq[l].astype(BF16),
            rwkv=_rwkv_params(mu_shift[l], w0[l], w_up[l], a0[l], a_up[l], g_up[l], k_k[l], k_a[l], r_k[l],
                              lnx_g[l], lnx_b[l]),
            sb_bias=sb_bias[l], ln1_g=ln1_g[l], ln1_b=ln1_b[l], ln2_g=ln2_g[l], ln2_b=ln2_b[l],
            sub_keys=_sub_key_pairs(sub_keys[l]), peer_u=_pack_table(peer_u[l]), peer_v=_pack_table(peer_v[l]))
        mod = _cond(c_all, w_cond[l], b_cond[l])
        mod_p = jnp.split(mod[:bp], N_MOD, axis=-1)
        mod_s = jnp.split(mod[bp:bp + bs], N_MOD, axis=-1)
        res = _layer_pair(yp, ys, mod_p, mod_s, cache_k[l], cache_v[l], page_table, state_wkv[l], state_shift[l], wts)
        yp, ys = res[0], res[1]
        for acc, val in zip(outs, res[2:]):
            acc.append(val)
    return (yp, ys) + tuple(jnp.stack(o) for o in outs)
```

```python
import functools

import jax
import jax.numpy as jnp
from jax import lax
from jax.experimental import pallas as pl
from jax.experimental.pallas import tpu as pltpu

F32 = jnp.float32
BF16 = jnp.bfloat16

D_MODEL = 1024
HEAD_DIM = 64
N_HEADS = 8
WIDTH = N_HEADS * HEAD_DIM
W_LORA, A_LORA, G_LORA = 64, 64, 128
RWKV_PROJ = 3 * WIDTH + W_LORA + A_LORA + G_LORA
IN_COLS = RWKV_PROJ + 3 * WIDTH
GN_EPS = HEAD_DIM * 1e-5
LN_EPS = 1e-5
SB_SCALE = HEAD_DIM ** -0.5
PAGE_SIZE = 128
PEER_HEADS = 8
N_KEYS = 128
PEER_TOPK = 16
PEER_SLOTS = PEER_HEADS * PEER_TOPK
N_MOD = 6
DEPTH = 1
DN_ALPHA = (2 * DEPTH) ** 0.25

VMEM_LIMIT = 56 * 1024 * 1024


def _cparams(*sem):
    return pltpu.CompilerParams(dimension_semantics=sem, vmem_limit_bytes=VMEM_LIMIT)


def _ln_rows(x):
    mu = jnp.mean(x, axis=-1, keepdims=True)
    xc = x - mu
    var = jnp.mean(xc * xc, axis=-1, keepdims=True)
    return xc * lax.rsqrt(var + LN_EPS)


def _split_bf16(x):
    hi = x.astype(BF16)
    lo = (x - hi.astype(F32)).astype(BF16)
    return hi, lo


def _dot(a, b):
    return jnp.dot(a, b, preferred_element_type=F32)


def _dot_nt(a, b):
    return lax.dot_general(a, b, (((1,), (1,)), ((), ())), preferred_element_type=F32)


def _dot2(x, w_bf16):
    hi, lo = _split_bf16(x)
    return _dot(hi, w_bf16) + _dot(lo, w_bf16)


def _cond_kernel(c_ref, w_ref, b_ref, o_ref):
    c = c_ref[...]
    s = c * jax.nn.sigmoid(c)
    o_ref[...] = jnp.dot(s, w_ref[...], preferred_element_type=F32,
                         precision=lax.Precision.HIGHEST) + b_ref[...]


def _cond(c, w_cond, b_cond):
    n, d = c.shape
    cols = w_cond.shape[1]
    bn = 1024
    return pl.pallas_call(
        _cond_kernel,
        grid=(cols // bn,),
        in_specs=[pl.BlockSpec((n, d), lambda j: (0, 0)),
                  pl.BlockSpec((d, bn), lambda j: (0, j)),
                  pl.BlockSpec((1, bn), lambda j: (0, j))],
        out_specs=pl.BlockSpec((n, bn), lambda j: (0, j)),
        out_shape=jax.ShapeDtypeStruct((n, cols), F32),
        compiler_params=_cparams("parallel"),
        name="cond",
    )(c, w_cond, b_cond.reshape(1, cols))


def _inproj_kernel(x_ref, sh_ref, sc_ref, w_ref, p_ref, q_ref, k_ref, v_ref, kb_ref, vb_ref):
    h = _ln_rows(x_ref[...]) * (1.0 + sc_ref[...]) + sh_ref[...]
    hb = h.astype(BF16)
    p_ref[...] = _dot(hb, w_ref[:, :RWKV_PROJ])
    q_ref[...] = _dot(hb, w_ref[:, RWKV_PROJ:RWKV_PROJ + WIDTH]).astype(BF16)
    k = _dot(hb, w_ref[:, RWKV_PROJ + WIDTH:RWKV_PROJ + 2 * WIDTH])
    v = _dot(hb, w_ref[:, RWKV_PROJ + 2 * WIDTH:])
    k_ref[...] = k
    v_ref[...] = v
    kb_ref[...] = k.astype(BF16)
    vb_ref[...] = v.astype(BF16)


def _inproj(x, shift, scale, w_in_bf16, rows_per_mod, block_rows):
    n, d = x.shape
    rb = block_rows
    shift, mod_spec = _mod_spec(shift, rows_per_mod, rb, d)
    scale, _ = _mod_spec(scale, rows_per_mod, rb, d)
    row = lambda c: pl.BlockSpec((rb, c), lambda i: (i, 0))
    return pl.pallas_call(
        _inproj_kernel,
        grid=(n // rb,),
        in_specs=[row(d), mod_spec, mod_spec,
                  pl.BlockSpec((d, IN_COLS), lambda i: (0, 0))],
        out_specs=[row(RWKV_PROJ), row(WIDTH), row(WIDTH), row(WIDTH), row(WIDTH), row(WIDTH)],
        out_shape=[jax.ShapeDtypeStruct((n, RWKV_PROJ), F32),
                   jax.ShapeDtypeStruct((n, WIDTH), BF16),
                   jax.ShapeDtypeStruct((n, WIDTH), F32),
                   jax.ShapeDtypeStruct((n, WIDTH), F32),
                   jax.ShapeDtypeStruct((n, WIDTH), BF16),
                   jax.ShapeDtypeStruct((n, WIDTH), BF16)],
        compiler_params=_cparams("parallel"),
        name="inproj",
    )(x, shift, scale, w_in_bf16)


def _seg_ones(n):
    i = jnp.arange(n) // HEAD_DIM
    return (i[:, None] == i[None, :]).astype(BF16)


def _softplus(u):
    return jnp.maximum(u, 0.0) + jnp.log(1.0 + jnp.exp(-jnp.abs(u)))


def _rwkv_pre_kernel(has_prev, chunked, p_ref, prev_ref, mu_ref, w0_ref, wup_ref, a0_ref, aup_ref, gup_ref,
                     kk_ref, ka_ref, rk_ref, seg_ref, *refs):
    out_refs, carry_ref = refs[:-1], refs[-1]
    hi = lax.Precision.HIGHEST
    pf = p_ref[...]
    if has_prev:
        prev = prev_ref[...]
    else:
        tb = pl.program_id(1)
        first = jnp.where(tb == 0, prev_ref[...], carry_ref[...])
        rows = lax.broadcasted_iota(jnp.int32, pf.shape, 0)
        prev = jnp.where(rows == 0, first, pltpu.roll(pf, 1, axis=0))
        carry_ref[...] = pf[pf.shape[0] - 1:, :]
    pm = pf + (prev - pf) * mu_ref[...]
    r = pm[:, :WIDTH]
    k = pm[:, WIDTH:2 * WIDTH]
    v = pm[:, 2 * WIDTH:3 * WIDTH]
    dwa = pm[:, 3 * WIDTH:3 * WIDTH + W_LORA + A_LORA]
    dg = pm[:, 3 * WIDTH + W_LORA + A_LORA:]
    seg = seg_ref[...]
    w = -_softplus(-(w0_ref[...] + jnp.dot(jnp.tanh(dwa), wup_ref[...], precision=hi,
                                           preferred_element_type=F32))) - 0.5
    log_dec = -jnp.exp(w)
    a = jax.nn.sigmoid(a0_ref[...] + jnp.dot(dwa, aup_ref[...], precision=hi, preferred_element_type=F32))
    g = jnp.dot(jax.nn.sigmoid(dg), gup_ref[...], precision=hi, preferred_element_type=F32)
    kkr = k * kk_ref[...]
    kk = kkr * lax.rsqrt(jnp.maximum(_dot2(kkr * kkr, seg), 1e-24))
    km = k * (1.0 + (a - 1.0) * ka_ref[...])
    b = kk * a
    bonus = _dot2(r * km * rk_ref[...], seg) * v
    if chunked:
        outs = (kk, log_dec, b, km, v, r, g, bonus)
    else:
        dec = jnp.exp(log_dec)
        outs = (kk, dec, b, km, v, dec * r, _dot2(b * r, seg), _dot2(km * r, seg), g, bonus)
    for ref, val in zip(out_refs, outs):
        ref[...] = val


def _rwkv_pre(p, prev, has_prev, chunked, prm, tb):
    bsz, t, _ = p.shape
    n_out = 8 if chunked else 10
    blk = lambda c: pl.BlockSpec((None, tb, c), lambda i, j: (i, j, 0))
    full = lambda a: pl.BlockSpec(a.shape, lambda i, j: (0,) * a.ndim)
    prev_spec = blk(RWKV_PROJ) if has_prev else pl.BlockSpec((None, 1, RWKV_PROJ), lambda i, j: (i, 0, 0))
    params = [prm["mu"], prm["w0"], prm["wup"], prm["a0"], prm["aup"], prm["gup"],
              prm["k_k"], prm["k_a"], prm["r_k"], prm["seg512"]]
    return pl.pallas_call(
        functools.partial(_rwkv_pre_kernel, has_prev, chunked),
        grid=(bsz, t // tb),
        in_specs=[blk(RWKV_PROJ), prev_spec] + [full(a) for a in params],
        out_specs=[blk(WIDTH)] * n_out,
        out_shape=[jax.ShapeDtypeStruct((bsz, t, WIDTH), F32)] * n_out,
        scratch_shapes=[pltpu.VMEM((1, RWKV_PROJ), F32)],
        compiler_params=_cparams("parallel", "arbitrary"),
        name="rwkv_pre",
    )(p, prev, *params)


def _rwkv_scan_kernel(nb_count, tb, kk_ref, dec_ref, b_ref, km_ref, v_ref, wr_ref, br_ref, kr_ref,
                      g_ref, bon_ref, s0_ref, lg_ref, lb_ref, seg256_ref, seg512_ref,
                      out_ref, st_ref, s_scr, y_scr):
    step_blk = pl.program_id(1)

    @pl.when(step_blk == 0)
    def _():
        s_scr[...] = s0_ref[...]

    shape = (HEAD_DIM, WIDTH)
    ident = (lax.broadcasted_iota(jnp.int32, shape, 1) & (HEAD_DIM - 1)) == lax.broadcasted_iota(jnp.int32, shape, 0)
    seg = seg256_ref[...]
    half = WIDTH // 2

    def segsum(lhs):
        return jnp.concatenate([_dot(lhs[:, :half], seg), _dot(lhs[:, half:], seg)], axis=1)

    def step(t, carry):
        for nb in range(nb_count):
            row = lambda ref: ref[nb, pl.ds(t, 1), :]
            s = s_scr[nb]
            p_hi, p_lo = _split_bf16(s * row(kk_ref))
            dv = jnp.where(ident, row(v_ref), 0.0).astype(BF16)
            pr = (s * row(wr_ref)).astype(BF16)
            res = segsum(jnp.concatenate([p_hi, p_lo, dv, pr], axis=0))
            sa = res[:HEAD_DIM] + res[HEAD_DIM:2 * HEAD_DIM]
            vcol = res[2 * HEAD_DIM:3 * HEAD_DIM]
            ycol = res[3 * HEAD_DIM:] - sa * row(br_ref)
            s_scr[nb] = s * row(dec_ref) - sa * row(b_ref) + vcol * row(km_ref)
            y_scr[nb, pl.ds(t, 1), :] = (jnp.sum(jnp.where(ident, ycol, 0.0), axis=0, keepdims=True)
                                         + row(v_ref) * row(kr_ref))
        return carry

    lax.fori_loop(0, tb, step, 0)

    seg512 = seg512_ref[...]
    for nb in range(nb_count):
        y = y_scr[nb]
        mu = _dot2(y, seg512) * (1.0 / HEAD_DIM)
        yc = y - mu
        var = _dot2(yc * yc, seg512) * (1.0 / HEAD_DIM)
        yn = yc * lax.rsqrt(var + GN_EPS) * lg_ref[...] + lb_ref[...]
        out_ref[nb] = (yn + bon_ref[nb]) * g_ref[nb]

    @pl.when(step_blk == pl.num_programs(1) - 1)
    def _():
        st_ref[...] = s_scr[...]


def _rwkv_scan(pre, s0, prm, nb, tb):
    kk, dec, b, km, v, wr, br, kr, g, bon = pre
    bsz, t, _ = kk.shape
    blk = pl.BlockSpec((nb, tb, WIDTH), lambda i, j: (i, j, 0))
    sblk = pl.BlockSpec((nb, HEAD_DIM, WIDTH), lambda i, j: (i, 0, 0))
    full = lambda a: pl.BlockSpec(a.shape, lambda i, j: (0,) * a.ndim)
    params = [prm["lnx_g"], prm["lnx_b"], prm["seg256"], prm["seg512"]]
    return pl.pallas_call(
        functools.partial(_rwkv_scan_kernel, nb, tb),
        grid=(bsz // nb, t // tb),
        in_specs=[blk] * 10 + [sblk] + [full(a) for a in params],
        out_specs=[blk, sblk],
        out_shape=[jax.ShapeDtypeStruct((bsz, t, WIDTH), F32),
                   jax.ShapeDtypeStruct((bsz, HEAD_DIM, WIDTH), F32)],
        scratch_shapes=[pltpu.VMEM((nb, HEAD_DIM, WIDTH), F32), pltpu.VMEM((nb, tb, WIDTH), F32)],
        compiler_params=_cparams("parallel", "arbitrary"),
        name="rwkv_scan",
    )(kk, dec, b, km, v, wr, br, kr, g, bon, s0, *params)


def _rwkv_params(mu_shift, w0, w_up, a0, a_up, g_up, k_k, k_a, r_k, lnx_g, lnx_b):
    row = lambda a: a.reshape(1, -1).astype(F32)
    zeros = jnp.zeros((A_LORA, WIDTH), F32)
    return dict(mu=row(mu_shift), w0=row(w0), a0=row(a0), k_k=row(k_k), k_a=row(k_a), r_k=row(r_k),
                lnx_g=row(lnx_g), lnx_b=row(lnx_b), gup=g_up,
                wup=jnp.concatenate([w_up, zeros], axis=0), aup=jnp.concatenate([zeros, a_up], axis=0),
                seg256=_seg_ones(WIDTH // 2), seg512=_seg_ones(WIDTH))


def _state_to_rows(wkv):
    bsz = wkv.shape[0]
    return wkv.transpose(0, 2, 1, 3).reshape(bsz, HEAD_DIM, WIDTH)


def _rows_to_state(s):
    bsz = s.shape[0]
    return s.reshape(bsz, HEAD_DIM, N_HEADS, HEAD_DIM).transpose(0, 2, 1, 3)


CHUNK = 64
PREP_CHUNKS = 4
PAIR = 2 * HEAD_DIM
N_PAIRS = WIDTH // PAIR
_NN = ((1,), (0,))
_NT = ((1,), (1,))
_TN = ((0,), (0,))


def _dot3(a, b, dims):
    ah, al = _split_bf16(a)
    bh, bl = _split_bf16(b)
    dg = lambda x, y: lax.dot_general(x, y, (dims, ((), ())), preferred_element_type=F32)
    return dg(ah, bh) + dg(ah, bl) + dg(al, bh)


def _dot1(a, b, dims):
    return lax.dot_general(a.astype(BF16), b.astype(BF16), (dims, ((), ())), preferred_element_type=F32)


def _by_head(x, first):
    zero = jnp.zeros_like(x)
    return jnp.concatenate([jnp.where(first, x, zero), jnp.where(first, zero, x)], axis=0)


def _fold_heads(x):
    c = x.shape[0] // 2
    return x[:c] + x[c:]


def _rwkv_chunk_prep_kernel(kk_ref, ld_ref, b_ref, km_ref, v_ref, r_ref, tri_ref,
                            w1_ref, w2_ref, qr_ref, arb_ref, y0_ref, bd_ref, sadd_ref, gc_ref):
    c = CHUNK
    tri = tri_ref[...]
    first = lax.broadcasted_iota(jnp.int32, (1, PAIR), 1) < HEAD_DIM
    rows = lax.broadcasted_iota(jnp.int32, (PAIR, PAIR), 0)
    cols = lax.broadcasted_iota(jnp.int32, (PAIR, PAIR), 1)
    same = (rows // c) == (cols // c)
    strict = same & (cols < rows)
    incl = same & (cols <= rows)
    eye = jnp.where(rows == cols, 1.0, 0.0)

    units = []
    for cc in range(kk_ref.shape[0] // c):
        rs = slice(cc * c, (cc + 1) * c)
        ld = ld_ref[rs, :]
        ld_hi, ld_lo = _split_bf16(ld)
        cum = _dot(tri, ld_hi) + _dot(tri, ld_lo)
        last = cum[c - 1:, :]
        g_end = jnp.exp(last - cum)
        g_inv = jnp.exp(-cum)
        qa = kk_ref[rs, :] * jnp.exp(cum - ld)
        qr = r_ref[rs, :] * jnp.exp(cum)
        kb, kt = b_ref[rs, :] * g_inv, km_ref[rs, :] * g_inv
        bd, kd = b_ref[rs, :] * g_end, km_ref[rs, :] * g_end
        v = v_ref[rs, :]
        qr_ref[rs, :] = qr
        bd_ref[rs, :] = bd
        gc_ref[cc] = jnp.exp(last)
        for p in range(N_PAIRS):
            sl = slice(p * PAIR, (p + 1) * PAIR)
            units.append(dict(cc=cc, p=p, rs=rs, sl=sl, v=v[:, sl], kd=kd[:, sl],
                              **{k: _by_head(x[:, sl], first)
                                 for k, x in dict(qa2=qa, qr2=qr, kb2=kb, kt2=kt, v2=v).items()}))
    n = [-jnp.where(strict, _dot3(u["qa2"], u["kb2"], _NT), 0.0) for u in units]
    aak = [jnp.where(strict, _dot3(u["qa2"], u["kt2"], _NT), 0.0) for u in units]
    for u in units:
        arb_ref[u["rs"], u["sl"]] = _fold_heads(jnp.where(incl, _dot1(u["qr2"], u["kb2"], _NT), 0.0))
        ark = jnp.where(incl, _dot1(u["qr2"], u["kt2"], _NT), 0.0)
        y0_ref[u["rs"], u["sl"]] = _fold_heads(_dot1(ark, u["v2"], _NN))
        sadd_ref[u["cc"], u["p"]] = jnp.where(same, _dot3(u["v"], u["kd"], _TN), 0.0)
    t_inv = [eye + x for x in n]
    power = n
    for _ in range(c.bit_length() - 2):
        power = [_dot(x.astype(BF16), x.astype(BF16)) for x in power]
        t_inv = [t + _dot(t.astype(BF16), x.astype(BF16)) for t, x in zip(t_inv, power)]
    rhs = [_dot3(a, u["v2"], _NN) for a, u in zip(aak, units)]
    for t, r, u in zip(t_inv, rhs, units):
        w1_ref[u["rs"], u["sl"]] = _fold_heads(_dot3(t, u["qa2"], _NN))
        w2_ref[u["rs"], u["sl"]] = _fold_heads(_dot3(t, r, _NN))


def _rwkv_chunk_scan_kernel(nb_count, w1_ref, w2_ref, qr_ref, arb_ref, y0_ref, bd_ref, sadd_ref, gc_ref,
                            g_ref, bon_ref, s0_ref, lg_ref, lb_ref, seg512_ref, out_ref, st_ref, sx_scr):
    c = CHUNK
    chunk_id = pl.program_id(1)
    first = lax.broadcasted_iota(jnp.int32, (1, PAIR), 1) < HEAD_DIM
    rows = lax.broadcasted_iota(jnp.int32, (PAIR, PAIR), 0)
    cols = lax.broadcasted_iota(jnp.int32, (PAIR, PAIR), 1)
    same = (rows // HEAD_DIM) == (cols // HEAD_DIM)

    @pl.when(chunk_id == 0)
    def _():
        for nb in range(nb_count):
            for p in range(N_PAIRS):
                sx_scr[nb, p] = _by_head(s0_ref[nb, :, p * PAIR:(p + 1) * PAIR], first)

    seg512 = seg512_ref[...]
    chains = [(nb, p, slice(p * PAIR, (p + 1) * PAIR)) for nb in range(nb_count) for p in range(N_PAIRS)]
    proj = [_dot3(jnp.concatenate([w1_ref[nb, :, sl], qr_ref[nb, :, sl]], axis=0), sx_scr[nb, p], _NT)
            for nb, p, sl in chains]
    us = [proj[i][:c] + w2_ref[nb, :, sl] for i, (nb, p, sl) in enumerate(chains)]
    ys = [proj[i][c:] - _dot1(arb_ref[nb, :, sl], _by_head(us[i], first), _NN) + y0_ref[nb, :, sl]
          for i, (nb, p, sl) in enumerate(chains)]
    for i, (nb, p, sl) in enumerate(chains):
        upd = jnp.where(same, _dot3(us[i], bd_ref[nb, :, sl], _TN), 0.0)
        sx_scr[nb, p] = sx_scr[nb, p] * gc_ref[nb, :, sl] - upd + sadd_ref[nb, p]
    for nb in range(nb_count):
        y = jnp.concatenate(ys[nb * N_PAIRS:(nb + 1) * N_PAIRS], axis=1)
        mu = _dot2(y, seg512) * (1.0 / HEAD_DIM)
        yc = y - mu
        var = _dot2(yc * yc, seg512) * (1.0 / HEAD_DIM)
        yn = yc * lax.rsqrt(var + GN_EPS) * lg_ref[...] + lb_ref[...]
        out_ref[nb] = (yn + bon_ref[nb]) * g_ref[nb]

    @pl.when(chunk_id == pl.num_programs(1) - 1)
    def _():
        for nb in range(nb_count):
            st_ref[nb] = jnp.concatenate([_fold_heads(sx_scr[nb, p]) for p in range(N_PAIRS)], axis=1)


def _rwkv_chunked(pre, s0, prm, nb):
    kk, ld, b, km, v, r, g, bon = pre
    bsz, t, _ = kk.shape
    n_chunks = t // CHUNK
    idx = jnp.arange(CHUNK)
    tri = (idx[None, :] <= idx[:, None]).astype(BF16)
    cps = PREP_CHUNKS
    blk = pl.BlockSpec((None, cps * CHUNK, WIDTH), lambda i, j: (i, j, 0))
    wide = jax.ShapeDtypeStruct((bsz, t, WIDTH), F32)
    w1, w2, qr, arb, y0, bd, sadd, gc = pl.pallas_call(
        _rwkv_chunk_prep_kernel,
        grid=(bsz, n_chunks // cps),
        in_specs=[blk] * 6 + [pl.BlockSpec((CHUNK, CHUNK), lambda i, j: (0, 0))],
        out_specs=[blk] * 6 + [pl.BlockSpec((None, cps, N_PAIRS, PAIR, PAIR), lambda i, j: (i, j, 0, 0, 0)),
                               pl.BlockSpec((None, cps, 1, WIDTH), lambda i, j: (i, j, 0, 0))],
        out_shape=[wide] * 6 + [jax.ShapeDtypeStruct((bsz, n_chunks, N_PAIRS, PAIR, PAIR), F32),
                                jax.ShapeDtypeStruct((bsz, n_chunks, 1, WIDTH), F32)],
        compiler_params=_cparams("parallel", "parallel"),
        name="rwkv_chunk_prep",
    )(kk, ld, b, km, v, r, tri)
    nblk = pl.BlockSpec((nb, CHUNK, WIDTH), lambda i, j: (i, j, 0))
    sblk = pl.BlockSpec((nb, HEAD_DIM, WIDTH), lambda i, j: (i, 0, 0))
    full = lambda a: pl.BlockSpec(a.shape, lambda i, j: (0,) * a.ndim)
    params = [prm["lnx_g"], prm["lnx_b"], prm["seg512"]]
    return pl.pallas_call(
        functools.partial(_rwkv_chunk_scan_kernel, nb),
        grid=(bsz // nb, n_chunks),
        in_specs=[nblk] * 6
        + [pl.BlockSpec((nb, None, N_PAIRS, PAIR, PAIR), lambda i, j: (i, j, 0, 0, 0)),
           pl.BlockSpec((nb, None, 1, WIDTH), lambda i, j: (i, j, 0, 0)),
           nblk, nblk, sblk] + [full(a) for a in params],
        out_specs=[nblk, sblk],
        out_shape=[wide, jax.ShapeDtypeStruct((bsz, HEAD_DIM, WIDTH), F32)],
        scratch_shapes=[pltpu.VMEM((nb, N_PAIRS, PAIR, PAIR), F32)],
        compiler_params=_cparams("parallel", "arbitrary"),
        name="rwkv_chunk_scan",
    )(w1, w2, qr, arb, y0, bd, sadd, gc, g, bon, s0, *params)


def _rwkv(p, shift_prev, wkv0, prm, nb, tb):
    bsz, t, _ = p.shape
    s0 = _state_to_rows(wkv0)
    if t == 1:
        pre = _rwkv_pre(p.reshape(1, bsz, RWKV_PROJ), shift_prev.reshape(1, bsz, RWKV_PROJ), True, False, prm, bsz)
        out, st = _rwkv_scan([a.reshape(bsz, 1, WIDTH) for a in pre], s0, prm, nb, 1)
    else:
        pre = _rwkv_pre(p, shift_prev.reshape(bsz, 1, RWKV_PROJ), False, True, prm, tb)
        out, st = _rwkv_chunked(pre, s0, prm, nb)
    return out, _rows_to_state(st)


def _sb_tile(z, mask, carry, neg_ge, neg_ones):
    sp = _softplus(z)
    if mask is not None:
        sp = jnp.where(mask, sp, 0.0)
    spb = sp.astype(BF16)
    inc = _dot(spb, neg_ge)
    bk = z.shape[1]
    a = jnp.exp(z + inc + jnp.concatenate([carry] * (bk // carry.shape[1]), axis=1))
    if mask is not None:
        a = jnp.where(mask, a, 0.0)
    if neg_ones is None:
        return a.astype(BF16), carry + jnp.broadcast_to(inc[:, :1], carry.shape)
    return a.astype(BF16), carry + _dot(spb, neg_ones)


def _sb_prompt_kernel(bq, bk, bias_ref, q_ref, k_ref, v_ref, mge_ref, o_ref):
    hp = pl.program_id(1)
    qi = pl.program_id(2)
    lane = lax.broadcasted_iota(jnp.int32, (1, 2 * HEAD_DIM), 1)
    first = lane < HEAD_DIM
    q2 = q_ref[...] * jnp.asarray(SB_SCALE, BF16)
    zero = jnp.zeros_like(q2)
    q_heads = (jnp.where(first, q2, zero), jnp.where(first, zero, q2))
    biases = (bias_ref[2 * hp], bias_ref[2 * hp + 1])
    m_ge = mge_ref[...]
    per_q = bq // bk
    rows = lax.broadcasted_iota(jnp.int32, (bq, bk), 0)
    cols = lax.broadcasted_iota(jnp.int32, (bq, bk), 1)

    def tile(j, mask, state):
        acc, carries = state
        kblk = k_ref[pl.ds(pl.multiple_of(j * bk, bk), bk), :]
        vblk = v_ref[pl.ds(pl.multiple_of(j * bk, bk), bk), :]
        vzero = jnp.zeros_like(vblk)
        v_heads = (jnp.where(first, vblk, vzero), jnp.where(first, vzero, vblk))
        new_carries = []
        for e in range(2):
            z = _dot_nt(q_heads[e], kblk) + biases[e]
            a, c = _sb_tile(z, mask, carries[e], m_ge, None)
            acc = acc + _dot(a, v_heads[e])
            new_carries.append(c)
        return acc, tuple(new_carries)

    zc = jnp.zeros((bq, 2 * HEAD_DIM), F32)
    state = (zc, (zc, zc))
    for u in reversed(range(per_q)):
        state = tile(qi * per_q + u, cols + u * bk < rows, state)
    n_full = qi * per_q

    def full_tiles(i, s):
        for u in range(per_q):
            s = tile(n_full - 1 - i * per_q - u, None, s)
        return s

    state = lax.fori_loop(0, qi, full_tiles, state)
    o_ref[...] = state[0]


def _sb_prompt(q, k, v, sb_bias, bq, bk):
    bsz, t, _ = q.shape
    pair = 2 * HEAD_DIM
    idx = jnp.arange(bk)
    m_ge = -(idx[:, None] >= idx[None, :]).astype(BF16)
    return pl.pallas_call(
        functools.partial(_sb_prompt_kernel, bq, bk),
        grid=(bsz, WIDTH // pair, t // bq),
        in_specs=[pl.BlockSpec(memory_space=pltpu.SMEM),
                  pl.BlockSpec((None, bq, pair), lambda b, h, i: (b, i, h)),
                  pl.BlockSpec((None, t, pair), lambda b, h, i: (b, 0, h)),
                  pl.BlockSpec((None, t, pair), lambda b, h, i: (b, 0, h)),
                  pl.BlockSpec((bk, bk), lambda b, h, i: (0, 0))],
        out_specs=pl.BlockSpec((None, bq, pair), lambda b, h, i: (b, i, h)),
        out_shape=jax.ShapeDtypeStruct((bsz, t, WIDTH), F32),
        compiler_params=_cparams("parallel", "parallel", "arbitrary"),
        name="sb_prompt",
    )(sb_bias.astype(F32), q, k, v, m_ge)


def _sb_sample_kernel(pp, n_pages, pt_ref, q_ref, knew_ref, vnew_ref, bias_ref, mge_ref, ones_ref, *refs):
    k_refs, v_refs = refs[:pp], refs[pp:2 * pp]
    o_ref, carry_scr, acc_scr = refs[2 * pp:]
    g = pl.program_id(1)

    @pl.when(g == 0)
    def _():
        carry_scr[...] = jnp.zeros_like(carry_scr)
        acc_scr[...] = jnp.zeros_like(acc_scr)

    head_row = lax.broadcasted_iota(jnp.int32, (N_HEADS, PAGE_SIZE), 0)
    head_row_d = lax.broadcasted_iota(jnp.int32, (N_HEADS, HEAD_DIM), 0)
    q = q_ref[...]
    qb = (q * SB_SCALE).astype(BF16)
    bias = bias_ref[...]
    zs = []
    for u in range(pp):
        z = None
        for h in range(N_HEADS):
            zh = _dot(qb, k_refs[u][h].astype(BF16))
            z = zh if z is None else jnp.where(head_row == h, zh, z)
        zs.append(z + bias)
    z_all = jnp.concatenate(zs, axis=1)
    spb = _softplus(z_all).astype(BF16)
    incs, carries = [], []
    carry = carry_scr[...]
    for u in range(pp):
        page = spb[:, u * PAGE_SIZE:(u + 1) * PAGE_SIZE]
        incs.append(_dot(page, mge_ref[...]))
        carries.append(carry)
        carry = carry + _dot(page, ones_ref[...])
    carry_scr[...] = carry
    a_all = jnp.exp(z_all + jnp.concatenate(incs, axis=1) + jnp.concatenate(carries, axis=1)).astype(BF16)
    acc = acc_scr[...]
    for u in range(pp):
        a = a_all[:, u * PAGE_SIZE:(u + 1) * PAGE_SIZE]
        for h in range(N_HEADS):
            acc = acc + jnp.where(head_row_d == h, _dot_nt(a, v_refs[u][h].astype(BF16)), 0.0)
    acc_scr[...] = acc

    @pl.when(g == pl.num_programs(1) - 1)
    def _():
        past = n_pages * PAGE_SIZE
        z_new = jnp.sum(q * knew_ref[...], axis=1, keepdims=True) * SB_SCALE + bias[:, :1]
        a_new = jnp.where(past < past, jnp.exp(-_softplus(-z_new)), 0.0)
        o_ref[...] = acc + a_new * vnew_ref[...]


def _sb_sample(q, k_new, v_new, sb_bias, cache_k, cache_v, page_table, pp):
    bsz = q.shape[0]
    n_pages = page_table.shape[1]
    idx = jnp.arange(PAGE_SIZE)
    m_ge = -(idx[:, None] >= idx[None, :]).astype(BF16)
    ones_cols = -jnp.ones((PAGE_SIZE, PAGE_SIZE), BF16)
    bias = jnp.broadcast_to(sb_bias.astype(F32)[:, None], (N_HEADS, PAGE_SIZE))
    rows = lambda c: c.transpose(0, 2, 3, 1)
    row = pl.BlockSpec((None, N_HEADS, HEAD_DIM), lambda b, g, pt: (b, 0, 0))
    full = lambda a: pl.BlockSpec(a.shape, lambda b, g, pt: (0,) * a.ndim)

    def page_spec(u):
        return pl.BlockSpec((None, N_HEADS, HEAD_DIM, PAGE_SIZE),
                            lambda b, g, pt: (pt[b * n_pages + n_pages - 1 - (g * pp + u)], 0, 0, 0))

    grid_spec = pltpu.PrefetchScalarGridSpec(
        num_scalar_prefetch=1,
        grid=(bsz, n_pages // pp),
        in_specs=[row, row, row, full(bias), full(m_ge), full(ones_cols)]
        + [page_spec(u) for u in range(pp)] * 2,
        out_specs=row,
        scratch_shapes=[pltpu.VMEM((N_HEADS, PAGE_SIZE), F32), pltpu.VMEM((N_HEADS, HEAD_DIM), F32)],
    )
    return pl.pallas_call(
        functools.partial(_sb_sample_kernel, pp, n_pages),
        grid_spec=grid_spec,
        out_shape=jax.ShapeDtypeStruct((bsz, N_HEADS, HEAD_DIM), F32),
        compiler_params=_cparams("parallel", "arbitrary"),
        name="sb_sample",
    )(page_table.reshape(-1), q, k_new, v_new, bias, m_ge, ones_cols, *([rows(cache_k)] * pp), *([rows(cache_v)] * pp))


def _outproj_kernel(rw_ref, sb_ref, x_ref, g1_ref, sh2_ref, sc2_ref, wo_ref, lg_ref, lb_ref, wpq_ref,
                    x1_ref, h2_ref, qp_ref):
    mix = _dot(rw_ref[...].astype(BF16), wo_ref[:WIDTH, :]) + _dot(sb_ref[...].astype(BF16), wo_ref[WIDTH:, :])
    x1 = _ln_rows(DN_ALPHA * x_ref[...] + g1_ref[...] * mix) * lg_ref[...] + lb_ref[...]
    h2 = _ln_rows(x1) * (1.0 + sc2_ref[...]) + sh2_ref[...]
    x1_ref[...] = x1
    h2_ref[...] = h2
    qp_ref[...] = _dot(h2.astype(BF16), wpq_ref[...]).astype(BF16)


def _mod_spec(a, rows_per_mod, rb, d):
    if rows_per_mod == 1:
        return a, pl.BlockSpec((rb, d), lambda i: (i, 0))
    per = rows_per_mod // rb
    return a.reshape(-1, 1, d), pl.BlockSpec((None, 1, d), lambda i: (i // per, 0, 0))


def _outproj(rw, sb, x, g1, sh2, sc2, w_out_bf16, ln_g, ln_b, w_pq_bf16, rows_per_mod, rb):
    n, d = x.shape
    row = lambda c: pl.BlockSpec((rb, c), lambda i: (i, 0))
    full = lambda a: pl.BlockSpec(a.shape, lambda i: (0,) * a.ndim)
    mods, mod_specs = zip(*[_mod_spec(a, rows_per_mod, rb, d) for a in (g1, sh2, sc2)])
    lg, lb = ln_g.reshape(1, d), ln_b.reshape(1, d)
    return pl.pallas_call(
        _outproj_kernel,
        grid=(n // rb,),
        in_specs=[row(WIDTH), row(WIDTH), row(d), *mod_specs, full(w_out_bf16), full(lg), full(lb), full(w_pq_bf16)],
        out_specs=[row(d), row(d), row(d)],
        out_shape=[jax.ShapeDtypeStruct((n, d), F32), jax.ShapeDtypeStruct((n, d), F32),
                   jax.ShapeDtypeStruct((n, d), BF16)],
        compiler_params=_cparams("parallel"),
        name="outproj",
    )(rw, sb, x, *mods, w_out_bf16, lg, lb, w_pq_bf16)


def _take_top(x, ids, payload, count):
    vals, picked = [], []
    for _ in range(count):
        m = jnp.max(x, axis=0, keepdims=True)
        pos = jnp.min(jnp.where(x == m, ids, jnp.finfo(F32).max), axis=0, keepdims=True)
        hit = ids == pos
        vals.append(m)
        picked.append(pos if payload is None else jnp.sum(jnp.where(hit, payload, 0.0), axis=0, keepdims=True))
        x = jnp.where(hit, -jnp.inf, x)
    return jnp.concatenate(vals, axis=0), jnp.concatenate(picked, axis=0)


def _pair_candidates(sv0, si0, sv1, si1):
    k = PEER_TOPK
    tokens = sv0.shape[1]
    wide = 4
    vals, ids, experts = [], [], []
    for a in range(wide):
        n = -(-(k // (a + 1)) // 8) * 8
        b_ids = lax.broadcasted_iota(jnp.int32, (n, tokens), 0)
        valid = b_ids < k // (a + 1)
        vals.append(jnp.where(valid, sv0[a:a + 1] + sv1[:n], -jnp.inf))
        ids.append(jnp.where(valid, a * k + b_ids, -1).astype(F32))
        experts.append(si0[a:a + 1] * N_KEYS + si1[:n])
    for b in range(k // (wide + 1)):
        last_a = k // (b + 1) - 1
        n = -(-(last_a + 1) // 8) * 8
        a_ids = lax.broadcasted_iota(jnp.int32, (n, tokens), 0)
        valid = (a_ids >= wide) & (a_ids <= last_a)
        vals.append(jnp.where(valid, sv0[:n] + sv1[b:b + 1], -jnp.inf))
        ids.append(jnp.where(valid, a_ids * k + b, -1).astype(F32))
        experts.append(si0[:n] * N_KEYS + si1[b:b + 1])
    return jnp.concatenate(vals, axis=0), jnp.concatenate(ids, axis=0), jnp.concatenate(experts, axis=0)


def _route_kernel(qp_ref, sk_ref, e_ref, g_ref):
    tokens = qp_ref.shape[0]
    scores = _dot_nt(sk_ref[...], qp_ref[...])
    key_ids = lax.broadcasted_iota(jnp.int32, (N_KEYS, tokens), 0).astype(F32)
    sv0, si0 = _take_top(scores[:N_KEYS], key_ids, None, PEER_TOPK)
    sv1, si1 = _take_top(scores[N_KEYS:], key_ids, None, PEER_TOPK)
    cand, cand_ids, cidx = _pair_candidates(sv0, si0, sv1, si1)
    top, eidx = _take_top(cand, cand_ids, cidx, PEER_TOPK)
    ex = jnp.exp(top - top[:1])
    e_ref[...] = eidx.astype(jnp.int32)
    g_ref[...] = ex / jnp.sum(ex, axis=0, keepdims=True)


def _route(qp, sk_pairs, tb):
    n = qp.shape[0]
    blk = pl.BlockSpec((None, PEER_TOPK, tb), lambda i, h: (h, 0, i))
    return pl.pallas_call(
        _route_kernel,
        grid=(n // tb, PEER_HEADS),
        in_specs=[pl.BlockSpec((tb, 2 * HEAD_DIM), lambda i, h: (i, h)),
                  pl.BlockSpec((None, 2 * N_KEYS, 2 * HEAD_DIM), lambda i, h: (h, 0, 0))],
        out_specs=[blk, blk],
        out_shape=[jax.ShapeDtypeStruct((PEER_HEADS, PEER_TOPK, n), jnp.int32),
                   jax.ShapeDtypeStruct((PEER_HEADS, PEER_TOPK, n), F32)],
        compiler_params=_cparams("parallel", "parallel"),
        name="peer_route",
    )(qp, sk_pairs)


def _sub_key_pairs(sub_keys):
    z = jnp.zeros_like(sub_keys[:, 0])
    top = jnp.concatenate([sub_keys[:, 0], z], axis=-1)
    bot = jnp.concatenate([z, sub_keys[:, 1]], axis=-1)
    return jnp.concatenate([top, bot], axis=1).astype(BF16)


ROW_WORDS = D_MODEL // 2
ROW_PLANES = ROW_WORDS // 128
PLANE_STRIDE = PEER_SLOTS + 8


def _pack_table(w):
    bits = lax.bitcast_convert_type(w.astype(BF16), jnp.uint16).astype(jnp.uint32)
    packed = bits[:, :ROW_WORDS] | (bits[:, ROW_WORDS:] << 16)
    return lax.bitcast_convert_type(packed, jnp.int32).reshape(-1, 128)


def _gather_rows(idx_ref, first_token, tbl_ref, buf_ref, bank):
    token_idx = [idx_ref.at[pl.ds((first_token + w) * PEER_SLOTS, PEER_SLOTS)] for w in range(PEER_WIDTH)]
    for m in range(PEER_SLOTS):
        for w in range(PEER_WIDTH):
            start = pl.multiple_of(token_idx[w][m], ROW_PLANES)
            buf_ref[bank * PEER_WIDTH + w, pl.ds(m, ROW_PLANES, stride=PLANE_STRIDE), :] = (
                tbl_ref[pl.ds(start, ROW_PLANES), :])


def _plane_halves(buf_ref, j):
    words = buf_ref[pl.ds(j * PLANE_STRIDE, PEER_SLOTS), :]
    lo = lax.bitcast_convert_type(words << 16, F32)
    hi = lax.bitcast_convert_type(words & jnp.int32(-65536), F32)
    return lo, hi


PEER_WIDTH = 1
PEER_GROUP = 2 * PEER_WIDTH


def _token_groups(tb, idx_ref, tbl_ref, buf_ref, compute, init):
    _gather_rows(idx_ref, 0, tbl_ref, buf_ref, 0)

    def group(i, carry):
        t0 = PEER_GROUP * i
        for bank in range(2):
            nxt = t0 + PEER_WIDTH if bank == 0 else jnp.minimum(t0 + PEER_GROUP, tb - PEER_WIDTH)
            _gather_rows(idx_ref, nxt, tbl_ref, buf_ref, 1 - bank)
            for w in range(PEER_WIDTH):
                pos = bank * PEER_WIDTH + w
                carry = compute(t0 + pos, buf_ref.at[pos], pos, carry)
        return carry

    return lax.fori_loop(0, tb // PEER_GROUP, group, init)


def _peer_act_kernel(tb, idx_ref, h_ref, tbl_ref, act_ref, buf_ref, acc_ref):
    token_lane = lax.broadcasted_iota(jnp.int32, (PEER_SLOTS, tb), 1)
    acc_ref[...] = jnp.zeros_like(acc_ref)

    def finish(t, pos, acts):
        return jnp.where(token_lane == t, jnp.sum(acc_ref[pos], axis=1, keepdims=True), acts)

    def compute(t, buf, pos, acts):
        acts = finish(t - PEER_GROUP, pos, acts)
        hrow = h_ref[pl.ds(t, 1), :]
        acc = None
        for j in range(ROW_PLANES):
            lo, hi = _plane_halves(buf, j)
            term = (lo * hrow[:, j * 128:(j + 1) * 128]
                    + hi * hrow[:, ROW_WORDS + j * 128:ROW_WORDS + (j + 1) * 128])
            acc = term if acc is None else acc + term
        acc_ref[pos] = acc
        return acts

    acts = _token_groups(tb, idx_ref, tbl_ref, buf_ref, compute, jnp.zeros((PEER_SLOTS, tb), F32))
    for pos in range(PEER_GROUP):
        acts = finish(tb - PEER_GROUP + pos, pos, acts)
    act_ref[...] = acts


def _peer_out_kernel(tb, idx_ref, act_ref, gate_ref, tbl_ref, f_ref, buf_ref, coef_ref, col_ref):
    act = act_ref[...]
    coef_ref[...] = gate_ref[...] * (0.5 * act * (1.0 + lax.erf(act * (2.0 ** -0.5))))
    token_lane = lax.broadcasted_iota(jnp.int32, (PEER_SLOTS, tb), 1)

    def stage_coef(t, pos):
        col = jnp.sum(jnp.where(token_lane == t, coef_ref[...], 0.0), axis=1, keepdims=True)
        col_ref[pos] = jnp.broadcast_to(col, (PEER_SLOTS, 128))

    for pos in range(PEER_GROUP):
        stage_coef(pos, pos)

    def compute(t, buf, pos, carry):
        coef = col_ref[pos]
        los, his = [], []
        for j in range(ROW_PLANES):
            lo, hi = _plane_halves(buf, j)
            los.append(jnp.sum(lo * coef, axis=0, keepdims=True))
            his.append(jnp.sum(hi * coef, axis=0, keepdims=True))
        f_ref[pl.ds(t, 1), :] = jnp.concatenate(los + his, axis=1)
        stage_coef(t + PEER_GROUP, pos)
        return carry

    _token_groups(tb, idx_ref, tbl_ref, buf_ref, compute, 0)


def _peer_experts(h2, idx_flat, gate_t, tbl_u, tbl_v, tb):
    n, d = h2.shape
    smem = pl.BlockSpec((tb * PEER_SLOTS,), lambda i: (i,), memory_space=pltpu.SMEM)
    row = pl.BlockSpec((tb, d), lambda i: (i, 0))
    slot_major = pl.BlockSpec((PEER_SLOTS, tb), lambda i: (0, i))
    table = pl.BlockSpec(memory_space=pltpu.VMEM)
    buf = pltpu.VMEM((PEER_GROUP, ROW_PLANES * PLANE_STRIDE, 128), jnp.int32)
    staged = pltpu.VMEM((PEER_GROUP, PEER_SLOTS, 128), F32)
    act_t = pl.pallas_call(
        functools.partial(_peer_act_kernel, tb),
        grid=(n // tb,),
        in_specs=[smem, row, table],
        out_specs=slot_major,
        out_shape=jax.ShapeDtypeStruct((PEER_SLOTS, n), F32),
        scratch_shapes=[buf, staged],
        compiler_params=_cparams("arbitrary"),
        name="peer_act",
    )(idx_flat, h2, tbl_u)
    return pl.pallas_call(
        functools.partial(_peer_out_kernel, tb),
        grid=(n // tb,),
        in_specs=[smem, slot_major, slot_major, table],
        out_specs=row,
        out_shape=jax.ShapeDtypeStruct((n, d), F32),
        scratch_shapes=[buf, pltpu.VMEM((PEER_SLOTS, tb), F32), staged],
        compiler_params=_cparams("arbitrary"),
        name="peer_out",
    )(idx_flat, act_t, gate_t, tbl_v)


def _final_kernel(x1_ref, f_ref, g2_ref, lg_ref, lb_ref, y_ref):
    y_ref[...] = _ln_rows(DN_ALPHA * x1_ref[...] + g2_ref[...] * f_ref[...]) * lg_ref[...] + lb_ref[...]


def _final(x1, f, g2, ln_g, ln_b, rows_per_mod, rb):
    n, d = x1.shape
    row = pl.BlockSpec((rb, d), lambda i: (i, 0))
    g2, g2_spec = _mod_spec(g2, rows_per_mod, rb, d)
    vec = pl.BlockSpec((1, d), lambda i: (0, 0))
    return pl.pallas_call(
        _final_kernel,
        grid=(n // rb,),
        in_specs=[row, row, g2_spec, vec, vec],
        out_specs=row,
        out_shape=jax.ShapeDtypeStruct((n, d), F32),
        compiler_params=_cparams("parallel"),
        name="final_ln",
    )(x1, f, g2, ln_g.reshape(1, d), ln_b.reshape(1, d))


ROW_BLOCK = 256
SCAN_BATCH = 4
SCAN_BLOCK = 128
SB_BLOCK = 1024
SB_KEY_BLOCK = 256
SB_PAGES = 16
ROUTE_BLOCK = 128
PEER_BLOCK = 128


def _mixer_half(x, mods, shift_prev, wkv0, attend, wts, rows_per_mod, rb):
    bsz, t, d = x.shape
    n = bsz * t
    sh1, sc1, g1, sh2, sc2, _ = mods
    xr = x.reshape(n, d)
    p, q, k, v, kb, vb = _inproj(xr, sh1, sc1, wts["w_in"], rows_per_mod, rb)
    p3 = p.reshape(bsz, t, RWKV_PROJ)
    rw, wkv_new = _rwkv(p3, shift_prev, wkv0, wts["rwkv"], SCAN_BATCH, SCAN_BLOCK)
    sb = attend(q, k, v, kb, vb)
    x1, h2, qp = _outproj(rw.reshape(n, WIDTH), sb.reshape(n, WIDTH), xr, g1, sh2, sc2, wts["w_out"],
                          wts["ln1_g"], wts["ln1_b"], wts["w_pq"], rows_per_mod, rb)
    heads = lambda a: a.reshape(bsz, t, N_HEADS, HEAD_DIM)
    return x1, h2, qp, heads(k), heads(v), wkv_new, p3[:, -1]


def _layer_pair(xp, xs, mod_p, mod_s, cache_k, cache_v, page_table, state_wkv, state_shift, wts):
    bp, tp, d = xp.shape
    bs, ts, _ = xs.shape
    assert ts == 1, "the paged attention handles one new token per sequence"
    n_p = bp * tp

    def attend_p(q, k, v, kb, vb):
        r = lambda a: a.reshape(bp, tp, WIDTH)
        return _sb_prompt(r(q), r(kb), r(vb), wts["sb_bias"], SB_BLOCK, SB_KEY_BLOCK)

    def attend_s(q, k, v, kb, vb):
        heads = lambda a: a.astype(F32).reshape(bs, N_HEADS, HEAD_DIM)
        return _sb_sample(heads(q), heads(k), heads(v), wts["sb_bias"], cache_k, cache_v, page_table, SB_PAGES)

    zero_shift = jnp.zeros((bp, RWKV_PROJ), xp.dtype)
    zero_wkv = jnp.zeros((bp, N_HEADS, HEAD_DIM, HEAD_DIM), state_wkv.dtype)
    x1p, h2p, qpp, kp, vp, wp, sp = _mixer_half(xp, mod_p, zero_shift, zero_wkv, attend_p, wts, tp, ROW_BLOCK)
    x1s, h2s, qps, ks, vs, ws, ss = _mixer_half(xs, mod_s, state_shift, state_wkv, attend_s, wts, 1, bs)

    def channel_mixer(h2, qp):
        n = h2.shape[0]
        eidx, gate = _route(qp, wts["sub_keys"], ROUTE_BLOCK)
        idx_flat = eidx.transpose(2, 0, 1).reshape(-1) * ROW_PLANES
        return _peer_experts(h2, idx_flat, gate.reshape(PEER_SLOTS, n), wts["peer_u"], wts["peer_v"], PEER_BLOCK)

    yp = _final(x1p, channel_mixer(h2p, qpp), mod_p[5], wts["ln2_g"], wts["ln2_b"], tp, ROW_BLOCK).reshape(bp, tp, d)
    ys = _final(x1s, channel_mixer(h2s, qps), mod_s[5], wts["ln2_g"], wts["ln2_b"], 1, bs).reshape(bs, ts, d)
    return yp, ys, kp, vp, ks, vs, wp, ws, sp, ss


def kernel(x_prompt, x_sample, c_prompt, c_sample, cache_k, cache_v, page_table, state_wkv, state_shift,
           w_cond, b_cond, w_in, mu_shift, w0, w_up, a0, a_up, g_up, k_k, k_a, r_k, lnx_g, lnx_b,
           sb_bias, w_out, ln1_g, ln1_b, w_pq, sub_keys, peer_u, peer_v, ln2_g, ln2_b):
    depth = w_in.shape[0]
    bp, bs = c_prompt.shape[0], c_sample.shape[0]
    pad = (-(bp + bs)) % 8
    c_all = jnp.concatenate([c_prompt, c_sample, jnp.zeros((pad, c_prompt.shape[1]), c_prompt.dtype)], axis=0)
    yp, ys = x_prompt, x_sample
    outs = [[] for _ in range(8)]
    for l in range(depth):
        wts = dict(
            w_in=w_in[l].astype(BF16), w_out=w_out[l].astype(BF16), w_pq=w_pq[l].astype(BF16),
            rwkv=_rwkv_params(mu_shift[l], w0[l], w_up[l], a0[l], a_up[l], g_up[l], k_k[l], k_a[l], r_k[l],
                              lnx_g[l], lnx_b[l]),
            sb_bias=sb_bias[l], ln1_g=ln1_g[l], ln1_b=ln1_b[l], ln2_g=ln2_g[l], ln2_b=ln2_b[l],
            sub_keys=_sub_key_pairs(sub_keys[l]), peer_u=_pack_table(peer_u[l]), peer_v=_pack_table(peer_v[l]))
        mod = _cond(c_all, w_cond[l], b_cond[l])
        mod_p = jnp.split(mod[:bp], N_MOD, axis=-1)
        mod_s = jnp.split(mod[bp:bp + bs], N_MOD, axis=-1)
        res = _layer_pair(yp, ys, mod_p, mod_s, cache_k[l], cache_v[l], page_table, state_wkv[l], state_shift[l], wts)
        yp, ys = res[0], res[1]
        for acc, val in zip(outs, res[2:]):
            acc.append(val)
    return (yp, ys) + tuple(jnp.stack(o) for o in outs)
```

```python
import functools

import jax
import jax.numpy as jnp
from jax import lax
from jax.experimental import pallas as pl
from jax.experimental.pallas import tpu as pltpu

F32 = jnp.float32
BF16 = jnp.bfloat16

D_MODEL = 1024
HEAD_DIM = 64
N_HEADS = 8
WIDTH = N_HEADS * HEAD_DIM
W_LORA, A_LORA, G_LORA = 64, 64, 128
RWKV_PROJ = 3 * WIDTH + W_LORA + A_LORA + G_LORA
IN_COLS = RWKV_PROJ + 3 * WIDTH
GN_EPS = HEAD_DIM * 1e-5
LN_EPS = 1e-5
SB_SCALE = HEAD_DIM ** -0.5
PAGE_SIZE = 128
PEER_HEADS = 8
N_KEYS = 128
PEER_TOPK = 16
PEER_SLOTS = PEER_HEADS * PEER_TOPK
N_MOD = 6
DEPTH = 1
DN_ALPHA = (2 * DEPTH) ** 0.25

VMEM_LIMIT = 56 * 1024 * 1024


def _cparams(*sem):
    return pltpu.CompilerParams(dimension_semantics=sem, vmem_limit_bytes=VMEM_LIMIT)


def _ln_rows(x):
    mu = jnp.mean(x, axis=-1, keepdims=True)
    xc = x - mu
    var = jnp.mean(xc * xc, axis=-1, keepdims=True)
    return xc * lax.rsqrt(var + LN_EPS)


def _split_bf16(x):
    hi = x.astype(BF16)
    lo = (x - hi.astype(F32)).astype(BF16)
    return hi, lo


def _dot(a, b):
    return jnp.dot(a, b, preferred_element_type=F32)


def _dot_nt(a, b):
    return lax.dot_general(a, b, (((1,), (1,)), ((), ())), preferred_element_type=F32)


def _dot2(x, w_bf16):
    hi, lo = _split_bf16(x)
    return _dot(hi, w_bf16) + _dot(lo, w_bf16)


def _cond_kernel(c_ref, w_ref, b_ref, o_ref):
    c = c_ref[...]
    s = c * jax.nn.sigmoid(c)
    o_ref[...] = jnp.dot(s, w_ref[...], preferred_element_type=F32,
                         precision=lax.Precision.HIGHEST) + b_ref[...]


def _cond(c, w_cond, b_cond):
    n, d = c.shape
    cols = w_cond.shape[1]
    bn = 1024
    return pl.pallas_call(
        _cond_kernel,
        grid=(cols // bn,),
        in_specs=[pl.BlockSpec((n, d), lambda j: (0, 0)),
                  pl.BlockSpec((d, bn), lambda j: (0, j)),
                  pl.BlockSpec((1, bn), lambda j: (0, j))],
        out_specs=pl.BlockSpec((n, bn), lambda j: (0, j)),
        out_shape=jax.ShapeDtypeStruct((n, cols), F32),
        compiler_params=_cparams("parallel"),
        name="cond",
    )(c, w_cond, b_cond.reshape(1, cols))


def _inproj_kernel(x_ref, sh_ref, sc_ref, w_ref, p_ref, q_ref, k_ref, v_ref, kb_ref, vb_ref):
    h = _ln_rows(x_ref[...]) * (1.0 + sc_ref[...]) + sh_ref[...]
    hb = h.astype(BF16)
    p_ref[...] = _dot(hb, w_ref[:, :RWKV_PROJ])
    q_ref[...] = _dot(hb, w_ref[:, RWKV_PROJ:RWKV_PROJ + WIDTH]).astype(BF16)
    k = _dot(hb, w_ref[:, RWKV_PROJ + WIDTH:RWKV_PROJ + 2 * WIDTH])
    v = _dot(hb, w_ref[:, RWKV_PROJ + 2 * WIDTH:])
    k_ref[...] = k
    v_ref[...] = v
    kb_ref[...] = k.astype(BF16)
    vb_ref[...] = v.astype(BF16)


def _inproj(x, shift, scale, w_in_bf16, rows_per_mod, block_rows):
    n, d = x.shape
    rb = block_rows
    shift, mod_spec = _mod_spec(shift, rows_per_mod, rb, d)
    scale, _ = _mod_spec(scale, rows_per_mod, rb, d)
    row = lambda c: pl.BlockSpec((rb, c), lambda i: (i, 0))
    return pl.pallas_call(
        _inproj_kernel,
        grid=(n // rb,),
        in_specs=[row(d), mod_spec, mod_spec,
                  pl.BlockSpec((d, IN_COLS), lambda i: (0, 0))],
        out_specs=[row(RWKV_PROJ), row(WIDTH), row(WIDTH), row(WIDTH), row(WIDTH), row(WIDTH)],
        out_shape=[jax.ShapeDtypeStruct((n, RWKV_PROJ), F32),
                   jax.ShapeDtypeStruct((n, WIDTH), BF16),
                   jax.ShapeDtypeStruct((n, WIDTH), F32),
                   jax.ShapeDtypeStruct((n, WIDTH), F32),
                   jax.ShapeDtypeStruct((n, WIDTH), BF16),
                   jax.ShapeDtypeStruct((n, WIDTH), BF16)],
        compiler_params=_cparams("parallel"),
        name="inproj",
    )(x, shift, scale, w_in_bf16)


def _seg_ones(n):
    i = jnp.arange(n) // HEAD_DIM
    return (i[:, None] == i[None, :]).astype(BF16)


def _softplus(u):
    return jnp.maximum(u, 0.0) + jnp.log(1.0 + jnp.exp(-jnp.abs(u)))


def _rwkv_pre_kernel(has_prev, chunked, p_ref, prev_ref, mu_ref, w0_ref, wup_ref, a0_ref, aup_ref, gup_ref,
                     kk_ref, ka_ref, rk_ref, seg_ref, *refs):
    out_refs, carry_ref = refs[:-1], refs[-1]
    pf = p_ref[...]
    if has_prev:
        prev = prev_ref[...]
    else:
        tb = pl.program_id(1)
        first = jnp.where(tb == 0, prev_ref[...], carry_ref[...])
        rows = lax.broadcasted_iota(jnp.int32, pf.shape, 0)
        prev = jnp.where(rows == 0, first, pltpu.roll(pf, 1, axis=0))
        carry_ref[...] = pf[pf.shape[0] - 1:, :]
    pm = pf + (prev - pf) * mu_ref[...]
    r = pm[:, :WIDTH]
    k = pm[:, WIDTH:2 * WIDTH]
    v = pm[:, 2 * WIDTH:3 * WIDTH]
    dwa = pm[:, 3 * WIDTH:3 * WIDTH + W_LORA + A_LORA]
    dg = pm[:, 3 * WIDTH + W_LORA + A_LORA:]
    seg = seg_ref[...]
    w = -_softplus(-(w0_ref[...] + _dot3(jnp.tanh(dwa), wup_ref[...], _NN))) - 0.5
    log_dec = -jnp.exp(w)
    a = jax.nn.sigmoid(a0_ref[...] + _dot3(dwa, aup_ref[...], _NN))
    g = _dot3(jax.nn.sigmoid(dg), gup_ref[...], _NN)
    kkr = k * kk_ref[...]
    kk = kkr * lax.rsqrt(jnp.maximum(_dot2(kkr * kkr, seg), 1e-24))
    km = k * (1.0 + (a - 1.0) * ka_ref[...])
    b = kk * a
    bonus = _dot2(r * km * rk_ref[...], seg) * v
    if chunked:
        outs = (kk, log_dec, b, km, v, r, g, bonus)
    else:
        dec = jnp.exp(log_dec)
        outs = (kk, dec, b, km, v, dec * r, _dot2(b * r, seg), _dot2(km * r, seg), g, bonus)
    for ref, val in zip(out_refs, outs):
        ref[...] = val


def _rwkv_pre(p, prev, has_prev, chunked, prm, tb):
    bsz, t, _ = p.shape
    n_out = 8 if chunked else 10
    blk = lambda c: pl.BlockSpec((None, tb, c), lambda i, j: (i, j, 0))
    full = lambda a: pl.BlockSpec(a.shape, lambda i, j: (0,) * a.ndim)
    prev_spec = blk(RWKV_PROJ) if has_prev else pl.BlockSpec((None, 1, RWKV_PROJ), lambda i, j: (i, 0, 0))
    params = [prm["mu"], prm["w0"], prm["wup"], prm["a0"], prm["aup"], prm["gup"],
              prm["k_k"], prm["k_a"], prm["r_k"], prm["seg512"]]
    return pl.pallas_call(
        functools.partial(_rwkv_pre_kernel, has_prev, chunked),
        grid=(bsz, t // tb),
        in_specs=[blk(RWKV_PROJ), prev_spec] + [full(a) for a in params],
        out_specs=[blk(WIDTH)] * n_out,
        out_shape=[jax.ShapeDtypeStruct((bsz, t, WIDTH), F32)] * n_out,
        scratch_shapes=[pltpu.VMEM((1, RWKV_PROJ), F32)],
        compiler_params=_cparams("parallel", "arbitrary"),
        name="rwkv_pre",
    )(p, prev, *params)


def _rwkv_scan_kernel(nb_count, tb, kk_ref, dec_ref, b_ref, km_ref, v_ref, wr_ref, br_ref, kr_ref,
                      g_ref, bon_ref, s0_ref, lg_ref, lb_ref, seg256_ref, seg512_ref,
                      out_ref, st_ref, s_scr, y_scr):
    step_blk = pl.program_id(1)

    @pl.when(step_blk == 0)
    def _():
        s_scr[...] = s0_ref[...]

    shape = (HEAD_DIM, WIDTH)
    ident = (lax.broadcasted_iota(jnp.int32, shape, 1) & (HEAD_DIM - 1)) == lax.broadcasted_iota(jnp.int32, shape, 0)
    seg = seg256_ref[...]
    half = WIDTH // 2

    def segsum(lhs):
        return jnp.concatenate([_dot(lhs[:, :half], seg), _dot(lhs[:, half:], seg)], axis=1)

    def step(t, carry):
        for nb in range(nb_count):
            row = lambda ref: ref[nb, pl.ds(t, 1), :]
            s = s_scr[nb]
            p_hi, p_lo = _split_bf16(s * row(kk_ref))
            dv = jnp.where(ident, row(v_ref), 0.0).astype(BF16)
            pr = (s * row(wr_ref)).astype(BF16)
            res = segsum(jnp.concatenate([p_hi, p_lo, dv, pr], axis=0))
            sa = res[:HEAD_DIM] + res[HEAD_DIM:2 * HEAD_DIM]
            vcol = res[2 * HEAD_DIM:3 * HEAD_DIM]
            ycol = res[3 * HEAD_DIM:] - sa * row(br_ref)
            s_scr[nb] = s * row(dec_ref) - sa * row(b_ref) + vcol * row(km_ref)
            y_scr[nb, pl.ds(t, 1), :] = (jnp.sum(jnp.where(ident, ycol, 0.0), axis=0, keepdims=True)
                                         + row(v_ref) * row(kr_ref))
        return carry

    lax.fori_loop(0, tb, step, 0)

    seg512 = seg512_ref[...]
    for nb in range(nb_count):
        y = y_scr[nb]
        mu = _dot2(y, seg512) * (1.0 / HEAD_DIM)
        yc = y - mu
        var = _dot2(yc * yc, seg512) * (1.0 / HEAD_DIM)
        yn = yc * lax.rsqrt(var + GN_EPS) * lg_ref[...] + lb_ref[...]
        out_ref[nb] = (yn + bon_ref[nb]) * g_ref[nb]

    @pl.when(step_blk == pl.num_programs(1) - 1)
    def _():
        st_ref[...] = s_scr[...]


def _rwkv_scan(pre, s0, prm, nb, tb):
    kk, dec, b, km, v, wr, br, kr, g, bon = pre
    bsz, t, _ = kk.shape
    blk = pl.BlockSpec((nb, tb, WIDTH), lambda i, j: (i, j, 0))
    sblk = pl.BlockSpec((nb, HEAD_DIM, WIDTH), lambda i, j: (i, 0, 0))
    full = lambda a: pl.BlockSpec(a.shape, lambda i, j: (0,) * a.ndim)
    params = [prm["lnx_g"], prm["lnx_b"], prm["seg256"], prm["seg512"]]
    return pl.pallas_call(
        functools.partial(_rwkv_scan_kernel, nb, tb),
        grid=(bsz // nb, t // tb),
        in_specs=[blk] * 10 + [sblk] + [full(a) for a in params],
        out_specs=[blk, sblk],
        out_shape=[jax.ShapeDtypeStruct((bsz, t, WIDTH), F32),
                   jax.ShapeDtypeStruct((bsz, HEAD_DIM, WIDTH), F32)],
        scratch_shapes=[pltpu.VMEM((nb, HEAD_DIM, WIDTH), F32), pltpu.VMEM((nb, tb, WIDTH), F32)],
        compiler_params=_cparams("parallel", "arbitrary"),
        name="rwkv_scan",
    )(kk, dec, b, km, v, wr, br, kr, g, bon, s0, *params)


def _rwkv_params(mu_shift, w0, w_up, a0, a_up, g_up, k_k, k_a, r_k, lnx_g, lnx_b):
    row = lambda a: a.reshape(1, -1).astype(F32)
    zeros = jnp.zeros((A_LORA, WIDTH), F32)
    return dict(mu=row(mu_shift), w0=row(w0), a0=row(a0), k_k=row(k_k), k_a=row(k_a), r_k=row(r_k),
                lnx_g=row(lnx_g), lnx_b=row(lnx_b), gup=g_up,
                wup=jnp.concatenate([w_up, zeros], axis=0), aup=jnp.concatenate([zeros, a_up], axis=0),
                seg256=_seg_ones(WIDTH // 2), seg512=_seg_ones(WIDTH))


def _state_to_rows(wkv):
    bsz = wkv.shape[0]
    return wkv.transpose(0, 2, 1, 3).reshape(bsz, HEAD_DIM, WIDTH)


def _rows_to_state(s):
    bsz = s.shape[0]
    return s.reshape(bsz, HEAD_DIM, N_HEADS, HEAD_DIM).transpose(0, 2, 1, 3)


CHUNK = 64
PREP_CHUNKS = 4
PAIR = 2 * HEAD_DIM
N_PAIRS = WIDTH // PAIR
_NN = ((1,), (0,))
_NT = ((1,), (1,))
_TN = ((0,), (0,))


def _dot3(a, b, dims):
    ah, al = _split_bf16(a)
    bh, bl = _split_bf16(b)
    dg = lambda x, y: lax.dot_general(x, y, (dims, ((), ())), preferred_element_type=F32)
    return dg(ah, bh) + dg(ah, bl) + dg(al, bh)


def _dot1(a, b, dims):
    return lax.dot_general(a.astype(BF16), b.astype(BF16), (dims, ((), ())), preferred_element_type=F32)


def _by_head(x, first):
    zero = jnp.zeros_like(x)
    return jnp.concatenate([jnp.where(first, x, zero), jnp.where(first, zero, x)], axis=0)


def _fold_heads(x):
    c = x.shape[0] // 2
    return x[:c] + x[c:]


def _rwkv_chunk_prep_kernel(kk_ref, ld_ref, b_ref, km_ref, v_ref, r_ref, tri_ref,
                            w1_ref, w2_ref, qr_ref, arb_ref, y0_ref, bd_ref, sadd_ref, gc_ref):
    c = CHUNK
    tri = tri_ref[...]
    first = lax.broadcasted_iota(jnp.int32, (1, PAIR), 1) < HEAD_DIM
    rows = lax.broadcasted_iota(jnp.int32, (PAIR, PAIR), 0)
    cols = lax.broadcasted_iota(jnp.int32, (PAIR, PAIR), 1)
    same = (rows // c) == (cols // c)
    strict = same & (cols < rows)
    incl = same & (cols <= rows)
    eye = jnp.where(rows == cols, 1.0, 0.0)

    units = []
    for cc in range(kk_ref.shape[0] // c):
        rs = slice(cc * c, (cc + 1) * c)
        ld = ld_ref[rs, :]
        ld_hi, ld_lo = _split_bf16(ld)
        cum = _dot(tri, ld_hi) + _dot(tri, ld_lo)
        last = cum[c - 1:, :]
        g_end = jnp.exp(last - cum)
        g_inv = jnp.exp(-cum)
        qa = kk_ref[rs, :] * jnp.exp(cum - ld)
        qr = r_ref[rs, :] * jnp.exp(cum)
        kb, kt = b_ref[rs, :] * g_inv, km_ref[rs, :] * g_inv
        bd, kd = b_ref[rs, :] * g_end, km_ref[rs, :] * g_end
        v = v_ref[rs, :]
        qr_ref[rs, :] = qr
        bd_ref[rs, :] = bd
        gc_ref[cc] = jnp.exp(last)
        for p in range(N_PAIRS):
            sl = slice(p * PAIR, (p + 1) * PAIR)
            units.append(dict(cc=cc, p=p, rs=rs, sl=sl, v=v[:, sl], kd=kd[:, sl],
                              **{k: _by_head(x[:, sl], first)
                                 for k, x in dict(qa2=qa, qr2=qr, kb2=kb, kt2=kt, v2=v).items()}))
    n = [-jnp.where(strict, _dot3(u["qa2"], u["kb2"], _NT), 0.0) for u in units]
    aak = [jnp.where(strict, _dot3(u["qa2"], u["kt2"], _NT), 0.0) for u in units]
    for u in units:
        arb_ref[u["rs"], u["sl"]] = _fold_heads(jnp.where(incl, _dot1(u["qr2"], u["kb2"], _NT), 0.0))
        ark = jnp.where(incl, _dot1(u["qr2"], u["kt2"], _NT), 0.0)
        y0_ref[u["rs"], u["sl"]] = _fold_heads(_dot1(ark, u["v2"], _NN))
        sadd_ref[u["cc"], u["p"]] = jnp.where(same, _dot3(u["v"], u["kd"], _TN), 0.0)
    t_inv = [eye + x for x in n]
    power = n
    for _ in range(c.bit_length() - 2):
        power = [_dot(x.astype(BF16), x.astype(BF16)) for x in power]
        t_inv = [t + _dot(t.astype(BF16), x.astype(BF16)) for t, x in zip(t_inv, power)]
    rhs = [_dot3(a, u["v2"], _NN) for a, u in zip(aak, units)]
    for t, r, u in zip(t_inv, rhs, units):
        w1_ref[u["rs"], u["sl"]] = _fold_heads(_dot3(t, u["qa2"], _NN))
        w2_ref[u["rs"], u["sl"]] = _fold_heads(_dot3(t, r, _NN))


def _rwkv_chunk_scan_kernel(nb_count, w1_ref, w2_ref, qr_ref, arb_ref, y0_ref, bd_ref, sadd_ref, gc_ref,
                            g_ref, bon_ref, s0_ref, lg_ref, lb_ref, seg512_ref, out_ref, st_ref, sx_scr):
    c = CHUNK
    chunk_id = pl.program_id(1)
    first = lax.broadcasted_iota(jnp.int32, (1, PAIR), 1) < HEAD_DIM
    rows = lax.broadcasted_iota(jnp.int32, (PAIR, PAIR), 0)
    cols = lax.broadcasted_iota(jnp.int32, (PAIR, PAIR), 1)
    same = (rows // HEAD_DIM) == (cols // HEAD_DIM)

    @pl.when(chunk_id == 0)
    def _():
        for nb in range(nb_count):
            for p in range(N_PAIRS):
                sx_scr[nb, p] = _by_head(s0_ref[nb, :, p * PAIR:(p + 1) * PAIR], first)

    seg512 = seg512_ref[...]
    chains = [(nb, p, slice(p * PAIR, (p + 1) * PAIR)) for nb in range(nb_count) for p in range(N_PAIRS)]
    proj = [_dot3(jnp.concatenate([w1_ref[nb, :, sl], qr_ref[nb, :, sl]], axis=0), sx_scr[nb, p], _NT)
            for nb, p, sl in chains]
    us = [proj[i][:c] + w2_ref[nb, :, sl] for i, (nb, p, sl) in enumerate(chains)]
    ys = [proj[i][c:] - _dot1(arb_ref[nb, :, sl], _by_head(us[i], first), _NN) + y0_ref[nb, :, sl]
          for i, (nb, p, sl) in enumerate(chains)]
    for i, (nb, p, sl) in enumerate(chains):
        upd = jnp.where(same, _dot3(us[i], bd_ref[nb, :, sl], _TN), 0.0)
        sx_scr[nb, p] = sx_scr[nb, p] * gc_ref[nb, :, sl] - upd + sadd_ref[nb, p]
    for nb in range(nb_count):
        y = jnp.concatenate(ys[nb * N_PAIRS:(nb + 1) * N_PAIRS], axis=1)
        mu = _dot2(y, seg512) * (1.0 / HEAD_DIM)
        yc = y - mu
        var = _dot2(yc * yc, seg512) * (1.0 / HEAD_DIM)
        yn = yc * lax.rsqrt(var + GN_EPS) * lg_ref[...] + lb_ref[...]
        out_ref[nb] = (yn + bon_ref[nb]) * g_ref[nb]

    @pl.when(chunk_id == pl.num_programs(1) - 1)
    def _():
        for nb in range(nb_count):
            st_ref[nb] = jnp.concatenate([_fold_heads(sx_scr[nb, p]) for p in range(N_PAIRS)], axis=1)


def _rwkv_chunked(pre, s0, prm, nb):
    kk, ld, b, km, v, r, g, bon = pre
    bsz, t, _ = kk.shape
    n_chunks = t // CHUNK
    idx = jnp.arange(CHUNK)
    tri = (idx[None, :] <= idx[:, None]).astype(BF16)
    cps = PREP_CHUNKS
    blk = pl.BlockSpec((None, cps * CHUNK, WIDTH), lambda i, j: (i, j, 0))
    wide = jax.ShapeDtypeStruct((bsz, t, WIDTH), F32)
    w1, w2, qr, arb, y0, bd, sadd, gc = pl.pallas_call(
        _rwkv_chunk_prep_kernel,
        grid=(bsz, n_chunks // cps),
        in_specs=[blk] * 6 + [pl.BlockSpec((CHUNK, CHUNK), lambda i, j: (0, 0))],
        out_specs=[blk] * 6 + [pl.BlockSpec((None, cps, N_PAIRS, PAIR, PAIR), lambda i, j: (i, j, 0, 0, 0)),
                               pl.BlockSpec((None, cps, 1, WIDTH), lambda i, j: (i, j, 0, 0))],
        out_shape=[wide] * 6 + [jax.ShapeDtypeStruct((bsz, n_chunks, N_PAIRS, PAIR, PAIR), F32),
                                jax.ShapeDtypeStruct((bsz, n_chunks, 1, WIDTH), F32)],
        compiler_params=_cparams("parallel", "parallel"),
        name="rwkv_chunk_prep",
    )(kk, ld, b, km, v, r, tri)
    nblk = pl.BlockSpec((nb, CHUNK, WIDTH), lambda i, j: (i, j, 0))
    sblk = pl.BlockSpec((nb, HEAD_DIM, WIDTH), lambda i, j: (i, 0, 0))
    full = lambda a: pl.BlockSpec(a.shape, lambda i, j: (0,) * a.ndim)
    params = [prm["lnx_g"], prm["lnx_b"], prm["seg512"]]
    return pl.pallas_call(
        functools.partial(_rwkv_chunk_scan_kernel, nb),
        grid=(bsz // nb, n_chunks),
        in_specs=[nblk] * 6
        + [pl.BlockSpec((nb, None, N_PAIRS, PAIR, PAIR), lambda i, j: (i, j, 0, 0, 0)),
           pl.BlockSpec((nb, None, 1, WIDTH), lambda i, j: (i, j, 0, 0)),
           nblk, nblk, sblk] + [full(a) for a in params],
        out_specs=[nblk, sblk],
        out_shape=[wide, jax.ShapeDtypeStruct((bsz, HEAD_DIM, WIDTH), F32)],
        scratch_shapes=[pltpu.VMEM((nb, N_PAIRS, PAIR, PAIR), F32)],
        compiler_params=_cparams("parallel", "arbitrary"),
        name="rwkv_chunk_scan",
    )(w1, w2, qr, arb, y0, bd, sadd, gc, g, bon, s0, *params)


def _rwkv(p, shift_prev, wkv0, prm, nb, tb):
    bsz, t, _ = p.shape
    s0 = _state_to_rows(wkv0)
    if t == 1:
        pre = _rwkv_pre(p.reshape(1, bsz, RWKV_PROJ), shift_prev.reshape(1, bsz, RWKV_PROJ), True, False, prm, bsz)
        out, st = _rwkv_scan([a.reshape(bsz, 1, WIDTH) for a in pre], s0, prm, nb, 1)
    else:
        pre = _rwkv_pre(p, shift_prev.reshape(bsz, 1, RWKV_PROJ), False, True, prm, tb)
        out, st = _rwkv_chunked(pre, s0, prm, nb)
    return out, _rows_to_state(st)


def _sb_tile(z, mask, carry, neg_ge, neg_ones):
    sp = _softplus(z)
    if mask is not None:
        sp = jnp.where(mask, sp, 0.0)
    spb = sp.astype(BF16)
    inc = _dot(spb, neg_ge)
    bk = z.shape[1]
    a = jnp.exp(z + inc + jnp.concatenate([carry] * (bk // carry.shape[1]), axis=1))
    if mask is not None:
        a = jnp.where(mask, a, 0.0)
    if neg_ones is None:
        return a.astype(BF16), carry + jnp.broadcast_to(inc[:, :1], carry.shape)
    return a.astype(BF16), carry + _dot(spb, neg_ones)


def _sb_prompt_kernel(bq, bk, bias_ref, q_ref, k_ref, v_ref, mge_ref, o_ref):
    hp = pl.program_id(1)
    qi = pl.program_id(2)
    lane = lax.broadcasted_iota(jnp.int32, (1, 2 * HEAD_DIM), 1)
    first = lane < HEAD_DIM
    q2 = q_ref[...] * jnp.asarray(SB_SCALE, BF16)
    zero = jnp.zeros_like(q2)
    q_heads = (jnp.where(first, q2, zero), jnp.where(first, zero, q2))
    biases = (bias_ref[2 * hp], bias_ref[2 * hp + 1])
    m_ge = mge_ref[...]
    per_q = bq // bk
    rows = lax.broadcasted_iota(jnp.int32, (bq, bk), 0)
    cols = lax.broadcasted_iota(jnp.int32, (bq, bk), 1)

    def tile(j, mask, state):
        acc, carries = state
        kblk = k_ref[pl.ds(pl.multiple_of(j * bk, bk), bk), :]
        vblk = v_ref[pl.ds(pl.multiple_of(j * bk, bk), bk), :]
        vzero = jnp.zeros_like(vblk)
        v_heads = (jnp.where(first, vblk, vzero), jnp.where(first, vzero, vblk))
        new_carries = []
        for e in range(2):
            z = _dot_nt(q_heads[e], kblk) + biases[e]
            a, c = _sb_tile(z, mask, carries[e], m_ge, None)
            acc = acc + _dot(a, v_heads[e])
            new_carries.append(c)
        return acc, tuple(new_carries)

    zc = jnp.zeros((bq, 2 * HEAD_DIM), F32)
    state = (zc, (zc, zc))
    for u in reversed(range(per_q)):
        state = tile(qi * per_q + u, cols + u * bk < rows, state)
    n_full = qi * per_q

    def full_tiles(i, s):
        for u in range(per_q):
            s = tile(n_full - 1 - i * per_q - u, None, s)
        return s

    state = lax.fori_loop(0, qi, full_tiles, state)
    o_ref[...] = state[0]


def _sb_prompt(q, k, v, sb_bias, bq, bk):
    bsz, t, _ = q.shape
    pair = 2 * HEAD_DIM
    idx = jnp.arange(bk)
    m_ge = -(idx[:, None] >= idx[None, :]).astype(BF16)
    return pl.pallas_call(
        functools.partial(_sb_prompt_kernel, bq, bk),
        grid=(bsz, WIDTH // pair, t // bq),
        in_specs=[pl.BlockSpec(memory_space=pltpu.SMEM),
                  pl.BlockSpec((None, bq, pair), lambda b, h, i: (b, i, h)),
                  pl.BlockSpec((None, t, pair), lambda b, h, i: (b, 0, h)),
                  pl.BlockSpec((None, t, pair), lambda b, h, i: (b, 0, h)),
                  pl.BlockSpec((bk, bk), lambda b, h, i: (0, 0))],
        out_specs=pl.BlockSpec((None, bq, pair), lambda b, h, i: (b, i, h)),
        out_shape=jax.ShapeDtypeStruct((bsz, t, WIDTH), F32),
        compiler_params=_cparams("parallel", "parallel", "arbitrary"),
        name="sb_prompt",
    )(sb_bias.astype(F32), q, k, v, m_ge)


def _sb_sample_kernel(pp, n_pages, pt_ref, q_ref, knew_ref, vnew_ref, bias_ref, mge_ref, ones_ref, *refs):
    k_refs, v_refs = refs[:pp], refs[pp:2 * pp]
    o_ref, carry_scr, acc_scr = refs[2 * pp:]
    g = pl.program_id(1)

    @pl.when(g == 0)
    def _():
        carry_scr[...] = jnp.zeros_like(carry_scr)
        acc_scr[...] = jnp.zeros_like(acc_scr)

    shape = (N_HEADS, WIDTH)
    own = (lax.broadcasted_iota(jnp.int32, shape, 1) // HEAD_DIM) == lax.broadcasted_iota(jnp.int32, shape, 0)
    q = q_ref[...]
    qb = jnp.where(own, jnp.concatenate([q * SB_SCALE] * N_HEADS, axis=1), 0.0).astype(BF16)
    bias = bias_ref[...]
    zs = [_dot(qb, k_refs[u][...].astype(BF16)) + bias for u in range(pp)]
    z_all = jnp.concatenate(zs, axis=1)
    spb = _softplus(z_all).astype(BF16)
    incs, carries = [], []
    carry = carry_scr[...]
    for u in range(pp):
        page = spb[:, u * PAGE_SIZE:(u + 1) * PAGE_SIZE]
        incs.append(_dot(page, mge_ref[...]))
        carries.append(carry)
        carry = carry + _dot(page, ones_ref[...])
    carry_scr[...] = carry
    a_all = jnp.exp(z_all + jnp.concatenate(incs, axis=1) + jnp.concatenate(carries, axis=1)).astype(BF16)
    acc = acc_scr[...]
    for u in range(pp):
        acc = acc + _dot_nt(a_all[:, u * PAGE_SIZE:(u + 1) * PAGE_SIZE], v_refs[u][...].astype(BF16))
    acc_scr[...] = acc

    @pl.when(g == pl.num_programs(1) - 1)
    def _():
        head_row = lax.broadcasted_iota(jnp.int32, (N_HEADS, HEAD_DIM), 0)
        out = jnp.zeros((N_HEADS, HEAD_DIM), F32)
        for h in range(N_HEADS):
            out = jnp.where(head_row == h, acc[:, h * HEAD_DIM:(h + 1) * HEAD_DIM], out)
        past = n_pages * PAGE_SIZE
        z_new = jnp.sum(q * knew_ref[...], axis=1, keepdims=True) * SB_SCALE + bias[:, :1]
        a_new = jnp.where(past < past, jnp.exp(-_softplus(-z_new)), 0.0)
        o_ref[...] = out + a_new * vnew_ref[...]


def _sb_sample(q, k_new, v_new, sb_bias, cache_k, cache_v, page_table, pp):
    bsz = q.shape[0]
    n_pages = page_table.shape[1]
    idx = jnp.arange(PAGE_SIZE)
    m_ge = -(idx[:, None] >= idx[None, :]).astype(BF16)
    ones_cols = -jnp.ones((PAGE_SIZE, PAGE_SIZE), BF16)
    bias = jnp.broadcast_to(sb_bias.astype(F32)[:, None], (N_HEADS, PAGE_SIZE))
    rows = lambda c: c.transpose(0, 2, 3, 1).reshape(c.shape[0], WIDTH, PAGE_SIZE)
    row = pl.BlockSpec((None, N_HEADS, HEAD_DIM), lambda b, g, pt: (b, 0, 0))
    full = lambda a: pl.BlockSpec(a.shape, lambda b, g, pt: (0,) * a.ndim)

    def page_spec(u):
        return pl.BlockSpec((None, WIDTH, PAGE_SIZE),
                            lambda b, g, pt: (pt[b * n_pages + n_pages - 1 - (g * pp + u)], 0, 0))

    grid_spec = pltpu.PrefetchScalarGridSpec(
        num_scalar_prefetch=1,
        grid=(bsz, n_pages // pp),
        in_specs=[row, row, row, full(bias), full(m_ge), full(ones_cols)]
        + [page_spec(u) for u in range(pp)] * 2,
        out_specs=row,
        scratch_shapes=[pltpu.VMEM((N_HEADS, PAGE_SIZE), F32), pltpu.VMEM((N_HEADS, WIDTH), F32)],
    )
    return pl.pallas_call(
        functools.partial(_sb_sample_kernel, pp, n_pages),
        grid_spec=grid_spec,
        out_shape=jax.ShapeDtypeStruct((bsz, N_HEADS, HEAD_DIM), F32),
        compiler_params=_cparams("parallel", "arbitrary"),
        name="sb_sample",
    )(page_table.reshape(-1), q, k_new, v_new, bias, m_ge, ones_cols, *([rows(cache_k)] * pp), *([rows(cache_v)] * pp))


def _outproj_kernel(rw_ref, sb_ref, x_ref, g1_ref, sh2_ref, sc2_ref, wo_ref, lg_ref, lb_ref, wpq_ref,
                    x1_ref, h2_ref, qp_ref):
    mix = _dot(rw_ref[...].astype(BF16), wo_ref[:WIDTH, :]) + _dot(sb_ref[...].astype(BF16), wo_ref[WIDTH:, :])
    x1 = _ln_rows(DN_ALPHA * x_ref[...] + g1_ref[...] * mix) * lg_ref[...] + lb_ref[...]
    h2 = _ln_rows(x1) * (1.0 + sc2_ref[...]) + sh2_ref[...]
    x1_ref[...] = x1
    h2_ref[...] = h2
    qp_ref[...] = _dot(h2.astype(BF16), wpq_ref[...]).astype(BF16)


def _mod_spec(a, rows_per_mod, rb, d):
    if rows_per_mod == 1:
        return a, pl.BlockSpec((rb, d), lambda i: (i, 0))
    per = rows_per_mod // rb
    return a.reshape(-1, 1, d), pl.BlockSpec((None, 1, d), lambda i: (i // per, 0, 0))


def _outproj(rw, sb, x, g1, sh2, sc2, w_out_bf16, ln_g, ln_b, w_pq_bf16, rows_per_mod, rb):
    n, d = x.shape
    row = lambda c: pl.BlockSpec((rb, c), lambda i: (i, 0))
    full = lambda a: pl.BlockSpec(a.shape, lambda i: (0,) * a.ndim)
    mods, mod_specs = zip(*[_mod_spec(a, rows_per_mod, rb, d) for a in (g1, sh2, sc2)])
    lg, lb = ln_g.reshape(1, d), ln_b.reshape(1, d)
    return pl.pallas_call(
        _outproj_kernel,
        grid=(n // rb,),
        in_specs=[row(WIDTH), row(WIDTH), row(d), *mod_specs, full(w_out_bf16), full(lg), full(lb), full(w_pq_bf16)],
        out_specs=[row(d), row(d), row(d)],
        out_shape=[jax.ShapeDtypeStruct((n, d), F32), jax.ShapeDtypeStruct((n, d), F32),
                   jax.ShapeDtypeStruct((n, d), BF16)],
        compiler_params=_cparams("parallel"),
        name="outproj",
    )(rw, sb, x, *mods, w_out_bf16, lg, lb, w_pq_bf16)


def _take_top(x, ids, payload, count):
    vals, picked = [], []
    for _ in range(count):
        m = jnp.max(x, axis=0, keepdims=True)
        pos = jnp.min(jnp.where(x == m, ids, jnp.finfo(F32).max), axis=0, keepdims=True)
        hit = ids == pos
        vals.append(m)
        picked.append(pos if payload is None else jnp.sum(jnp.where(hit, payload, 0.0), axis=0, keepdims=True))
        x = jnp.where(hit, -jnp.inf, x)
    return jnp.concatenate(vals, axis=0), jnp.concatenate(picked, axis=0)


def _pair_candidates(sv0, si0, sv1, si1):
    k = PEER_TOPK
    tokens = sv0.shape[1]
    wide = 4
    vals, ids, experts = [], [], []
    for a in range(wide):
        n = -(-(k // (a + 1)) // 8) * 8
        b_ids = lax.broadcasted_iota(jnp.int32, (n, tokens), 0)
        valid = b_ids < k // (a + 1)
        vals.append(jnp.where(valid, sv0[a:a + 1] + sv1[:n], -jnp.inf))
        ids.append(jnp.where(valid, a * k + b_ids, -1).astype(F32))
        experts.append(si0[a:a + 1] * N_KEYS + si1[:n])
    for b in range(k // (wide + 1)):
        last_a = k // (b + 1) - 1
        n = -(-(last_a + 1) // 8) * 8
        a_ids = lax.broadcasted_iota(jnp.int32, (n, tokens), 0)
        valid = (a_ids >= wide) & (a_ids <= last_a)
        vals.append(jnp.where(valid, sv0[:n] + sv1[b:b + 1], -jnp.inf))
        ids.append(jnp.where(valid, a_ids * k + b, -1).astype(F32))
        experts.append(si0[:n] * N_KEYS + si1[b:b + 1])
    return jnp.concatenate(vals, axis=0), jnp.concatenate(ids, axis=0), jnp.concatenate(experts, axis=0)


def _route_kernel(qp_ref, sk_ref, e_ref, g_ref):
    tokens = qp_ref.shape[0]
    scores = _dot_nt(sk_ref[...], qp_ref[...])
    key_ids = lax.broadcasted_iota(jnp.int32, (N_KEYS, tokens), 0).astype(F32)
    sv0, si0 = _take_top(scores[:N_KEYS], key_ids, None, PEER_TOPK)
    sv1, si1 = _take_top(scores[N_KEYS:], key_ids, None, PEER_TOPK)
    cand, cand_ids, cidx = _pair_candidates(sv0, si0, sv1, si1)
    top, eidx = _take_top(cand, cand_ids, cidx, PEER_TOPK)
    ex = jnp.exp(top - top[:1])
    e_ref[...] = eidx.astype(jnp.int32)
    g_ref[...] = ex / jnp.sum(ex, axis=0, keepdims=True)


def _route(qp, sk_pairs, tb):
    n = qp.shape[0]
    blk = pl.BlockSpec((None, PEER_TOPK, tb), lambda i, h: (h, 0, i))
    return pl.pallas_call(
        _route_kernel,
        grid=(n // tb, PEER_HEADS),
        in_specs=[pl.BlockSpec((tb, 2 * HEAD_DIM), lambda i, h: (i, h)),
                  pl.BlockSpec((None, 2 * N_KEYS, 2 * HEAD_DIM), lambda i, h: (h, 0, 0))],
        out_specs=[blk, blk],
        out_shape=[jax.ShapeDtypeStruct((PEER_HEADS, PEER_TOPK, n), jnp.int32),
                   jax.ShapeDtypeStruct((PEER_HEADS, PEER_TOPK, n), F32)],
        compiler_params=_cparams("parallel", "parallel"),
        name="peer_route",
    )(qp, sk_pairs)


def _sub_key_pairs(sub_keys):
    z = jnp.zeros_like(sub_keys[:, 0])
    top = jnp.concatenate([sub_keys[:, 0], z], axis=-1)
    bot = jnp.concatenate([z, sub_keys[:, 1]], axis=-1)
    return jnp.concatenate([top, bot], axis=1).astype(BF16)


ROW_WORDS = D_MODEL // 2
ROW_PLANES = ROW_WORDS // 128
PLANE_STRIDE = PEER_SLOTS + 8


def _pack_table(w):
    bits = lax.bitcast_convert_type(w.astype(BF16), jnp.uint16).astype(jnp.uint32)
    packed = bits[:, :ROW_WORDS] | (bits[:, ROW_WORDS:] << 16)
    return lax.bitcast_convert_type(packed, jnp.int32).reshape(-1, 128)


def _gather_rows(idx_ref, first_token, tbl_ref, buf_ref, bank):
    token_idx = [idx_ref.at[pl.ds((first_token + w) * PEER_SLOTS, PEER_SLOTS)] for w in range(PEER_WIDTH)]
    for m in range(PEER_SLOTS):
        for w in range(PEER_WIDTH):
            start = pl.multiple_of(token_idx[w][m], ROW_PLANES)
            buf_ref[bank * PEER_WIDTH + w, pl.ds(m, ROW_PLANES, stride=PLANE_STRIDE), :] = (
                tbl_ref[pl.ds(start, ROW_PLANES), :])


def _plane_halves(buf_ref, j):
    words = buf_ref[pl.ds(j * PLANE_STRIDE, PEER_SLOTS), :]
    lo = lax.bitcast_convert_type(words << 16, F32)
    hi = lax.bitcast_convert_type(words & jnp.int32(-65536), F32)
    return lo, hi


PEER_WIDTH = 1
PEER_GROUP = 2 * PEER_WIDTH


def _token_groups(tb, idx_ref, tbl_ref, buf_ref, compute, init):
    _gather_rows(idx_ref, 0, tbl_ref, buf_ref, 0)

    def group(i, carry):
        t0 = PEER_GROUP * i
        for bank in range(2):
            nxt = t0 + PEER_WIDTH if bank == 0 else jnp.minimum(t0 + PEER_GROUP, tb - PEER_WIDTH)
            _gather_rows(idx_ref, nxt, tbl_ref, buf_ref, 1 - bank)
            for w in range(PEER_WIDTH):
                pos = bank * PEER_WIDTH + w
                carry = compute(t0 + pos, buf_ref.at[pos], pos, carry)
        return carry

    return lax.fori_loop(0, tb // PEER_GROUP, group, init)


def _peer_act_kernel(tb, idx_ref, h_ref, tbl_ref, act_ref, buf_ref, acc_ref):
    token_lane = lax.broadcasted_iota(jnp.int32, (PEER_SLOTS, tb), 1)
    acc_ref[...] = jnp.zeros_like(acc_ref)

    def finish(t, pos, acts):
        return jnp.where(token_lane == t, jnp.sum(acc_ref[pos], axis=1, keepdims=True), acts)

    def compute(t, buf, pos, acts):
        acts = finish(t - PEER_GROUP, pos, acts)
        hrow = h_ref[pl.ds(t, 1), :]
        acc = None
        for j in range(ROW_PLANES):
            lo, hi = _plane_halves(buf, j)
            term = (lo * hrow[:, j * 128:(j + 1) * 128]
                    + hi * hrow[:, ROW_WORDS + j * 128:ROW_WORDS + (j + 1) * 128])
            acc = term if acc is None else acc + term
        acc_ref[pos] = acc
        return acts

    acts = _token_groups(tb, idx_ref, tbl_ref, buf_ref, compute, jnp.zeros((PEER_SLOTS, tb), F32))
    for pos in range(PEER_GROUP):
        acts = finish(tb - PEER_GROUP + pos, pos, acts)
    act_ref[...] = acts


def _peer_out_kernel(tb, idx_ref, act_ref, gate_ref, tbl_ref, f_ref, buf_ref, coef_ref, col_ref):
    act = act_ref[...]
    coef_ref[...] = gate_ref[...] * (0.5 * act * (1.0 + lax.erf(act * (2.0 ** -0.5))))
    token_lane = lax.broadcasted_iota(jnp.int32, (PEER_SLOTS, tb), 1)

    def stage_coef(t, pos):
        col = jnp.sum(jnp.where(token_lane == t, coef_ref[...], 0.0), axis=1, keepdims=True)
        col_ref[pos] = jnp.broadcast_to(col, (PEER_SLOTS, 128))

    for pos in range(PEER_GROUP):
        stage_coef(pos, pos)

    def compute(t, buf, pos, carry):
        coef = col_ref[pos]
        los, his = [], []
        for j in range(ROW_PLANES):
            lo, hi = _plane_halves(buf, j)
            los.append(jnp.sum(lo * coef, axis=0, keepdims=True))
            his.append(jnp.sum(hi * coef, axis=0, keepdims=True))
        f_ref[pl.ds(t, 1), :] = jnp.concatenate(los + his, axis=1)
        stage_coef(t + PEER_GROUP, pos)
        return carry

    _token_groups(tb, idx_ref, tbl_ref, buf_ref, compute, 0)


def _peer_experts(h2, idx_flat, gate_t, tbl_u, tbl_v, tb):
    n, d = h2.shape
    smem = pl.BlockSpec((tb * PEER_SLOTS,), lambda i: (i,), memory_space=pltpu.SMEM)
    row = pl.BlockSpec((tb, d), lambda i: (i, 0))
    slot_major = pl.BlockSpec((PEER_SLOTS, tb), lambda i: (0, i))
    table = pl.BlockSpec(memory_space=pltpu.VMEM)
    buf = pltpu.VMEM((PEER_GROUP, ROW_PLANES * PLANE_STRIDE, 128), jnp.int32)
    staged = pltpu.VMEM((PEER_GROUP, PEER_SLOTS, 128), F32)
    act_t = pl.pallas_call(
        functools.partial(_peer_act_kernel, tb),
        grid=(n // tb,),
        in_specs=[smem, row, table],
        out_specs=slot_major,
        out_shape=jax.ShapeDtypeStruct((PEER_SLOTS, n), F32),
        scratch_shapes=[buf, staged],
        compiler_params=_cparams("arbitrary"),
        name="peer_act",
    )(idx_flat, h2, tbl_u)
    return pl.pallas_call(
        functools.partial(_peer_out_kernel, tb),
        grid=(n // tb,),
        in_specs=[smem, slot_major, slot_major, table],
        out_specs=row,
        out_shape=jax.ShapeDtypeStruct((n, d), F32),
        scratch_shapes=[buf, pltpu.VMEM((PEER_SLOTS, tb), F32), staged],
        compiler_params=_cparams("arbitrary"),
        name="peer_out",
    )(idx_flat, act_t, gate_t, tbl_v)


def _final_kernel(x1_ref, f_ref, g2_ref, lg_ref, lb_ref, y_ref):
    y_ref[...] = _ln_rows(DN_ALPHA * x1_ref[...] + g2_ref[...] * f_ref[...]) * lg_ref[...] + lb_ref[...]


def _final(x1, f, g2, ln_g, ln_b, rows_per_mod, rb):
    n, d = x1.shape
    row = pl.BlockSpec((rb, d), lambda i: (i, 0))
    g2, g2_spec = _mod_spec(g2, rows_per_mod, rb, d)
    vec = pl.BlockSpec((1, d), lambda i: (0, 0))
    return pl.pallas_call(
        _final_kernel,
        grid=(n // rb,),
        in_specs=[row, row, g2_spec, vec, vec],
        out_specs=row,
        out_shape=jax.ShapeDtypeStruct((n, d), F32),
        compiler_params=_cparams("parallel"),
        name="final_ln",
    )(x1, f, g2, ln_g.reshape(1, d), ln_b.reshape(1, d))


ROW_BLOCK = 256
SCAN_BATCH = 4
SCAN_BLOCK = 128
SB_BLOCK = 1024
SB_KEY_BLOCK = 256
SB_PAGES = 16
ROUTE_BLOCK = 128
PEER_BLOCK = 128


def _mixer_half(x, mods, shift_prev, wkv0, attend, wts, rows_per_mod, rb):
    bsz, t, d = x.shape
    n = bsz * t
    sh1, sc1, g1, sh2, sc2, _ = mods
    xr = x.reshape(n, d)
    p, q, k, v, kb, vb = _inproj(xr, sh1, sc1, wts["w_in"], rows_per_mod, rb)
    p3 = p.reshape(bsz, t, RWKV_PROJ)
    rw, wkv_new = _rwkv(p3, shift_prev, wkv0, wts["rwkv"], SCAN_BATCH, SCAN_BLOCK)
    sb = attend(q, k, v, kb, vb)
    x1, h2, qp = _outproj(rw.reshape(n, WIDTH), sb.reshape(n, WIDTH), xr, g1, sh2, sc2, wts["w_out"],
                          wts["ln1_g"], wts["ln1_b"], wts["w_pq"], rows_per_mod, rb)
    heads = lambda a: a.reshape(bsz, t, N_HEADS, HEAD_DIM)
    return x1, h2, qp, heads(k), heads(v), wkv_new, p3[:, -1]


def _layer_pair(xp, xs, mod_p, mod_s, cache_k, cache_v, page_table, state_wkv, state_shift, wts):
    bp, tp, d = xp.shape
    bs, ts, _ = xs.shape
    assert ts == 1, "the paged attention handles one new token per sequence"
    n_p = bp * tp

    def attend_p(q, k, v, kb, vb):
        r = lambda a: a.reshape(bp, tp, WIDTH)
        return _sb_prompt(r(q), r(kb), r(vb), wts["sb_bias"], SB_BLOCK, SB_KEY_BLOCK)

    def attend_s(q, k, v, kb, vb):
        heads = lambda a: a.astype(F32).reshape(bs, N_HEADS, HEAD_DIM)
        return _sb_sample(heads(q), heads(k), heads(v), wts["sb_bias"], cache_k, cache_v, page_table, SB_PAGES)

    zero_shift = jnp.zeros((bp, RWKV_PROJ), xp.dtype)
    zero_wkv = jnp.zeros((bp, N_HEADS, HEAD_DIM, HEAD_DIM), state_wkv.dtype)
    x1p, h2p, qpp, kp, vp, wp, sp = _mixer_half(xp, mod_p, zero_shift, zero_wkv, attend_p, wts, tp, ROW_BLOCK)
    x1s, h2s, qps, ks, vs, ws, ss = _mixer_half(xs, mod_s, state_shift, state_wkv, attend_s, wts, 1, bs)

    def channel_mixer(h2, qp):
        n = h2.shape[0]
        eidx, gate = _route(qp, wts["sub_keys"], ROUTE_BLOCK)
        idx_flat = eidx.transpose(2, 0, 1).reshape(-1) * ROW_PLANES
        return _peer_experts(h2, idx_flat, gate.reshape(PEER_SLOTS, n), wts["peer_u"], wts["peer_v"], PEER_BLOCK)

    yp = _final(x1p, channel_mixer(h2p, qpp), mod_p[5], wts["ln2_g"], wts["ln2_b"], tp, ROW_BLOCK).reshape(bp, tp, d)
    ys = _final(x1s, channel_mixer(h2s, qps), mod_s[5], wts["ln2_g"], wts["ln2_b"], 1, bs).reshape(bs, ts, d)
    return yp, ys, kp, vp, ks, vs, wp, ws, sp, ss


def kernel(x_prompt, x_sample, c_prompt, c_sample, cache_k, cache_v, page_table, state_wkv, state_shift,
           w_cond, b_cond, w_in, mu_shift, w0, w_up, a0, a_up, g_up, k_k, k_a, r_k, lnx_g, lnx_b,
           sb_bias, w_out, ln1_g, ln1_b, w_pq, sub_keys, peer_u, peer_v, ln2_g, ln2_b):
    depth = w_in.shape[0]
    bp, bs = c_prompt.shape[0], c_sample.shape[0]
    pad = (-(bp + bs)) % 8
    c_all = jnp.concatenate([c_prompt, c_sample, jnp.zeros((pad, c_prompt.shape[1]), c_prompt.dtype)], axis=0)
    yp, ys = x_prompt, x_sample
    outs = [[] for _ in range(8)]
    for l in range(depth):
        wts = dict(
            w_in=w_in[l].astype(BF16), w_out=w_out[l].astype(BF16), w_pq=w_pq[l].astype(BF16),
            rwkv=_rwkv_params(mu_shift[l], w0[l], w_up[l], a0[l], a_up[l], g_up[l], k_k[l], k_a[l], r_k[l],
                              lnx_g[l], lnx_b[l]),
            sb_bias=sb_bias[l], ln1_g=ln1_g[l], ln1_b=ln1_b[l], ln2_g=ln2_g[l], ln2_b=ln2_b[l],
            sub_keys=_sub_key_pairs(sub_keys[l]), peer_u=_pack_table(peer_u[l]), peer_v=_pack_table(peer_v[l]))
        mod = _cond(c_all, w_cond[l], b_cond[l])
        mod_p = jnp.split(mod[:bp], N_MOD, axis=-1)
        mod_s = jnp.split(mod[bp:bp + bs], N_MOD, axis=-1)
        res = _layer_pair(yp, ys, mod_p, mod_s, cache_k[l], cache_v[l], page_table, state_wkv[l], state_shift[l], wts)
        yp, ys = res[0], res[1]
        for acc, val in zip(outs, res[2:]):
            acc.append(val)
    return (yp, ys) + tuple(jnp.stack(o) for o in outs)
```

```python
import functools

import jax
import jax.numpy as jnp
from jax import lax
from jax.experimental import pallas as pl
from jax.experimental.pallas import tpu as pltpu

F32 = jnp.float32
BF16 = jnp.bfloat16

D_MODEL = 1024
HEAD_DIM = 64
N_HEADS = 8
WIDTH = N_HEADS * HEAD_DIM
W_LORA, A_LORA, G_LORA = 64, 64, 128
RWKV_PROJ = 3 * WIDTH + W_LORA + A_LORA + G_LORA
IN_COLS = RWKV_PROJ + 3 * WIDTH
GN_EPS = HEAD_DIM * 1e-5
LN_EPS = 1e-5
SB_SCALE = HEAD_DIM ** -0.5
PAGE_SIZE = 128
PEER_HEADS = 8
N_KEYS = 128
PEER_TOPK = 16
PEER_SLOTS = PEER_HEADS * PEER_TOPK
N_MOD = 6
DEPTH = 1
DN_ALPHA = (2 * DEPTH) ** 0.25

VMEM_LIMIT = 56 * 1024 * 1024


def _cparams(*sem):
    return pltpu.CompilerParams(dimension_semantics=sem, vmem_limit_bytes=VMEM_LIMIT)


def _ln_rows(x):
    mu = jnp.mean(x, axis=-1, keepdims=True)
    xc = x - mu
    var = jnp.mean(xc * xc, axis=-1, keepdims=True)
    return xc * lax.rsqrt(var + LN_EPS)


def _split_bf16(x):
    hi = x.astype(BF16)
    lo = (x - hi.astype(F32)).astype(BF16)
    return hi, lo


def _dot(a, b):
    return jnp.dot(a, b, preferred_element_type=F32)


def _dot_nt(a, b):
    return lax.dot_general(a, b, (((1,), (1,)), ((), ())), preferred_element_type=F32)


def _dot2(x, w_bf16):
    hi, lo = _split_bf16(x)
    return _dot(hi, w_bf16) + _dot(lo, w_bf16)


def _cond_kernel(c_ref, w_ref, b_ref, o_ref):
    c = c_ref[...]
    s = c * jax.nn.sigmoid(c)
    o_ref[...] = jnp.dot(s, w_ref[...], preferred_element_type=F32,
                         precision=lax.Precision.HIGHEST) + b_ref[...]


def _cond(c, w_cond, b_cond):
    n, d = c.shape
    cols = w_cond.shape[1]
    bn = 1024
    return pl.pallas_call(
        _cond_kernel,
        grid=(cols // bn,),
        in_specs=[pl.BlockSpec((n, d), lambda j: (0, 0)),
                  pl.BlockSpec((d, bn), lambda j: (0, j)),
                  pl.BlockSpec((1, bn), lambda j: (0, j))],
        out_specs=pl.BlockSpec((n, bn), lambda j: (0, j)),
        out_shape=jax.ShapeDtypeStruct((n, cols), F32),
        compiler_params=_cparams("parallel"),
        name="cond",
    )(c, w_cond, b_cond.reshape(1, cols))


def _inproj_kernel(x_ref, sh_ref, sc_ref, w_ref, p_ref, q_ref, k_ref, v_ref, kb_ref, vb_ref):
    h = _ln_rows(x_ref[...]) * (1.0 + sc_ref[...]) + sh_ref[...]
    hb = h.astype(BF16)
    p_ref[...] = _dot(hb, w_ref[:, :RWKV_PROJ])
    q_ref[...] = _dot(hb, w_ref[:, RWKV_PROJ:RWKV_PROJ + WIDTH]).astype(BF16)
    k = _dot(hb, w_ref[:, RWKV_PROJ + WIDTH:RWKV_PROJ + 2 * WIDTH])
    v = _dot(hb, w_ref[:, RWKV_PROJ + 2 * WIDTH:])
    k_ref[...] = k
    v_ref[...] = v
    kb_ref[...] = k.astype(BF16)
    vb_ref[...] = v.astype(BF16)


def _inproj(x, shift, scale, w_in_bf16, rows_per_mod, block_rows):
    n, d = x.shape
    rb = block_rows
    shift, mod_spec = _mod_spec(shift, rows_per_mod, rb, d)
    scale, _ = _mod_spec(scale, rows_per_mod, rb, d)
    row = lambda c: pl.BlockSpec((rb, c), lambda i: (i, 0))
    return pl.pallas_call(
        _inproj_kernel,
        grid=(n // rb,),
        in_specs=[row(d), mod_spec, mod_spec,
                  pl.BlockSpec((d, IN_COLS), lambda i: (0, 0))],
        out_specs=[row(RWKV_PROJ), row(WIDTH), row(WIDTH), row(WIDTH), row(WIDTH), row(WIDTH)],
        out_shape=[jax.ShapeDtypeStruct((n, RWKV_PROJ), F32),
                   jax.ShapeDtypeStruct((n, WIDTH), BF16),
                   jax.ShapeDtypeStruct((n, WIDTH), F32),
                   jax.ShapeDtypeStruct((n, WIDTH), F32),
                   jax.ShapeDtypeStruct((n, WIDTH), BF16),
                   jax.ShapeDtypeStruct((n, WIDTH), BF16)],
        compiler_params=_cparams("parallel"),
        name="inproj",
    )(x, shift, scale, w_in_bf16)


def _seg_ones(n):
    i = jnp.arange(n) // HEAD_DIM
    return (i[:, None] == i[None, :]).astype(BF16)


def _softplus(u):
    return jnp.maximum(u, 0.0) + jnp.log(1.0 + jnp.exp(-jnp.abs(u)))


def _rwkv_pre_kernel(has_prev, chunked, p_ref, prev_ref, mu_ref, w0_ref, wup_ref, a0_ref, aup_ref, gup_ref,
                     kk_ref, ka_ref, rk_ref, seg_ref, *refs):
    out_refs, carry_ref = refs[:-1], refs[-1]
    pf = p_ref[...]
    if has_prev:
        prev = prev_ref[...]
    else:
        tb = pl.program_id(1)
        first = jnp.where(tb == 0, prev_ref[...], carry_ref[...])
        rows = lax.broadcasted_iota(jnp.int32, pf.shape, 0)
        prev = jnp.where(rows == 0, first, pltpu.roll(pf, 1, axis=0))
        carry_ref[...] = pf[pf.shape[0] - 1:, :]
    pm = pf + (prev - pf) * mu_ref[...]
    r = pm[:, :WIDTH]
    k = pm[:, WIDTH:2 * WIDTH]
    v = pm[:, 2 * WIDTH:3 * WIDTH]
    dwa = pm[:, 3 * WIDTH:3 * WIDTH + W_LORA + A_LORA]
    dg = pm[:, 3 * WIDTH + W_LORA + A_LORA:]
    seg = seg_ref[...]
    w = -_softplus(-(w0_ref[...] + _dot3(jnp.tanh(dwa), wup_ref[...], _NN))) - 0.5
    log_dec = -jnp.exp(w)
    a = jax.nn.sigmoid(a0_ref[...] + _dot3(dwa, aup_ref[...], _NN))
    g = _dot3(jax.nn.sigmoid(dg), gup_ref[...], _NN)
    kkr = k * kk_ref[...]
    kk = kkr * lax.rsqrt(jnp.maximum(_dot2(kkr * kkr, seg), 1e-24))
    km = k * (1.0 + (a - 1.0) * ka_ref[...])
    b = kk * a
    bonus = _dot2(r * km * rk_ref[...], seg) * v
    if chunked:
        outs = (kk, log_dec, b, km, v, r, g, bonus)
    else:
        dec = jnp.exp(log_dec)
        outs = (kk, dec, b, km, v, dec * r, _dot2(b * r, seg), _dot2(km * r, seg), g, bonus)
    for ref, val in zip(out_refs, outs):
        ref[...] = val


def _rwkv_pre(p, prev, has_prev, chunked, prm, tb):
    bsz, t, _ = p.shape
    n_out = 8 if chunked else 10
    blk = lambda c: pl.BlockSpec((None, tb, c), lambda i, j: (i, j, 0))
    full = lambda a: pl.BlockSpec(a.shape, lambda i, j: (0,) * a.ndim)
    prev_spec = blk(RWKV_PROJ) if has_prev else pl.BlockSpec((None, 1, RWKV_PROJ), lambda i, j: (i, 0, 0))
    params = [prm["mu"], prm["w0"], prm["wup"], prm["a0"], prm["aup"], prm["gup"],
              prm["k_k"], prm["k_a"], prm["r_k"], prm["seg512"]]
    return pl.pallas_call(
        functools.partial(_rwkv_pre_kernel, has_prev, chunked),
        grid=(bsz, t // tb),
        in_specs=[blk(RWKV_PROJ), prev_spec] + [full(a) for a in params],
        out_specs=[blk(WIDTH)] * n_out,
        out_shape=[jax.ShapeDtypeStruct((bsz, t, WIDTH), F32)] * n_out,
        scratch_shapes=[pltpu.VMEM((1, RWKV_PROJ), F32)],
        compiler_params=_cparams("parallel", "arbitrary"),
        name="rwkv_pre",
    )(p, prev, *params)


def _rwkv_scan_kernel(nb_count, tb, kk_ref, dec_ref, b_ref, km_ref, v_ref, wr_ref, br_ref, kr_ref,
                      g_ref, bon_ref, s0_ref, lg_ref, lb_ref, seg256_ref, seg512_ref,
                      out_ref, st_ref, s_scr, y_scr):
    step_blk = pl.program_id(1)

    @pl.when(step_blk == 0)
    def _():
        s_scr[...] = s0_ref[...]

    shape = (HEAD_DIM, WIDTH)
    ident = (lax.broadcasted_iota(jnp.int32, shape, 1) & (HEAD_DIM - 1)) == lax.broadcasted_iota(jnp.int32, shape, 0)
    seg = seg256_ref[...]
    half = WIDTH // 2

    def segsum(lhs):
        return jnp.concatenate([_dot(lhs[:, :half], seg), _dot(lhs[:, half:], seg)], axis=1)

    def step(t, carry):
        for nb in range(nb_count):
            row = lambda ref: ref[nb, pl.ds(t, 1), :]
            s = s_scr[nb]
            p_hi, p_lo = _split_bf16(s * row(kk_ref))
            dv = jnp.where(ident, row(v_ref), 0.0).astype(BF16)
            pr = (s * row(wr_ref)).astype(BF16)
            res = segsum(jnp.concatenate([p_hi, p_lo, dv, pr], axis=0))
            sa = res[:HEAD_DIM] + res[HEAD_DIM:2 * HEAD_DIM]
            vcol = res[2 * HEAD_DIM:3 * HEAD_DIM]
            ycol = res[3 * HEAD_DIM:] - sa * row(br_ref)
            s_scr[nb] = s * row(dec_ref) - sa * row(b_ref) + vcol * row(km_ref)
            y_scr[nb, pl.ds(t, 1), :] = (jnp.sum(jnp.where(ident, ycol, 0.0), axis=0, keepdims=True)
                                         + row(v_ref) * row(kr_ref))
        return carry

    lax.fori_loop(0, tb, step, 0)

    seg512 = seg512_ref[...]
    for nb in range(nb_count):
        y = y_scr[nb]
        mu = _dot2(y, seg512) * (1.0 / HEAD_DIM)
        yc = y - mu
        var = _dot2(yc * yc, seg512) * (1.0 / HEAD_DIM)
        yn = yc * lax.rsqrt(var + GN_EPS) * lg_ref[...] + lb_ref[...]
        out_ref[nb] = (yn + bon_ref[nb]) * g_ref[nb]

    @pl.when(step_blk == pl.num_programs(1) - 1)
    def _():
        st_ref[...] = s_scr[...]


def _rwkv_scan(pre, s0, prm, nb, tb):
    kk, dec, b, km, v, wr, br, kr, g, bon = pre
    bsz, t, _ = kk.shape
    blk = pl.BlockSpec((nb, tb, WIDTH), lambda i, j: (i, j, 0))
    sblk = pl.BlockSpec((nb, HEAD_DIM, WIDTH), lambda i, j: (i, 0, 0))
    full = lambda a: pl.BlockSpec(a.shape, lambda i, j: (0,) * a.ndim)
    params = [prm["lnx_g"], prm["lnx_b"], prm["seg256"], prm["seg512"]]
    return pl.pallas_call(
        functools.partial(_rwkv_scan_kernel, nb, tb),
        grid=(bsz // nb, t // tb),
        in_specs=[blk] * 10 + [sblk] + [full(a) for a in params],
        out_specs=[blk, sblk],
        out_shape=[jax.ShapeDtypeStruct((bsz, t, WIDTH), F32),
                   jax.ShapeDtypeStruct((bsz, HEAD_DIM, WIDTH), F32)],
        scratch_shapes=[pltpu.VMEM((nb, HEAD_DIM, WIDTH), F32), pltpu.VMEM((nb, tb, WIDTH), F32)],
        compiler_params=_cparams("parallel", "arbitrary"),
        name="rwkv_scan",
    )(kk, dec, b, km, v, wr, br, kr, g, bon, s0, *params)


def _rwkv_params(mu_shift, w0, w_up, a0, a_up, g_up, k_k, k_a, r_k, lnx_g, lnx_b):
    row = lambda a: a.reshape(1, -1).astype(F32)
    zeros = jnp.zeros((A_LORA, WIDTH), F32)
    return dict(mu=row(mu_shift), w0=row(w0), a0=row(a0), k_k=row(k_k), k_a=row(k_a), r_k=row(r_k),
                lnx_g=row(lnx_g), lnx_b=row(lnx_b), gup=g_up,
                wup=jnp.concatenate([w_up, zeros], axis=0), aup=jnp.concatenate([zeros, a_up], axis=0),
                seg256=_seg_ones(WIDTH // 2), seg512=_seg_ones(WIDTH))


def _state_to_rows(wkv):
    bsz = wkv.shape[0]
    return wkv.transpose(0, 2, 1, 3).reshape(bsz, HEAD_DIM, WIDTH)


def _rows_to_state(s):
    bsz = s.shape[0]
    return s.reshape(bsz, HEAD_DIM, N_HEADS, HEAD_DIM).transpose(0, 2, 1, 3)


CHUNK = 64
PREP_CHUNKS = 4
PAIR = 2 * HEAD_DIM
N_PAIRS = WIDTH // PAIR
_NN = ((1,), (0,))
_NT = ((1,), (1,))
_TN = ((0,), (0,))


def _dot3(a, b, dims):
    ah, al = _split_bf16(a)
    bh, bl = _split_bf16(b)
    dg = lambda x, y: lax.dot_general(x, y, (dims, ((), ())), preferred_element_type=F32)
    return dg(ah, bh) + dg(ah, bl) + dg(al, bh)


def _dot1(a, b, dims):
    return lax.dot_general(a.astype(BF16), b.astype(BF16), (dims, ((), ())), preferred_element_type=F32)


def _by_head(x, first):
    zero = jnp.zeros_like(x)
    return jnp.concatenate([jnp.where(first, x, zero), jnp.where(first, zero, x)], axis=0)


def _fold_heads(x):
    c = x.shape[0] // 2
    return x[:c] + x[c:]


def _rwkv_chunk_prep_kernel(kk_ref, ld_ref, b_ref, km_ref, v_ref, r_ref, tri_ref,
                            w1_ref, w2_ref, qr_ref, arb_ref, y0_ref, bd_ref, sadd_ref, gc_ref):
    c = CHUNK
    tri = tri_ref[...]
    first = lax.broadcasted_iota(jnp.int32, (1, PAIR), 1) < HEAD_DIM
    rows = lax.broadcasted_iota(jnp.int32, (PAIR, PAIR), 0)
    cols = lax.broadcasted_iota(jnp.int32, (PAIR, PAIR), 1)
    same = (rows // c) == (cols // c)
    strict = same & (cols < rows)
    incl = same & (cols <= rows)
    eye = jnp.where(rows == cols, 1.0, 0.0)

    units = []
    for cc in range(kk_ref.shape[0] // c):
        rs = slice(cc * c, (cc + 1) * c)
        ld = ld_ref[rs, :]
        ld_hi, ld_lo = _split_bf16(ld)
        cum = _dot(tri, ld_hi) + _dot(tri, ld_lo)
        last = cum[c - 1:, :]
        g_end = jnp.exp(last - cum)
        g_inv = jnp.exp(-cum)
        qa = kk_ref[rs, :] * jnp.exp(cum - ld)
        qr = r_ref[rs, :] * jnp.exp(cum)
        kb, kt = b_ref[rs, :] * g_inv, km_ref[rs, :] * g_inv
        bd, kd = b_ref[rs, :] * g_end, km_ref[rs, :] * g_end
        v = v_ref[rs, :]
        qr_ref[rs, :] = qr
        bd_ref[rs, :] = bd
        gc_ref[cc] = jnp.exp(last)
        for p in range(N_PAIRS):
            sl = slice(p * PAIR, (p + 1) * PAIR)
            units.append(dict(cc=cc, p=p, rs=rs, sl=sl, v=v[:, sl], kd=kd[:, sl],
                              **{k: _by_head(x[:, sl], first)
                                 for k, x in dict(qa2=qa, qr2=qr, kb2=kb, kt2=kt, v2=v).items()}))
    n = [-jnp.where(strict, _dot3(u["qa2"], u["kb2"], _NT), 0.0) for u in units]
    aak = [jnp.where(strict, _dot3(u["qa2"], u["kt2"], _NT), 0.0) for u in units]
    for u in units:
        arb_ref[u["rs"], u["sl"]] = _fold_heads(jnp.where(incl, _dot1(u["qr2"], u["kb2"], _NT), 0.0))
        ark = jnp.where(incl, _dot1(u["qr2"], u["kt2"], _NT), 0.0)
        y0_ref[u["rs"], u["sl"]] = _fold_heads(_dot1(ark, u["v2"], _NN))
        sadd_ref[u["cc"], u["p"]] = jnp.where(same, _dot3(u["v"], u["kd"], _TN), 0.0)
    t_inv = [eye + x for x in n]
    power = n
    for _ in range(c.bit_length() - 2):
        power = [_dot(x.astype(BF16), x.astype(BF16)) for x in power]
        t_inv = [t + _dot(t.astype(BF16), x.astype(BF16)) for t, x in zip(t_inv, power)]
    rhs = [_dot3(a, u["v2"], _NN) for a, u in zip(aak, units)]
    for t, r, u in zip(t_inv, rhs, units):
        w1_ref[u["rs"], u["sl"]] = _fold_heads(_dot3(t, u["qa2"], _NN))
        w2_ref[u["rs"], u["sl"]] = _fold_heads(_dot3(t, r, _NN))


def _rwkv_chunk_scan_kernel(nb_count, w1_ref, w2_ref, qr_ref, arb_ref, y0_ref, bd_ref, sadd_ref, gc_ref,
                            g_ref, bon_ref, s0_ref, lg_ref, lb_ref, seg512_ref, out_ref, st_ref, sx_scr):
    c = CHUNK
    chunk_id = pl.program_id(1)
    first = lax.broadcasted_iota(jnp.int32, (1, PAIR), 1) < HEAD_DIM
    rows = lax.broadcasted_iota(jnp.int32, (PAIR, PAIR), 0)
    cols = lax.broadcasted_iota(jnp.int32, (PAIR, PAIR), 1)
    same = (rows // HEAD_DIM) == (cols // HEAD_DIM)

    @pl.when(chunk_id == 0)
    def _():
        for nb in range(nb_count):
            for p in range(N_PAIRS):
                sx_scr[nb, p] = _by_head(s0_ref[nb, :, p * PAIR:(p + 1) * PAIR], first)

    seg512 = seg512_ref[...]
    chains = [(nb, p, slice(p * PAIR, (p + 1) * PAIR)) for nb in range(nb_count) for p in range(N_PAIRS)]
    proj = [_dot3(jnp.concatenate([w1_ref[nb, :, sl], qr_ref[nb, :, sl]], axis=0), sx_scr[nb, p], _NT)
            for nb, p, sl in chains]
    us = [proj[i][:c] + w2_ref[nb, :, sl] for i, (nb, p, sl) in enumerate(chains)]
    ys = [proj[i][c:] - _dot1(arb_ref[nb, :, sl], _by_head(us[i], first), _NN) + y0_ref[nb, :, sl]
          for i, (nb, p, sl) in enumerate(chains)]
    for i, (nb, p, sl) in enumerate(chains):
        upd = jnp.where(same, _dot3(us[i], bd_ref[nb, :, sl], _TN), 0.0)
        sx_scr[nb, p] = sx_scr[nb, p] * gc_ref[nb, :, sl] - upd + sadd_ref[nb, p]
    for nb in range(nb_count):
        y = jnp.concatenate(ys[nb * N_PAIRS:(nb + 1) * N_PAIRS], axis=1)
        mu = _dot2(y, seg512) * (1.0 / HEAD_DIM)
        yc = y - mu
        var = _dot2(yc * yc, seg512) * (1.0 / HEAD_DIM)
        yn = yc * lax.rsqrt(var + GN_EPS) * lg_ref[...] + lb_ref[...]
        out_ref[nb] = (yn + bon_ref[nb]) * g_ref[nb]

    @pl.when(chunk_id == pl.num_programs(1) - 1)
    def _():
        for nb in range(nb_count):
            st_ref[nb] = jnp.concatenate([_fold_heads(sx_scr[nb, p]) for p in range(N_PAIRS)], axis=1)


def _rwkv_chunked(pre, s0, prm, nb):
    kk, ld, b, km, v, r, g, bon = pre
    bsz, t, _ = kk.shape
    n_chunks = t // CHUNK
    idx = jnp.arange(CHUNK)
    tri = (idx[None, :] <= idx[:, None]).astype(BF16)
    cps = PREP_CHUNKS
    blk = pl.BlockSpec((None, cps * CHUNK, WIDTH), lambda i, j: (i, j, 0))
    wide = jax.ShapeDtypeStruct((bsz, t, WIDTH), F32)
    w1, w2, qr, arb, y0, bd, sadd, gc = pl.pallas_call(
        _rwkv_chunk_prep_kernel,
        grid=(bsz, n_chunks // cps),
        in_specs=[blk] * 6 + [pl.BlockSpec((CHUNK, CHUNK), lambda i, j: (0, 0))],
        out_specs=[blk] * 6 + [pl.BlockSpec((None, cps, N_PAIRS, PAIR, PAIR), lambda i, j: (i, j, 0, 0, 0)),
                               pl.BlockSpec((None, cps, 1, WIDTH), lambda i, j: (i, j, 0, 0))],
        out_shape=[wide] * 6 + [jax.ShapeDtypeStruct((bsz, n_chunks, N_PAIRS, PAIR, PAIR), F32),
                                jax.ShapeDtypeStruct((bsz, n_chunks, 1, WIDTH), F32)],
        compiler_params=_cparams("parallel", "parallel"),
        name="rwkv_chunk_prep",
    )(kk, ld, b, km, v, r, tri)
    nblk = pl.BlockSpec((nb, CHUNK, WIDTH), lambda i, j: (i, j, 0))
    sblk = pl.BlockSpec((nb, HEAD_DIM, WIDTH), lambda i, j: (i, 0, 0))
    full = lambda a: pl.BlockSpec(a.shape, lambda i, j: (0,) * a.ndim)
    params = [prm["lnx_g"], prm["lnx_b"], prm["seg512"]]
    return pl.pallas_call(
        functools.partial(_rwkv_chunk_scan_kernel, nb),
        grid=(bsz // nb, n_chunks),
        in_specs=[nblk] * 6
        + [pl.BlockSpec((nb, None, N_PAIRS, PAIR, PAIR), lambda i, j: (i, j, 0, 0, 0)),
           pl.BlockSpec((nb, None, 1, WIDTH), lambda i, j: (i, j, 0, 0)),
           nblk, nblk, sblk] + [full(a) for a in params],
        out_specs=[nblk, sblk],
        out_shape=[wide, jax.ShapeDtypeStruct((bsz, HEAD_DIM, WIDTH), F32)],
        scratch_shapes=[pltpu.VMEM((nb, N_PAIRS, PAIR, PAIR), F32)],
        compiler_params=_cparams("parallel", "arbitrary"),
        name="rwkv_chunk_scan",
    )(w1, w2, qr, arb, y0, bd, sadd, gc, g, bon, s0, *params)


def _rwkv(p, shift_prev, wkv0, prm, nb, tb):
    bsz, t, _ = p.shape
    s0 = _state_to_rows(wkv0)
    if t == 1:
        pre = _rwkv_pre(p.reshape(1, bsz, RWKV_PROJ), shift_prev.reshape(1, bsz, RWKV_PROJ), True, False, prm, bsz)
        out, st = _rwkv_scan([a.reshape(bsz, 1, WIDTH) for a in pre], s0, prm, nb, 1)
    else:
        pre = _rwkv_pre(p, shift_prev.reshape(bsz, 1, RWKV_PROJ), False, True, prm, tb)
        out, st = _rwkv_chunked(pre, s0, prm, nb)
    return out, _rows_to_state(st)


def _sb_tile(z, mask, carry, neg_ge, neg_ones):
    sp = _softplus(z)
    if mask is not None:
        sp = jnp.where(mask, sp, 0.0)
    spb = sp.astype(BF16)
    inc = _dot(spb, neg_ge)
    bk = z.shape[1]
    a = jnp.exp(z + inc + jnp.concatenate([carry] * (bk // carry.shape[1]), axis=1))
    if mask is not None:
        a = jnp.where(mask, a, 0.0)
    if neg_ones is None:
        return a.astype(BF16), carry + jnp.broadcast_to(inc[:, :1], carry.shape)
    return a.astype(BF16), carry + _dot(spb, neg_ones)


def _sb_prompt_kernel(bq, bk, bias_ref, q_ref, k_ref, v_ref, mge_ref, o_ref):
    hp = pl.program_id(1)
    qi = pl.program_id(2)
    lane = lax.broadcasted_iota(jnp.int32, (1, 2 * HEAD_DIM), 1)
    first = lane < HEAD_DIM
    q2 = q_ref[...] * jnp.asarray(SB_SCALE, BF16)
    zero = jnp.zeros_like(q2)
    q_heads = (jnp.where(first, q2, zero), jnp.where(first, zero, q2))
    biases = (bias_ref[2 * hp], bias_ref[2 * hp + 1])
    m_ge = mge_ref[...]
    per_q = bq // bk
    rows = lax.broadcasted_iota(jnp.int32, (bq, bk), 0)
    cols = lax.broadcasted_iota(jnp.int32, (bq, bk), 1)

    def tile(j, mask, state):
        acc, carries = state
        kblk = k_ref[pl.ds(pl.multiple_of(j * bk, bk), bk), :]
        vblk = v_ref[pl.ds(pl.multiple_of(j * bk, bk), bk), :]
        vzero = jnp.zeros_like(vblk)
        v_heads = (jnp.where(first, vblk, vzero), jnp.where(first, vzero, vblk))
        new_carries = []
        for e in range(2):
            z = _dot_nt(q_heads[e], kblk) + biases[e]
            a, c = _sb_tile(z, mask, carries[e], m_ge, None)
            acc = acc + _dot(a, v_heads[e])
            new_carries.append(c)
        return acc, tuple(new_carries)

    zc = jnp.zeros((bq, 2 * HEAD_DIM), F32)
    state = (zc, (zc, zc))
    for u in reversed(range(per_q)):
        state = tile(qi * per_q + u, cols + u * bk < rows, state)
    n_full = qi * per_q

    def full_tiles(i, s):
        for u in range(per_q):
            s = tile(n_full - 1 - i * per_q - u, None, s)
        return s

    state = lax.fori_loop(0, qi, full_tiles, state)
    o_ref[...] = state[0]


def _sb_prompt(q, k, v, sb_bias, bq, bk):
    bsz, t, _ = q.shape
    pair = 2 * HEAD_DIM
    idx = jnp.arange(bk)
    m_ge = -(idx[:, None] >= idx[None, :]).astype(BF16)
    return pl.pallas_call(
        functools.partial(_sb_prompt_kernel, bq, bk),
        grid=(bsz, WIDTH // pair, t // bq),
        in_specs=[pl.BlockSpec(memory_space=pltpu.SMEM),
                  pl.BlockSpec((None, bq, pair), lambda b, h, i: (b, i, h)),
                  pl.BlockSpec((None, t, pair), lambda b, h, i: (b, 0, h)),
                  pl.BlockSpec((None, t, pair), lambda b, h, i: (b, 0, h)),
                  pl.BlockSpec((bk, bk), lambda b, h, i: (0, 0))],
        out_specs=pl.BlockSpec((None, bq, pair), lambda b, h, i: (b, i, h)),
        out_shape=jax.ShapeDtypeStruct((bsz, t, WIDTH), F32),
        compiler_params=_cparams("parallel", "parallel", "arbitrary"),
        name="sb_prompt",
    )(sb_bias.astype(F32), q, k, v, m_ge)


def _sb_sample_kernel(pp, n_pages, pt_ref, q_ref, knew_ref, vnew_ref, bias_ref, mge_ref, ones_ref, *refs):
    k_refs, v_refs = refs[:pp], refs[pp:2 * pp]
    o_ref, carry_scr, acc_scr = refs[2 * pp:]
    g = pl.program_id(1)

    @pl.when(g == 0)
    def _():
        carry_scr[...] = jnp.zeros_like(carry_scr)
        acc_scr[...] = jnp.zeros_like(acc_scr)

    shape = (N_HEADS, WIDTH)
    own = (lax.broadcasted_iota(jnp.int32, shape, 1) // HEAD_DIM) == lax.broadcasted_iota(jnp.int32, shape, 0)
    q = q_ref[...]
    qb = jnp.where(own, jnp.concatenate([q * SB_SCALE] * N_HEADS, axis=1), 0.0).astype(BF16)
    bias = bias_ref[...]
    zs = [_dot(qb, k_refs[u][...].astype(BF16)) + bias for u in range(pp)]
    z_all = jnp.concatenate(zs, axis=1)
    spb = _softplus(z_all).astype(BF16)
    incs, carries = [], []
    carry = carry_scr[...]
    for u in range(pp):
        page = spb[:, u * PAGE_SIZE:(u + 1) * PAGE_SIZE]
        incs.append(_dot(page, mge_ref[...]))
        carries.append(carry)
        carry = carry + _dot(page, ones_ref[...])
    carry_scr[...] = carry
    a_all = jnp.exp(z_all + jnp.concatenate(incs, axis=1) + jnp.concatenate(carries, axis=1)).astype(BF16)
    acc = acc_scr[...]
    for u in range(pp):
        acc = acc + _dot_nt(a_all[:, u * PAGE_SIZE:(u + 1) * PAGE_SIZE], v_refs[u][...].astype(BF16))
    acc_scr[...] = acc

    @pl.when(g == pl.num_programs(1) - 1)
    def _():
        head_row = lax.broadcasted_iota(jnp.int32, (N_HEADS, HEAD_DIM), 0)
        out = jnp.zeros((N_HEADS, HEAD_DIM), F32)
        for h in range(N_HEADS):
            out = jnp.where(head_row == h, acc[:, h * HEAD_DIM:(h + 1) * HEAD_DIM], out)
        past = n_pages * PAGE_SIZE
        z_new = jnp.sum(q * knew_ref[...], axis=1, keepdims=True) * SB_SCALE + bias[:, :1]
        a_new = jnp.where(past < past, jnp.exp(-_softplus(-z_new)), 0.0)
        o_ref[...] = out + a_new * vnew_ref[...]


def _sb_sample(q, k_new, v_new, sb_bias, cache_k, cache_v, page_table, pp):
    bsz = q.shape[0]
    n_pages = page_table.shape[1]
    idx = jnp.arange(PAGE_SIZE)
    m_ge = -(idx[:, None] >= idx[None, :]).astype(BF16)
    ones_cols = -jnp.ones((PAGE_SIZE, PAGE_SIZE), BF16)
    bias = jnp.broadcast_to(sb_bias.astype(F32)[:, None], (N_HEADS, PAGE_SIZE))
    rows = lambda c: c.transpose(0, 2, 3, 1).reshape(c.shape[0], WIDTH, PAGE_SIZE)
    row = pl.BlockSpec((None, N_HEADS, HEAD_DIM), lambda b, g, pt: (b, 0, 0))
    full = lambda a: pl.BlockSpec(a.shape, lambda b, g, pt: (0,) * a.ndim)

    def page_spec(u):
        return pl.BlockSpec((None, WIDTH, PAGE_SIZE),
                            lambda b, g, pt: (pt[b * n_pages + n_pages - 1 - (g * pp + u)], 0, 0))

    grid_spec = pltpu.PrefetchScalarGridSpec(
        num_scalar_prefetch=1,
        grid=(bsz, n_pages // pp),
        in_specs=[row, row, row, full(bias), full(m_ge), full(ones_cols)]
        + [page_spec(u) for u in range(pp)] * 2,
        out_specs=row,
        scratch_shapes=[pltpu.VMEM((N_HEADS, PAGE_SIZE), F32), pltpu.VMEM((N_HEADS, WIDTH), F32)],
    )
    return pl.pallas_call(
        functools.partial(_sb_sample_kernel, pp, n_pages),
        grid_spec=grid_spec,
        out_shape=jax.ShapeDtypeStruct((bsz, N_HEADS, HEAD_DIM), F32),
        compiler_params=_cparams("parallel", "arbitrary"),
        name="sb_sample",
    )(page_table.reshape(-1), q, k_new, v_new, bias, m_ge, ones_cols, *([rows(cache_k)] * pp), *([rows(cache_v)] * pp))


def _outproj_kernel(rw_ref, sb_ref, x_ref, g1_ref, sh2_ref, sc2_ref, wo_ref, lg_ref, lb_ref, wpq_ref,
                    x1_ref, h2_ref, qp_ref):
    mix = _dot(rw_ref[...].astype(BF16), wo_ref[:WIDTH, :]) + _dot(sb_ref[...].astype(BF16), wo_ref[WIDTH:, :])
    x1 = _ln_rows(DN_ALPHA * x_ref[...] + g1_ref[...] * mix) * lg_ref[...] + lb_ref[...]
    h2 = _ln_rows(x1) * (1.0 + sc2_ref[...]) + sh2_ref[...]
    x1_ref[...] = x1
    h2_ref[...] = h2
    qp_ref[...] = _dot(h2.astype(BF16), wpq_ref[...]).astype(BF16)


def _mod_spec(a, rows_per_mod, rb, d):
    if rows_per_mod == 1:
        return a, pl.BlockSpec((rb, d), lambda i: (i, 0))
    per = rows_per_mod // rb
    return a.reshape(-1, 1, d), pl.BlockSpec((None, 1, d), lambda i: (i // per, 0, 0))


def _outproj(rw, sb, x, g1, sh2, sc2, w_out_bf16, ln_g, ln_b, w_pq_bf16, rows_per_mod, rb):
    n, d = x.shape
    row = lambda c: pl.BlockSpec((rb, c), lambda i: (i, 0))
    full = lambda a: pl.BlockSpec(a.shape, lambda i: (0,) * a.ndim)
    mods, mod_specs = zip(*[_mod_spec(a, rows_per_mod, rb, d) for a in (g1, sh2, sc2)])
    lg, lb = ln_g.reshape(1, d), ln_b.reshape(1, d)
    return pl.pallas_call(
        _outproj_kernel,
        grid=(n // rb,),
        in_specs=[row(WIDTH), row(WIDTH), row(d), *mod_specs, full(w_out_bf16), full(lg), full(lb), full(w_pq_bf16)],
        out_specs=[row(d), row(d), row(d)],
        out_shape=[jax.ShapeDtypeStruct((n, d), F32), jax.ShapeDtypeStruct((n, d), F32),
                   jax.ShapeDtypeStruct((n, d), BF16)],
        compiler_params=_cparams("parallel"),
        name="outproj",
    )(rw, sb, x, *mods, w_out_bf16, lg, lb, w_pq_bf16)


def _take_top(x, ids, payload, count):
    vals, picked = [], []
    for _ in range(count):
        m = jnp.max(x, axis=0, keepdims=True)
        pos = jnp.min(jnp.where(x == m, ids, jnp.finfo(F32).max), axis=0, keepdims=True)
        hit = ids == pos
        vals.append(m)
        picked.append(pos if payload is None else jnp.sum(jnp.where(hit, payload, 0.0), axis=0, keepdims=True))
        x = jnp.where(hit, -jnp.inf, x)
    return jnp.concatenate(vals, axis=0), jnp.concatenate(picked, axis=0)


def _pair_candidates(sv0, si0, sv1, si1):
    k = PEER_TOPK
    tokens = sv0.shape[1]
    wide = 4
    vals, ids, experts = [], [], []
    for a in range(wide):
        n = -(-(k // (a + 1)) // 8) * 8
        b_ids = lax.broadcasted_iota(jnp.int32, (n, tokens), 0)
        valid = b_ids < k // (a + 1)
        vals.append(jnp.where(valid, sv0[a:a + 1] + sv1[:n], -jnp.inf))
        ids.append(jnp.where(valid, a * k + b_ids, -1).astype(F32))
        experts.append(si0[a:a + 1] * N_KEYS + si1[:n])
    for b in range(k // (wide + 1)):
        last_a = k // (b + 1) - 1
        n = -(-(last_a + 1) // 8) * 8
        a_ids = lax.broadcasted_iota(jnp.int32, (n, tokens), 0)
        valid = (a_ids >= wide) & (a_ids <= last_a)
        vals.append(jnp.where(valid, sv0[:n] + sv1[b:b + 1], -jnp.inf))
        ids.append(jnp.where(valid, a_ids * k + b, -1).astype(F32))
        experts.append(si0[:n] * N_KEYS + si1[b:b + 1])
    return jnp.concatenate(vals, axis=0), jnp.concatenate(ids, axis=0), jnp.concatenate(experts, axis=0)


def _route_kernel(qp_ref, sk_ref, e_ref, g_ref):
    tokens = qp_ref.shape[0]
    scores = _dot_nt(sk_ref[...], qp_ref[...])
    key_ids = lax.broadcasted_iota(jnp.int32, (N_KEYS, tokens), 0).astype(F32)
    sv0, si0 = _take_top(scores[:N_KEYS], key_ids, None, PEER_TOPK)
    sv1, si1 = _take_top(scores[N_KEYS:], key_ids, None, PEER_TOPK)
    cand, cand_ids, cidx = _pair_candidates(sv0, si0, sv1, si1)
    top, eidx = _take_top(cand, cand_ids, cidx, PEER_TOPK)
    ex = jnp.exp(top - top[:1])
    e_ref[...] = eidx.astype(jnp.int32)
    g_ref[...] = ex / jnp.sum(ex, axis=0, keepdims=True)


def _route(qp, sk_pairs, tb):
    n = qp.shape[0]
    blk = pl.BlockSpec((None, PEER_TOPK, tb), lambda i, h: (h, 0, i))
    return pl.pallas_call(
        _route_kernel,
        grid=(n // tb, PEER_HEADS),
        in_specs=[pl.BlockSpec((tb, 2 * HEAD_DIM), lambda i, h: (i, h)),
                  pl.BlockSpec((None, 2 * N_KEYS, 2 * HEAD_DIM), lambda i, h: (h, 0, 0))],
        out_specs=[blk, blk],
        out_shape=[jax.ShapeDtypeStruct((PEER_HEADS, PEER_TOPK, n), jnp.int32),
                   jax.ShapeDtypeStruct((PEER_HEADS, PEER_TOPK, n), F32)],
        compiler_params=_cparams("parallel", "parallel"),
        name="peer_route",
    )(qp, sk_pairs)


def _sub_key_pairs(sub_keys):
    z = jnp.zeros_like(sub_keys[:, 0])
    top = jnp.concatenate([sub_keys[:, 0], z], axis=-1)
    bot = jnp.concatenate([z, sub_keys[:, 1]], axis=-1)
    return jnp.concatenate([top, bot], axis=1).astype(BF16)


ROW_WORDS = D_MODEL // 2
ROW_PLANES = ROW_WORDS // 128
PLANE_STRIDE = PEER_SLOTS + 8


def _pack_table(w):
    bits = lax.bitcast_convert_type(w.astype(BF16), jnp.uint16).astype(jnp.uint32)
    packed = bits[:, :ROW_WORDS] | (bits[:, ROW_WORDS:] << 16)
    return lax.bitcast_convert_type(packed, jnp.int32).reshape(-1, 128)


def _gather_rows(idx_ref, first_token, tbl_ref, buf_ref, bank):
    token_idx = [idx_ref.at[pl.ds((first_token + w) * PEER_SLOTS, PEER_SLOTS)] for w in range(PEER_WIDTH)]
    for m in range(PEER_SLOTS):
        for w in range(PEER_WIDTH):
            start = pl.multiple_of(token_idx[w][m], ROW_PLANES)
            buf_ref[bank * PEER_WIDTH + w, pl.ds(m, ROW_PLANES, stride=PLANE_STRIDE), :] = (
                tbl_ref[pl.ds(start, ROW_PLANES), :])


def _plane_halves(buf_ref, j):
    words = buf_ref[pl.ds(j * PLANE_STRIDE, PEER_SLOTS), :]
    lo = lax.bitcast_convert_type(words << 16, F32)
    hi = lax.bitcast_convert_type(words & jnp.int32(-65536), F32)
    return lo, hi


PEER_WIDTH = 1
PEER_GROUP = 2 * PEER_WIDTH


def _token_groups(tb, idx_ref, tbl_ref, buf_ref, compute, init):
    _gather_rows(idx_ref, 0, tbl_ref, buf_ref, 0)

    def group(i, carry):
        t0 = PEER_GROUP * i
        for bank in range(2):
            nxt = t0 + PEER_WIDTH if bank == 0 else jnp.minimum(t0 + PEER_GROUP, tb - PEER_WIDTH)
            _gather_rows(idx_ref, nxt, tbl_ref, buf_ref, 1 - bank)
            for w in range(PEER_WIDTH):
                pos = bank * PEER_WIDTH + w
                carry = compute(t0 + pos, buf_ref.at[pos], pos, carry)
        return carry

    return lax.fori_loop(0, tb // PEER_GROUP, group, init)


def _peer_act_kernel(tb, idx_ref, h_ref, tbl_ref, act_ref, buf_ref, acc_ref):
    token_lane = lax.broadcasted_iota(jnp.int32, (PEER_SLOTS, tb), 1)
    acc_ref[...] = jnp.zeros_like(acc_ref)

    def finish(t, pos, acts):
        return jnp.where(token_lane == t, jnp.sum(acc_ref[pos], axis=1, keepdims=True), acts)

    def compute(t, buf, pos, acts):
        acts = finish(t - PEER_GROUP, pos, acts)
        hrow = h_ref[pl.ds(t, 1), :]
        acc = None
        for j in range(ROW_PLANES):
            lo, hi = _plane_halves(buf, j)
            term = (lo * hrow[:, j * 128:(j + 1) * 128]
                    + hi * hrow[:, ROW_WORDS + j * 128:ROW_WORDS + (j + 1) * 128])
            acc = term if acc is None else acc + term
        acc_ref[pos] = acc
        return acts

    acts = _token_groups(tb, idx_ref, tbl_ref, buf_ref, compute, jnp.zeros((PEER_SLOTS, tb), F32))
    for pos in range(PEER_GROUP):
        acts = finish(tb - PEER_GROUP + pos, pos, acts)
    act_ref[...] = acts


def _peer_out_kernel(tb, idx_ref, act_ref, gate_ref, tbl_ref, f_ref, buf_ref, coef_ref, col_ref):
    act = act_ref[...]
    coef_ref[...] = gate_ref[...] * (0.5 * act * (1.0 + lax.erf(act * (2.0 ** -0.5))))
    token_lane = lax.broadcasted_iota(jnp.int32, (PEER_SLOTS, tb), 1)

    def stage_coef(t, pos):
        col = jnp.sum(jnp.where(token_lane == t, coef_ref[...], 0.0), axis=1, keepdims=True)
        col_ref[pos] = jnp.broadcast_to(col, (PEER_SLOTS, 128))

    for pos in range(PEER_GROUP):
        stage_coef(pos, pos)

    def compute(t, buf, pos, carry):
        coef = col_ref[pos]
        los, his = [], []
        for j in range(ROW_PLANES):
            lo, hi = _plane_halves(buf, j)
            los.append(jnp.sum(lo * coef, axis=0, keepdims=True))
            his.append(jnp.sum(hi * coef, axis=0, keepdims=True))
        f_ref[pl.ds(t, 1), :] = jnp.concatenate(los + his, axis=1)
        stage_coef(t + PEER_GROUP, pos)
        return carry

    _token_groups(tb, idx_ref, tbl_ref, buf_ref, compute, 0)


def _peer_experts(h2, idx_flat, gate_t, tbl_u, tbl_v, tb):
    n, d = h2.shape
    smem = pl.BlockSpec((tb * PEER_SLOTS,), lambda i: (i,), memory_space=pltpu.SMEM)
    row = pl.BlockSpec((tb, d), lambda i: (i, 0))
    slot_major = pl.BlockSpec((PEER_SLOTS, tb), lambda i: (0, i))
    table = pl.BlockSpec(memory_space=pltpu.VMEM)
    buf = pltpu.VMEM((PEER_GROUP, ROW_PLANES * PLANE_STRIDE, 128), jnp.int32)
    staged = pltpu.VMEM((PEER_GROUP, PEER_SLOTS, 128), F32)
    act_t = pl.pallas_call(
        functools.partial(_peer_act_kernel, tb),
        grid=(n // tb,),
        in_specs=[smem, row, table],
        out_specs=slot_major,
        out_shape=jax.ShapeDtypeStruct((PEER_SLOTS, n), F32),
        scratch_shapes=[buf, staged],
        compiler_params=_cparams("arbitrary"),
        name="peer_act",
    )(idx_flat, h2, tbl_u)
    return pl.pallas_call(
        functools.partial(_peer_out_kernel, tb),
        grid=(n // tb,),
        in_specs=[smem, slot_major, slot_major, table],
        out_specs=row,
        out_shape=jax.ShapeDtypeStruct((n, d), F32),
        scratch_shapes=[buf, pltpu.VMEM((PEER_SLOTS, tb), F32), staged],
        compiler_params=_cparams("arbitrary"),
        name="peer_out",
    )(idx_flat, act_t, gate_t, tbl_v)


def _final_kernel(x1_ref, f_ref, g2_ref, lg_ref, lb_ref, y_ref):
    y_ref[...] = _ln_rows(DN_ALPHA * x1_ref[...] + g2_ref[...] * f_ref[...]) * lg_ref[...] + lb_ref[...]


def _final(x1, f, g2, ln_g, ln_b, rows_per_mod, rb):
    n, d = x1.shape
    row = pl.BlockSpec((rb, d), lambda i: (i, 0))
    g2, g2_spec = _mod_spec(g2, rows_per_mod, rb, d)
    vec = pl.BlockSpec((1, d), lambda i: (0, 0))
    return pl.pallas_call(
        _final_kernel,
        grid=(n // rb,),
        in_specs=[row, row, g2_spec, vec, vec],
        out_specs=row,
        out_shape=jax.ShapeDtypeStruct((n, d), F32),
        compiler_params=_cparams("parallel"),
        name="final_ln",
    )(x1, f, g2, ln_g.reshape(1, d), ln_b.reshape(1, d))


ROW_BLOCK = 256
SCAN_BATCH = 4
SCAN_BLOCK = 256
SB_BLOCK = 1024
SB_KEY_BLOCK = 256
SB_PAGES = 16
ROUTE_BLOCK = 1024
PEER_BLOCK = 128


def _mixer_half(x, mods, shift_prev, wkv0, attend, wts, rows_per_mod, rb):
    bsz, t, d = x.shape
    n = bsz * t
    sh1, sc1, g1, sh2, sc2, _ = mods
    xr = x.reshape(n, d)
    p, q, k, v, kb, vb = _inproj(xr, sh1, sc1, wts["w_in"], rows_per_mod, rb)
    p3 = p.reshape(bsz, t, RWKV_PROJ)
    rw, wkv_new = _rwkv(p3, shift_prev, wkv0, wts["rwkv"], SCAN_BATCH, SCAN_BLOCK)
    sb = attend(q, k, v, kb, vb)
    x1, h2, qp = _outproj(rw.reshape(n, WIDTH), sb.reshape(n, WIDTH), xr, g1, sh2, sc2, wts["w_out"],
                          wts["ln1_g"], wts["ln1_b"], wts["w_pq"], rows_per_mod, rb)
    heads = lambda a: a.reshape(bsz, t, N_HEADS, HEAD_DIM)
    return x1, h2, qp, heads(k), heads(v), wkv_new, p3[:, -1]


def _layer_pair(xp, xs, mod_p, mod_s, cache_k, cache_v, page_table, state_wkv, state_shift, wts):
    bp, tp, d = xp.shape
    bs, ts, _ = xs.shape
    assert ts == 1, "the paged attention handles one new token per sequence"
    n_p = bp * tp

    def attend_p(q, k, v, kb, vb):
        r = lambda a: a.reshape(bp, tp, WIDTH)
        return _sb_prompt(r(q), r(kb), r(vb), wts["sb_bias"], SB_BLOCK, SB_KEY_BLOCK)

    def attend_s(q, k, v, kb, vb):
        heads = lambda a: a.astype(F32).reshape(bs, N_HEADS, HEAD_DIM)
        return _sb_sample(heads(q), heads(k), heads(v), wts["sb_bias"], cache_k, cache_v, page_table, SB_PAGES)

    zero_shift = jnp.zeros((bp, RWKV_PROJ), xp.dtype)
    zero_wkv = jnp.zeros((bp, N_HEADS, HEAD_DIM, HEAD_DIM), state_wkv.dtype)
    x1p, h2p, qpp, kp, vp, wp, sp = _mixer_half(xp, mod_p, zero_shift, zero_wkv, attend_p, wts, tp, ROW_BLOCK)
    x1s, h2s, qps, ks, vs, ws, ss = _mixer_half(xs, mod_s, state_shift, state_wkv, attend_s, wts, 1, bs)

    def channel_mixer(h2, qp):
        n = h2.shape[0]
        eidx, gate = _route(qp, wts["sub_keys"], min(ROUTE_BLOCK, n))
        idx_flat = eidx.transpose(2, 0, 1).reshape(-1) * ROW_PLANES
        return _peer_experts(h2, idx_flat, gate.reshape(PEER_SLOTS, n), wts["peer_u"], wts["peer_v"], PEER_BLOCK)

    yp = _final(x1p, channel_mixer(h2p, qpp), mod_p[5], wts["ln2_g"], wts["ln2_b"], tp, ROW_BLOCK).reshape(bp, tp, d)
    ys = _final(x1s, channel_mixer(h2s, qps), mod_s[5], wts["ln2_g"], wts["ln2_b"], 1, bs).reshape(bs, ts, d)
    return yp, ys, kp, vp, ks, vs, wp, ws, sp, ss


def kernel(x_prompt, x_sample, c_prompt, c_sample, cache_k, cache_v, page_table, state_wkv, state_shift,
           w_cond, b_cond, w_in, mu_shift, w0, w_up, a0, a_up, g_up, k_k, k_a, r_k, lnx_g, lnx_b,
           sb_bias, w_out, ln1_g, ln1_b, w_pq, sub_keys, peer_u, peer_v, ln2_g, ln2_b):
    depth = w_in.shape[0]
    bp, bs = c_prompt.shape[0], c_sample.shape[0]
    pad = (-(bp + bs)) % 8
    c_all = jnp.concatenate([c_prompt, c_sample, jnp.zeros((pad, c_prompt.shape[1]), c_prompt.dtype)], axis=0)
    yp, ys = x_prompt, x_sample
    outs = [[] for _ in range(8)]
    for l in range(depth):
        wts = dict(
            w_in=w_in[l].astype(BF16), w_out=w_out[l].astype(BF16), w_pq=w_pq[l].astype(BF16),
            rwkv=_rwkv_params(mu_shift[l], w0[l], w_up[l], a0[l], a_up[l], g_up[l], k_k[l], k_a[l], r_k[l],
                              lnx_g[l], lnx_b[l]),
            sb_bias=sb_bias[l], ln1_g=ln1_g[l], ln1_b=ln1_b[l], ln2_g=ln2_g[l], ln2_b=ln2_b[l],
            sub_keys=_sub_key_pairs(sub_keys[l]), peer_u=_pack_table(peer_u[l]), peer_v=_pack_table(peer_v[l]))
        mod = _cond(c_all, w_cond[l], b_cond[l])
        mod_p = jnp.split(mod[:bp], N_MOD, axis=-1)
        mod_s = jnp.split(mod[bp:bp + bs], N_MOD, axis=-1)
        res = _layer_pair(yp, ys, mod_p, mod_s, cache_k[l], cache_v[l], page_table, state_wkv[l], state_shift[l], wts)
        yp, ys = res[0], res[1]
        for acc, val in zip(outs, res[2:]):
            acc.append(val)
    return (yp, ys) + tuple(jnp.stack(o) for o in outs)
```
